```python
import jax, jax.numpy as jnp
from jax import lax
import numpy as np

D_MODEL = 1024
BATCH = 16
SEQ = 2048
DEPTH = 4
DEC_BATCH = 32
DEC_SEQ = 64
PAST_LEN = 2048

CHUNK = 64
N_EVEN = (DEPTH + 1) // 2
N_ODD = DEPTH // 2
EPS = 1e-6
W_A = D_MODEL // 2
CONV_A_W = 3
HD_B = 64
W_B = D_MODEL // 2
H_B = W_B // HD_B
PAST_CHUNKS_B = 8
BAND_PAST = PAST_CHUNKS_B * CHUNK
BAND = BAND_PAST + CHUNK
REL_CLIP = 4 * CHUNK
HD_C = 128
H_C = D_MODEL // HD_C
W_C = H_C * HD_C
CONV_C_W = 4
IN_EVEN = 4 * W_A + 4 * W_B
IN_ODD = 4 * W_C + 2 * H_C

kernel_name = 'hybrid_streaming_conv_band_attn_gdn_step'


def rmsnorm(x, g):
    xf = x.astype(jnp.float32)
    y = xf * lax.rsqrt(jnp.mean(xf * xf, axis=-1, keepdims=True) + EPS)
    return (y * g.astype(jnp.float32)).astype(x.dtype)


def l2norm(x):
    return x * lax.rsqrt(jnp.sum(x * x, axis=-1, keepdims=True) + EPS)


def causal_dwconv(u, hist, w):
    width = w.shape[0]
    T = u.shape[1]
    up = jnp.concatenate([hist.astype(u.dtype), u], axis=1)
    out = up[:, 0:T] * w[0]
    for j in range(1, width):
        out = out + up[:, j:j + T] * w[j]
    return out, up[:, up.shape[1] - (width - 1):]


def rel_bias(table, q_pos, k_pos):
    idx = jnp.clip(q_pos[:, None] - k_pos[None, :], -REL_CLIP, REL_CLIP) + REL_CLIP
    return jnp.take(table, idx, axis=1).astype(jnp.float32)


def attend(q, k, v, bias, valid):
    s = jnp.einsum('bqhd,bkhd->bhqk', q, k, preferred_element_type=jnp.float32) * (HD_B ** -0.5) + bias[None]
    s = jnp.where(valid[None, None, None, :], s, -jnp.inf)
    p = jax.nn.softmax(s, axis=-1)
    return jnp.einsum('bhqk,bkhd->bqhd', p.astype(v.dtype), v)


def band_attn_prompt(q, k, v, table):
    Bn, T, H, Dh = q.shape
    nc = T // CHUNK
    pad = ((0, 0), (BAND_PAST, 0), (0, 0), (0, 0))
    kpad = jnp.pad(k, pad)
    vpad = jnp.pad(v, pad)
    qc = jnp.moveaxis(q.reshape(Bn, nc, CHUNK, H, Dh), 1, 0)
    k_off = jnp.arange(BAND) - BAND_PAST
    bias = rel_bias(table, jnp.arange(CHUNK), k_off)

    def one(args):
        qb, c = args
        start = c * CHUNK
        kb = lax.dynamic_slice_in_dim(kpad, start, BAND, axis=1)
        vb = lax.dynamic_slice_in_dim(vpad, start, BAND, axis=1)
        valid = (start + k_off) >= 0
        return attend(qb, kb, vb, bias, valid)

    o = lax.map(one, (qc, jnp.arange(nc)))
    o = jnp.moveaxis(o, 0, 1).reshape(Bn, T, H, Dh)
    keep = min(BAND_PAST, T)
    return o, k[:, T - keep:], v[:, T - keep:]


def band_attn_sample(q, k, v, ck, cv, table):
    T = q.shape[1]
    cb = ck.shape[1]
    kk = jnp.concatenate([ck.astype(k.dtype), k], axis=1)
    vv = jnp.concatenate([cv.astype(v.dtype), v], axis=1)
    bias = rel_bias(table, jnp.arange(T), jnp.arange(cb + T) - cb)
    valid = jnp.ones((cb + T,), dtype=bool)
    return attend(q, kk, vv, bias, valid)


def gdn_chunk(S, q, k, v, g, beta):
    L = q.shape[1]
    gc = jnp.cumsum(g, axis=1)
    gct = jnp.swapaxes(gc, 1, 2)
    incl = jnp.tril(jnp.ones((L, L), dtype=bool))
    strict = jnp.tril(jnp.ones((L, L), dtype=bool), -1)
    decay = jnp.exp(jnp.where(incl, gct[..., :, None] - gct[..., None, :], -jnp.inf))
    bt = jnp.swapaxes(beta, 1, 2)
    kk = jnp.einsum('blhd,bmhd->bhlm', k, k)
    a = jnp.where(strict, bt[..., :, None] * kk * decay, 0.0)
    tmat = a + jnp.eye(L, dtype=a.dtype)
    rhs = jnp.swapaxes(jnp.concatenate([v * beta[..., None], k * (beta * jnp.exp(gc))[..., None]], axis=-1), 1, 2)
    sol = lax.linalg.triangular_solve(tmat, rhs, left_side=True, lower=True, unit_diagonal=True)
    u = sol[..., :HD_C] - jnp.einsum('bhlk,bhkv->bhlv', sol[..., HD_C:], S)
    qk = jnp.einsum('blhd,bmhd->bhlm', q, k) * decay
    o = jnp.einsum('blhk,bhkv->blhv', q * jnp.exp(gc)[..., None], S) + jnp.einsum('bhlm,bhmv->blhv', qk, u)
    g_last = gc[:, -1]
    k_dec = k * jnp.exp(g_last[:, None, :] - gc)[..., None]
    S_new = jnp.exp(g_last)[..., None, None] * S + jnp.einsum('blhk,bhlv->bhkv', k_dec, u)
    return S_new, o


def gated_delta(q, k, v, g, beta, S0):
    Bn, T = q.shape[0], q.shape[1]
    if T <= CHUNK:
        return gdn_chunk(S0, q, k, v, g, beta)
    nc = T // CHUNK

    def split(t):
        return jnp.moveaxis(t.reshape((Bn, nc, CHUNK) + t.shape[2:]), 1, 0)

    def step(S, xs):
        return gdn_chunk(S, *xs)

    S, o = lax.scan(step, S0, (split(q), split(k), split(v), split(g), split(beta)))
    return S, jnp.moveaxis(o, 0, 1).reshape(Bn, T, H_C, HD_C)


def mixer_even(h, conv_hist, ck, cv, w_in, conv_w, table, w_out):
    Bn, T, _ = h.shape
    p = h @ w_in
    a_b, a_c, a_h, a_z, q, k, v, b_z = jnp.split(p, 8, axis=-1)
    conv_out, new_hist = causal_dwconv(a_c * a_h, conv_hist, conv_w)
    ya = a_b * conv_out * jax.nn.silu(a_z)
    q = q.reshape(Bn, T, H_B, HD_B)
    k = k.reshape(Bn, T, H_B, HD_B)
    v = v.reshape(Bn, T, H_B, HD_B)
    if ck is None:
        o, nk, nv = band_attn_prompt(q, k, v, table)
    else:
        o = band_attn_sample(q, k, v, ck, cv, table)
        nk, nv = k, v
    yb = o.reshape(Bn, T, W_B) * jax.nn.silu(b_z)
    y = jnp.concatenate([ya, yb], axis=-1) @ w_out
    return y, new_hist, nk, nv


def mixer_odd(h, conv_hist, S0, w_in, conv_w, a_log, dt_bias, onorm, w_out):
    Bn, T, _ = h.shape
    f32 = jnp.float32
    p = h @ w_in
    qkv, z, b_lg, a_lg = jnp.split(p, [3 * W_C, 4 * W_C, 4 * W_C + H_C], axis=-1)
    qkv, new_hist = causal_dwconv(qkv, conv_hist, conv_w)
    qkv = jax.nn.silu(qkv).astype(f32)
    q, k, v = [t.reshape(Bn, T, H_C, HD_C) for t in jnp.split(qkv, 3, axis=-1)]
    q = l2norm(q) * (HD_C ** -0.5)
    k = l2norm(k)
    beta = jax.nn.sigmoid(b_lg.astype(f32))
    g = -jnp.exp(a_log.astype(f32)) * jax.nn.softplus(a_lg.astype(f32) + dt_bias.astype(f32))
    S, o = gated_delta(q, k, v, g, beta, S0.astype(f32))
    o = rmsnorm(o, onorm).reshape(Bn, T, W_C).astype(h.dtype)
    y = (o * jax.nn.silu(z)) @ w_out
    return y, new_hist, S


def setup_inputs(seed: int = 0) -> dict:
    key = jax.random.key(seed)
    ks = jax.random.split(key, 20)
    cb = min(BAND_PAST, PAST_LEN)

    def nrm(k, shape, s):
        return jax.random.normal(k, shape, jnp.float32) * s

    dt = jnp.exp(jax.random.uniform(ks[18], (N_ODD, H_C), jnp.float32, np.log(1e-3), np.log(1e-1)))
    return {
        'x_prompt': nrm(ks[0], (BATCH, SEQ, D_MODEL), 1.0),
        'x_sample': nrm(ks[1], (DEC_BATCH, DEC_SEQ, D_MODEL), 1.0),
        'cache_conv_a': nrm(ks[2], (N_EVEN, DEC_BATCH, CONV_A_W - 1, W_A), 1.0),
        'cache_k_b': nrm(ks[3], (N_EVEN, DEC_BATCH, cb, H_B, HD_B), 1.0),
        'cache_v_b': nrm(ks[4], (N_EVEN, DEC_BATCH, cb, H_B, HD_B), 1.0),
        'state_conv_c': nrm(ks[5], (N_ODD, DEC_BATCH, CONV_C_W - 1, 3 * W_C), 1.0),
        'state_s_c': nrm(ks[6], (N_ODD, DEC_BATCH, H_C, HD_C, HD_C), 0.1),
        'norm_pre': 1.0 + nrm(ks[7], (DEPTH, D_MODEL), 0.1),
        'norm_post': 1.0 + nrm(ks[8], (DEPTH, D_MODEL), 0.1),
        'w_in_even': nrm(ks[9], (N_EVEN, D_MODEL, IN_EVEN), D_MODEL ** -0.5),
        'conv_w_a': nrm(ks[10], (N_EVEN, CONV_A_W, W_A), CONV_A_W ** -0.5),
        'rel_bias_b': nrm(ks[11], (N_EVEN, H_B, 2 * REL_CLIP + 1), 0.3),
        'w_out_even': nrm(ks[12], (N_EVEN, W_A + W_B, D_MODEL), (W_A + W_B) ** -0.5),
        'w_in_odd': nrm(ks[13], (N_ODD, D_MODEL, IN_ODD), D_MODEL ** -0.5),
        'conv_w_c': nrm(ks[14], (N_ODD, CONV_C_W, 3 * W_C), CONV_C_W ** -0.5),
        'a_log_c': jnp.log(jax.random.uniform(ks[15], (N_ODD, H_C), jnp.float32, 1.0, 16.0)),
        'dt_bias_c': dt + jnp.log(-jnp.expm1(-dt)),
        'out_norm_c': 1.0 + nrm(ks[16], (N_ODD, HD_C), 0.1),
        'w_out_odd': nrm(ks[17], (N_ODD, W_C, D_MODEL), W_C ** -0.5),
    }


def reference(x_prompt, x_sample, cache_conv_a, cache_k_b, cache_v_b, state_conv_c, state_s_c,
              norm_pre, norm_post, w_in_even, conv_w_a, rel_bias_b, w_out_even,
              w_in_odd, conv_w_c, a_log_c, dt_bias_c, out_norm_c, w_out_odd):
    yp, ys = x_prompt, x_sample
    ca_p, kb_p, vb_p, cc_p, sc_p = [], [], [], [], []
    ca_s, kb_s, vb_s, cc_s, sc_s = [], [], [], [], []
    for layer in range(DEPTH):
        i = layer // 2
        if layer % 2 == 0:
            wts = (w_in_even[i], conv_w_a[i], rel_bias_b[i], w_out_even[i])
            hist0 = jnp.zeros((yp.shape[0], CONV_A_W - 1, W_A), yp.dtype)
            out, hst, nk, nv = mixer_even(rmsnorm(yp, norm_pre[layer]), hist0, None, None, *wts)
            yp = yp + rmsnorm(out, norm_post[layer])
            ca_p.append(hst)
            kb_p.append(nk)
            vb_p.append(nv)
            out, hst, nk, nv = mixer_even(rmsnorm(ys, norm_pre[layer]), cache_conv_a[i], cache_k_b[i], cache_v_b[i], *wts)
            ys = ys + rmsnorm(out, norm_post[layer])
            ca_s.append(hst)
            kb_s.append(nk)
            vb_s.append(nv)
        else:
            wts = (w_in_odd[i], conv_w_c[i], a_log_c[i], dt_bias_c[i], out_norm_c[i], w_out_odd[i])
            hist0 = jnp.zeros((yp.shape[0], CONV_C_W - 1, 3 * W_C), yp.dtype)
            s0 = jnp.zeros((yp.shape[0], H_C, HD_C, HD_C), jnp.float32)
            out, hst, S = mixer_odd(rmsnorm(yp, norm_pre[layer]), hist0, s0, *wts)
            yp = yp + rmsnorm(out, norm_post[layer])
            cc_p.append(hst)
            sc_p.append(S.astype(x_prompt.dtype))
            out, hst, S = mixer_odd(rmsnorm(ys, norm_pre[layer]), state_conv_c[i], state_s_c[i], *wts)
            ys = ys + rmsnorm(out, norm_post[layer])
            cc_s.append(hst)
            sc_s.append(S.astype(state_s_c.dtype))
    return (yp, ys,
            jnp.stack(ca_p), jnp.stack(kb_p), jnp.stack(vb_p), jnp.stack(cc_p), jnp.stack(sc_p),
            jnp.stack(ca_s), jnp.stack(kb_s), jnp.stack(vb_s), jnp.stack(cc_s), jnp.stack(sc_s))
```

```python
import functools

import jax
import jax.numpy as jnp
from jax import lax
from jax.experimental import pallas as pl
from jax.experimental.pallas import tpu as pltpu

F32 = jnp.float32
BF16 = jnp.bfloat16

EPS = 1e-6
CHUNK = 64
BAND_PAST = 8 * CHUNK
BAND = BAND_PAST + CHUNK
REL_CLIP = 4 * CHUNK
HD_B = 64
HD_C = 128
LANES = 128
SUBLANES = 8
VMEM_LIMIT_BYTES = 48 * 1024 * 1024
ROW_TILE = 512


def _silu(x):
    return x * (1.0 / (1.0 + jnp.exp(-x)))


def _softplus(x):
    return jnp.maximum(x, 0.0) + jnp.log1p(jnp.exp(-jnp.abs(x)))


def _params(*sem):
    return pltpu.CompilerParams(dimension_semantics=sem, vmem_limit_bytes=VMEM_LIMIT_BYTES)


def _norm_proj_kernel(x_ref, g_ref, w_ref, *rest, n_main, col_tile):
    x = x_ref[...]
    ms = jnp.mean(x * x, axis=-1, keepdims=True)
    h = (x * lax.rsqrt(ms + EPS) * g_ref[...]).astype(BF16)
    if len(rest) == 3:
        wg_ref, o_ref, og_ref = rest
        og_ref[...] = jnp.dot(h, wg_ref[...], preferred_element_type=F32)
    else:
        (o_ref,) = rest
    for j in range(n_main // col_tile):
        cols = slice(j * col_tile, (j + 1) * col_tile)
        o_ref[:, cols] = jnp.dot(h, w_ref[:, cols], preferred_element_type=F32).astype(o_ref.dtype)


def _norm_proj(x2, gain, w_bf, wg_bf=None):
    m, d = x2.shape
    n_main = w_bf.shape[1]
    tm = min(m, ROW_TILE)
    in_specs = [pl.BlockSpec((tm, d), lambda i: (i, 0)),
                pl.BlockSpec((1, d), lambda i: (0, 0)),
                pl.BlockSpec((d, n_main), lambda i: (0, 0))]
    args = [x2, gain.reshape(1, d), w_bf]
    out_shape = [jax.ShapeDtypeStruct((m, n_main), BF16)]
    out_specs = [pl.BlockSpec((tm, n_main), lambda i: (i, 0))]
    if wg_bf is not None:
        in_specs.append(pl.BlockSpec((d, LANES), lambda i: (0, 0)))
        args.append(wg_bf)
        out_shape.append(jax.ShapeDtypeStruct((m, LANES), F32))
        out_specs.append(pl.BlockSpec((tm, LANES), lambda i: (i, 0)))
    res = pl.pallas_call(
        functools.partial(_norm_proj_kernel, n_main=n_main, col_tile=1024),
        grid=(m // tm,), in_specs=in_specs, out_specs=out_specs, out_shape=out_shape,
        compiler_params=_params("parallel"), name="norm_proj")(*args)
    return res if wg_bf is not None else res[0]


def _out_proj_kernel(y_ref, w_ref, g_ref, x_ref, o_ref):
    out = jnp.dot(y_ref[...], w_ref[...], preferred_element_type=F32)
    ms = jnp.mean(out * out, axis=-1, keepdims=True)
    o_ref[...] = x_ref[...] + out * lax.rsqrt(ms + EPS) * g_ref[...]


def _out_proj(y2, w_bf, gain, x2):
    m, k = y2.shape
    d = w_bf.shape[1]
    tm = min(m, ROW_TILE)
    return pl.pallas_call(
        _out_proj_kernel, grid=(m // tm,),
        in_specs=[pl.BlockSpec((tm, k), lambda i: (i, 0)),
                  pl.BlockSpec((k, d), lambda i: (0, 0)),
                  pl.BlockSpec((1, d), lambda i: (0, 0)),
                  pl.BlockSpec((tm, d), lambda i: (i, 0))],
        out_specs=pl.BlockSpec((tm, d), lambda i: (i, 0)),
        out_shape=jax.ShapeDtypeStruct((m, d), F32),
        compiler_params=_params("parallel"), name="out_proj")(y2, w_bf, gain.reshape(1, d), x2)


def _even_mixer_kernel(*refs, tq, has_cache):
    ab, ac, ah, az, q_ref, k_ref, v_ref, bz_ref, cw_ref, bias_ref = refs[:10]
    if has_cache:
        hist_ref, kc_ref, vc_ref = refs[10:13]
        refs = refs[13:]
    else:
        refs = refs[10:]
    y_ref, utail_ref, ubuf, kbuf, vbuf = refs
    w = ab.shape[-1]
    i = pl.program_id(1)

    @pl.when(i == 0)
    def _init():
        ubuf[0:SUBLANES, :] = jnp.zeros((SUBLANES, w), F32)
        if has_cache:
            ubuf[SUBLANES - 2:SUBLANES, :] = hist_ref[0]
            kbuf[0:BAND_PAST, :] = kc_ref[0]
            vbuf[0:BAND_PAST, :] = vc_ref[0]
        else:
            kbuf[0:BAND_PAST, :] = jnp.zeros((BAND_PAST, w), BF16)
            vbuf[0:BAND_PAST, :] = jnp.zeros((BAND_PAST, w), BF16)

    u = ac[0].astype(F32) * ah[0].astype(F32)
    ubuf[SUBLANES:SUBLANES + tq, :] = u
    conv = (ubuf[SUBLANES - 2:SUBLANES - 2 + tq, :] * cw_ref[0:1, :]
            + ubuf[SUBLANES - 1:SUBLANES - 1 + tq, :] * cw_ref[1:2, :]
            + u * cw_ref[2:3, :])
    ya = ab[0].astype(F32) * conv * _silu(az[0].astype(F32))
    y_ref[0, :, 0:w] = ya.astype(y_ref.dtype)
    tail = ubuf[tq:tq + SUBLANES, :]
    utail_ref[0] = tail
    ubuf[0:SUBLANES, :] = tail

    kbuf[BAND_PAST:BAND_PAST + tq, :] = k_ref[0]
    vbuf[BAND_PAST:BAND_PAST + tq, :] = v_ref[0]
    n_chunks = tq // CHUNK
    lane = lax.broadcasted_iota(jnp.int32, (CHUNK, LANES), 1)
    first_head = lane < HD_B
    key_idx = lax.broadcasted_iota(jnp.int32, (2 * CHUNK, BAND), 1)

    def chunk(c, carry):
        r0 = pl.multiple_of(c * CHUNK, CHUNK)
        qc = q_ref[0, pl.ds(r0, CHUNK), :]
        outs = []
        for hp in range(w // LANES):
            cols = slice(hp * LANES, (hp + 1) * LANES)
            qp = qc[:, cols] * (HD_B ** -0.5)
            zero = jnp.zeros_like(qp)
            q2 = jnp.concatenate([jnp.where(first_head, qp, zero), jnp.where(first_head, zero, qp)], axis=0)
            kp = kbuf[pl.ds(r0, BAND), cols]
            vp = vbuf[pl.ds(r0, BAND), cols]
            s = lax.dot_general(q2, kp, (((1,), (1,)), ((), ())), preferred_element_type=F32)
            s = s + bias_ref[hp]
            if not has_cache:
                first_valid = BAND_PAST - CHUNK * (i * n_chunks + c)
                s = jnp.where(key_idx >= first_valid, s, -jnp.inf)
            m = jnp.max(s, axis=-1, keepdims=True)
            e = jnp.exp(s - m)
            l = jnp.sum(e, axis=-1, keepdims=True)
            pv = jnp.dot(e.astype(BF16), vp, preferred_element_type=F32) / l
            outs.append(jnp.where(first_head, pv[0:CHUNK], pv[CHUNK:2 * CHUNK]))
        o = jnp.concatenate(outs, axis=1)
        yb = o * _silu(bz_ref[0, pl.ds(r0, CHUNK), :].astype(F32))
        y_ref[0, pl.ds(r0, CHUNK), w:2 * w] = yb.astype(y_ref.dtype)
        return carry

    lax.fori_loop(0, n_chunks, chunk, 0)
    kbuf[0:BAND_PAST, :] = kbuf[tq:tq + BAND_PAST, :]
    vbuf[0:BAND_PAST, :] = vbuf[tq:tq + BAND_PAST, :]


def _even_mixer(p, conv_w, bias2, hist, kc, vc):
    b, t, n = p.shape
    w = n // 8
    tq = min(t, ROW_TILE)
    has_cache = kc is not None

    def col(j):
        return pl.BlockSpec((1, tq, w), lambda bi, i, j=j: (bi, i, j))

    in_specs = [col(j) for j in range(8)]
    in_specs += [pl.BlockSpec(conv_w.shape, lambda bi, i: (0, 0)),
                 pl.BlockSpec(bias2.shape, lambda bi, i: (0, 0, 0))]
    args = [p] * 8 + [conv_w, bias2]
    if has_cache:
        in_specs += [pl.BlockSpec((1,) + hist.shape[1:], lambda bi, i: (bi, 0, 0)),
                     pl.BlockSpec((1, BAND_PAST, w), lambda bi, i: (bi, 0, 0)),
                     pl.BlockSpec((1, BAND_PAST, w), lambda bi, i: (bi, 0, 0))]
        args += [hist, kc, vc]
    y, utail = pl.pallas_call(
        functools.partial(_even_mixer_kernel, tq=tq, has_cache=has_cache),
        grid=(b, t // tq), in_specs=in_specs,
        out_specs=[pl.BlockSpec((1, tq, 2 * w), lambda bi, i: (bi, i, 0)),
                   pl.BlockSpec((1, SUBLANES, w), lambda bi, i: (bi, 0, 0))],
        out_shape=[jax.ShapeDtypeStruct((b, t, 2 * w), BF16),
                   jax.ShapeDtypeStruct((b, SUBLANES, w), F32)],
        scratch_shapes=[pltpu.VMEM((SUBLANES + tq, w), F32),
                        pltpu.VMEM((BAND_PAST + tq, w), BF16),
                        pltpu.VMEM((BAND_PAST + tq, w), BF16)],
        compiler_params=_params("parallel", "arbitrary"), name="even_mixer")(*args)
    return y, utail


def _cumsum_rows(x):
    n = x.shape[0]
    row = lax.broadcasted_iota(jnp.int32, x.shape, 0)
    shift = 1
    while shift < n:
        x = x + jnp.where(row >= shift, pltpu.roll(x, shift, axis=0), 0.0)
        shift *= 2
    return x


def _unit_lower_inverse(a):
    n = a.shape[0]
    row = lax.broadcasted_iota(jnp.int32, (n, n), 0)
    col = lax.broadcasted_iota(jnp.int32, (n, n), 1)
    x = -a
    p = jnp.where(row == col, 1.0, 0.0) + x
    xb = x.astype(BF16)
    y = jnp.dot(xb, xb, preferred_element_type=F32)
    power = 2
    while 2 * power < n:
        yb = y.astype(BF16)
        st = jnp.dot(jnp.concatenate([p, y], axis=0).astype(BF16), yb, preferred_element_type=F32)
        p = p + st[0:n]
        y = st[n:2 * n]
        power *= 2
    return p + jnp.dot(p.astype(BF16), y.astype(BF16), preferred_element_type=F32)


def _gdn_kernel(*refs, tt, has_state):
    qr, kr, vr, zr, gr, cw_ref, ab_ref, on_ref = refs[:8]
    if has_state:
        hist_ref, s0_ref = refs[8:10]
        refs = refs[10:]
    else:
        refs = refs[8:]
    y_ref, sout_ref, cbuf, qs, ks, vs, gsc, bsc, sbuf = refs
    w = qr.shape[-1]
    n_heads = w // HD_C
    i = pl.program_id(1)

    @pl.when(i == 0)
    def _init():
        cbuf[:, 0:SUBLANES, :] = jnp.zeros((3, SUBLANES, w), F32)
        if has_state:
            for g in range(3):
                cbuf[g, SUBLANES - 3:SUBLANES, :] = hist_ref[0, :, g * w:(g + 1) * w]
            sbuf[...] = s0_ref[0]
        else:
            sbuf[...] = jnp.zeros(sbuf.shape, F32)

    for g, (raw_ref, dst) in enumerate(((qr, qs), (kr, ks), (vr, vs))):
        cbuf[g, SUBLANES:SUBLANES + tt, :] = raw_ref[0].astype(F32)
        cw = cw_ref[:, g * w:(g + 1) * w]
        x = (cbuf[g, SUBLANES - 3:SUBLANES - 3 + tt, :] * cw[0:1]
             + cbuf[g, SUBLANES - 2:SUBLANES - 2 + tt, :] * cw[1:2]
             + cbuf[g, SUBLANES - 1:SUBLANES - 1 + tt, :] * cw[2:3]
             + cbuf[g, SUBLANES:SUBLANES + tt, :] * cw[3:4])
        x = _silu(x)
        cbuf[g, 0:SUBLANES, :] = cbuf[g, tt:tt + SUBLANES, :]
        if g == 2:
            dst[...] = x
        else:
            scale = HD_C ** -0.5 if g == 0 else 1.0
            for h in range(n_heads):
                cols = slice(h * HD_C, (h + 1) * HD_C)
                xh = x[:, cols]
                ss = jnp.sum(xh * xh, axis=-1, keepdims=True)
                dst[:, cols] = xh * (lax.rsqrt(ss + EPS) * scale)

    gl = gr[0]
    bsc[...] = 1.0 / (1.0 + jnp.exp(-gl))
    gsc[...] = -jnp.exp(ab_ref[0:1, :]) * _softplus(gl + ab_ref[1:2, :])

    row = lax.broadcasted_iota(jnp.int32, (CHUNK, CHUNK), 0)
    col = lax.broadcasted_iota(jnp.int32, (CHUNK, CHUNK), 1)
    incl = row >= col
    strict = row > col

    def chunk(n, carry):
        r0 = pl.multiple_of(n * CHUNK, CHUNK)
        rows = pl.ds(r0, CHUNK)
        gc = _cumsum_rows(gsc[rows, :])
        gct = gc.T
        bet = bsc[rows, :]
        g_last = gc[CHUNK - 1:CHUNK, :]
        eg_all = jnp.exp(gc)
        kd_all = jnp.exp(g_last - gc)
        egl_all = jnp.exp(g_last)
        for h in range(n_heads):
            cols = slice(h * HD_C, (h + 1) * HD_C)
            gl_ = n_heads + h
            b = bet[:, h:h + 1]
            eg = eg_all[:, gl_:gl_ + 1]
            kd = kd_all[:, gl_:gl_ + 1]
            egl = egl_all[:, gl_:gl_ + 1]
            decay = jnp.exp(jnp.where(incl, gc[:, gl_:gl_ + 1] - gct[gl_:gl_ + 1, :], -jnp.inf))
            qh = qs[rows, cols]
            kh = ks[rows, cols]
            vh = vs[rows, cols]
            khb = kh.astype(BF16)
            qk_kk = lax.dot_general(jnp.concatenate([qh.astype(BF16), khb], axis=0), khb,
                                    (((1,), (1,)), ((), ())), preferred_element_type=F32)
            qk = qk_kk[0:CHUNK] * decay
            a = jnp.where(strict, b * qk_kk[CHUNK:2 * CHUNK] * decay, 0.0)
            tinv = _unit_lower_inverse(a)
            rhs = jnp.concatenate([vh * b, kh * (b * eg)], axis=1).astype(BF16)
            sol = jnp.dot(tinv.astype(BF16), rhs, preferred_element_type=F32)
            s_h = sbuf[h]
            ws_qs = jnp.dot(jnp.concatenate([sol[:, HD_C:], qh * eg], axis=0).astype(BF16),
                            s_h.astype(BF16), preferred_element_type=F32)
            u = sol[:, 0:HD_C] - ws_qs[0:CHUNK]
            ub = u.astype(BF16)
            o = ws_qs[CHUNK:2 * CHUNK] + jnp.dot(qk.astype(BF16), ub, preferred_element_type=F32)
            kdt = (kh * kd).T.astype(BF16)
            sbuf[h] = egl * s_h + jnp.dot(kdt, ub, preferred_element_type=F32)
            ms = jnp.mean(o * o, axis=-1, keepdims=True)
            on = o * lax.rsqrt(ms + EPS) * on_ref[...]
            y_ref[0, rows, cols] = (on * _silu(zr[0, rows, cols].astype(F32))).astype(y_ref.dtype)
        return carry

    lax.fori_loop(0, tt // CHUNK, chunk, 0)
    sout_ref[0] = sbuf[...]


def _gdn_mixer(p, gates, conv_w, ab, onorm, hist, s0):
    b, t, n = p.shape
    w = n // 4
    n_heads = w // HD_C
    tt = min(t, ROW_TILE)
    has_state = s0 is not None

    def col(j):
        return pl.BlockSpec((1, tt, w), lambda bi, i, j=j: (bi, i, j))

    in_specs = [col(j) for j in range(4)]
    in_specs += [pl.BlockSpec((1, tt, LANES), lambda bi, i: (bi, i, 0)),
                 pl.BlockSpec(conv_w.shape, lambda bi, i: (0, 0)),
                 pl.BlockSpec(ab.shape, lambda bi, i: (0, 0)),
                 pl.BlockSpec(onorm.shape, lambda bi, i: (0, 0))]
    args = [p] * 4 + [gates, conv_w, ab, onorm]
    if has_state:
        in_specs += [pl.BlockSpec((1,) + hist.shape[1:], lambda bi, i: (bi, 0, 0)),
                     pl.BlockSpec((1, n_heads, HD_C, HD_C), lambda bi, i: (bi, 0, 0, 0))]
        args += [hist, s0]
    y, s_new = pl.pallas_call(
        functools.partial(_gdn_kernel, tt=tt, has_state=has_state),
        grid=(b, t // tt), in_specs=in_specs,
        out_specs=[pl.BlockSpec((1, tt, w), lambda bi, i: (bi, i, 0)),
                   pl.BlockSpec((1, n_heads, HD_C, HD_C), lambda bi, i: (bi, 0, 0, 0))],
        out_shape=[jax.ShapeDtypeStruct((b, t, w), BF16),
                   jax.ShapeDtypeStruct((b, n_heads, HD_C, HD_C), F32)],
        scratch_shapes=[pltpu.VMEM((3, SUBLANES + tt, w), F32),
                        pltpu.VMEM((tt, w), F32), pltpu.VMEM((tt, w), F32), pltpu.VMEM((tt, w), F32),
                        pltpu.VMEM((tt, LANES), F32), pltpu.VMEM((tt, LANES), F32),
                        pltpu.VMEM((n_heads, HD_C, HD_C), F32)],
        compiler_params=_params("parallel", "arbitrary"), name="gdn_mixer")(*args)
    return y, s_new


def _band_bias(table):
    q_pos = jnp.arange(CHUNK)
    k_pos = jnp.arange(BAND) - BAND_PAST
    idx = jnp.clip(q_pos[:, None] - k_pos[None, :], -REL_CLIP, REL_CLIP) + REL_CLIP
    bias = jnp.take(table, idx, axis=1).astype(F32)
    return bias.reshape(table.shape[0] // 2, 2 * CHUNK, BAND)


def _even_layer(x, hist, kc, vc, g_pre, g_post, w_in_bf, conv_w, bias2, w_out_bf):
    b, t, d = x.shape
    p = _norm_proj(x.reshape(b * t, d), g_pre, w_in_bf).reshape(b, t, -1)
    w = p.shape[-1] // 8
    if kc is not None:
        kc = kc.reshape(b, BAND_PAST, w).astype(BF16)
        vc = vc.reshape(b, BAND_PAST, w).astype(BF16)
    y, utail = _even_mixer(p, conv_w, bias2, hist, kc, vc)
    x_new = _out_proj(y.reshape(b * t, -1), w_out_bf, g_post, x.reshape(b * t, d)).reshape(b, t, d)
    keep = min(BAND_PAST, t)
    n_heads = w // HD_B
    new_k = p[:, t - keep:, 5 * w:6 * w].astype(F32).reshape(b, keep, n_heads, HD_B)
    new_v = p[:, t - keep:, 6 * w:7 * w].astype(F32).reshape(b, keep, n_heads, HD_B)
    return x_new, utail[:, SUBLANES - 2:], new_k, new_v


def _odd_layer(x, hist, s0, g_pre, g_post, w_main_bf, w_gate_bf, conv_w, ab, onorm, w_out_bf):
    b, t, d = x.shape
    p, gates = _norm_proj(x.reshape(b * t, d), g_pre, w_main_bf, w_gate_bf)
    p = p.reshape(b, t, -1)
    w = p.shape[-1] // 4
    y, s_new = _gdn_mixer(p, gates.reshape(b, t, LANES), conv_w, ab, onorm, hist, s0)
    x_new = _out_proj(y.reshape(b * t, -1), w_out_bf, g_post, x.reshape(b * t, d)).reshape(b, t, d)
    new_hist = p[:, t - 3:, 0:3 * w].astype(F32)
    return x_new, new_hist, s_new


def kernel(x_prompt, x_sample, cache_conv_a, cache_k_b, cache_v_b, state_conv_c, state_s_c, norm_pre, norm_post, w_in_even, conv_w_a, rel_bias_b, w_out_even, w_in_odd, conv_w_c, a_log_c, dt_bias_c, out_norm_c, w_out_odd):
    depth = norm_pre.shape[0]
    yp, ys = x_prompt, x_sample
    outs_p = [[] for _ in range(5)]
    outs_s = [[] for _ in range(5)]
    for layer in range(depth):
        i = layer // 2
        g_pre, g_post = norm_pre[layer], norm_post[layer]
        if layer % 2 == 0:
            w_in_bf = w_in_even[i].astype(BF16)
            w_out_bf = w_out_even[i].astype(BF16)
            bias2 = _band_bias(rel_bias_b[i])
            shared = (g_pre, g_post, w_in_bf, conv_w_a[i], bias2, w_out_bf)
            yp, hst, nk, nv = _even_layer(yp, None, None, None, *shared)
            for lst, val in zip(outs_p[:3], (hst, nk, nv)):
                lst.append(val)
            ys, hst, nk, nv = _even_layer(ys, cache_conv_a[i], cache_k_b[i], cache_v_b[i], *shared)
            for lst, val in zip(outs_s[:3], (hst, nk, nv)):
                lst.append(val)
        else:
            w_in = w_in_odd[i]
            n_heads = a_log_c.shape[1]
            n_main = w_in.shape[1] - 2 * n_heads
            w_main_bf = w_in[:, :n_main].astype(BF16)
            w_gate_bf = jnp.pad(w_in[:, n_main:], ((0, 0), (0, LANES - 2 * n_heads))).astype(BF16)
            pad = (n_heads, LANES - 2 * n_heads)
            ab = jnp.stack([jnp.pad(a_log_c[i], pad), jnp.pad(dt_bias_c[i], pad)]).astype(F32)
            shared = (g_pre, g_post, w_main_bf, w_gate_bf, conv_w_c[i], ab, out_norm_c[i].reshape(1, -1),
                      w_out_odd[i].astype(BF16))
            yp, hst, s_new = _odd_layer(yp, None, None, *shared)
            outs_p[3].append(hst)
            outs_p[4].append(s_new)
            ys, hst, s_new = _odd_layer(ys, state_conv_c[i], state_s_c[i], *shared)
            outs_s[3].append(hst)
            outs_s[4].append(s_new)
    return (yp, ys) + tuple(jnp.stack(v) for v in outs_p) + tuple(jnp.stack(v) for v in outs_s)
```

```python
import functools
import math

import jax
import jax.numpy as jnp
from jax import lax
from jax.experimental import pallas as pl
from jax.experimental.pallas import tpu as pltpu

F32 = jnp.float32
BF16 = jnp.bfloat16

EPS = 1e-6
LOG2E = math.log2(math.e)
CHUNK = 64
BAND_PAST = 8 * CHUNK
BAND = BAND_PAST + CHUNK
REL_CLIP = 4 * CHUNK
HD_B = 64
HD_C = 128
LANES = 128
SUBLANES = 8
VMEM_LIMIT_BYTES = 48 * 1024 * 1024
ROW_TILE = 512


def _silu(x):
    return x * (1.0 / (1.0 + jnp.exp(-x)))


def _softplus(x):
    return jnp.maximum(x, 0.0) + jnp.log1p(jnp.exp(-jnp.abs(x)))


def _params(*sem):
    return pltpu.CompilerParams(dimension_semantics=sem, vmem_limit_bytes=VMEM_LIMIT_BYTES)


def _norm_proj_kernel(x_ref, g_ref, w_ref, *rest, n_main, col_tile):
    x = x_ref[...]
    ms = jnp.mean(x * x, axis=-1, keepdims=True)
    h = (x * lax.rsqrt(ms + EPS) * g_ref[...]).astype(BF16)
    if len(rest) == 3:
        wg_ref, o_ref, og_ref = rest
        og_ref[...] = jnp.dot(h, wg_ref[...], preferred_element_type=F32)
    else:
        (o_ref,) = rest
    for j in range(n_main // col_tile):
        cols = slice(j * col_tile, (j + 1) * col_tile)
        o_ref[:, cols] = jnp.dot(h, w_ref[:, cols], preferred_element_type=F32).astype(o_ref.dtype)


def _norm_proj(x2, gain, w_bf, wg_bf=None):
    m, d = x2.shape
    n_main = w_bf.shape[1]
    tm = min(m, ROW_TILE)
    in_specs = [pl.BlockSpec((tm, d), lambda i: (i, 0)),
                pl.BlockSpec((1, d), lambda i: (0, 0)),
                pl.BlockSpec((d, n_main), lambda i: (0, 0))]
    args = [x2, gain.reshape(1, d), w_bf]
    out_shape = [jax.ShapeDtypeStruct((m, n_main), BF16)]
    out_specs = [pl.BlockSpec((tm, n_main), lambda i: (i, 0))]
    if wg_bf is not None:
        in_specs.append(pl.BlockSpec((d, LANES), lambda i: (0, 0)))
        args.append(wg_bf)
        out_shape.append(jax.ShapeDtypeStruct((m, LANES), F32))
        out_specs.append(pl.BlockSpec((tm, LANES), lambda i: (i, 0)))
    res = pl.pallas_call(
        functools.partial(_norm_proj_kernel, n_main=n_main, col_tile=1024),
        grid=(m // tm,), in_specs=in_specs, out_specs=out_specs, out_shape=out_shape,
        compiler_params=_params("parallel"), name="norm_proj")(*args)
    return res if wg_bf is not None else res[0]


def _out_proj_kernel(y_ref, w_ref, g_ref, x_ref, o_ref):
    out = jnp.dot(y_ref[...], w_ref[...], preferred_element_type=F32)
    ms = jnp.mean(out * out, axis=-1, keepdims=True)
    o_ref[...] = x_ref[...] + out * lax.rsqrt(ms + EPS) * g_ref[...]


def _out_proj(y2, w_bf, gain, x2):
    m, k = y2.shape
    d = w_bf.shape[1]
    tm = min(m, ROW_TILE)
    return pl.pallas_call(
        _out_proj_kernel, grid=(m // tm,),
        in_specs=[pl.BlockSpec((tm, k), lambda i: (i, 0)),
                  pl.BlockSpec((k, d), lambda i: (0, 0)),
                  pl.BlockSpec((1, d), lambda i: (0, 0)),
                  pl.BlockSpec((tm, d), lambda i: (i, 0))],
        out_specs=pl.BlockSpec((tm, d), lambda i: (i, 0)),
        out_shape=jax.ShapeDtypeStruct((m, d), F32),
        compiler_params=_params("parallel"), name="out_proj")(y2, w_bf, gain.reshape(1, d), x2)


def _even_mixer_kernel(*refs, tq, has_cache):
    ab, ac, ah, az, q_ref, k_ref, v_ref, bz_ref, cw_ref, bias_ref = refs[:10]
    if has_cache:
        hist_ref, kc_ref, vc_ref = refs[10:13]
        refs = refs[13:]
    else:
        refs = refs[10:]
    y_ref, utail_ref, ubuf, kbuf, vbuf = refs
    w = ab.shape[-1]
    i = pl.program_id(1)

    @pl.when(i == 0)
    def _init():
        ubuf[0:SUBLANES, :] = jnp.zeros((SUBLANES, w), F32)
        if has_cache:
            ubuf[SUBLANES - 2:SUBLANES, :] = hist_ref[0]
            kbuf[0:BAND_PAST, :] = kc_ref[0]
            vbuf[0:BAND_PAST, :] = vc_ref[0]
        else:
            kbuf[0:BAND_PAST, :] = jnp.zeros((BAND_PAST, w), BF16)
            vbuf[0:BAND_PAST, :] = jnp.zeros((BAND_PAST, w), BF16)

    u = ac[0].astype(F32) * ah[0].astype(F32)
    ubuf[SUBLANES:SUBLANES + tq, :] = u
    conv = (ubuf[SUBLANES - 2:SUBLANES - 2 + tq, :] * cw_ref[0:1, :]
            + ubuf[SUBLANES - 1:SUBLANES - 1 + tq, :] * cw_ref[1:2, :]
            + u * cw_ref[2:3, :])
    ya = ab[0].astype(F32) * conv * _silu(az[0].astype(F32))
    y_ref[0, :, 0:w] = ya.astype(y_ref.dtype)
    tail = ubuf[tq:tq + SUBLANES, :]
    utail_ref[0] = tail
    ubuf[0:SUBLANES, :] = tail

    kbuf[BAND_PAST:BAND_PAST + tq, :] = k_ref[0]
    vbuf[BAND_PAST:BAND_PAST + tq, :] = v_ref[0]
    n_chunks = tq // CHUNK
    n_pairs = w // LANES
    lane = lax.broadcasted_iota(jnp.int32, (CHUNK, LANES), 1)
    first_head = lane < HD_B

    def attend(first_tile):
        def band_start(c):
            missing = max(BAND_PAST - CHUNK * c, 0) if first_tile else 0
            return missing // LANES * LANES, missing % LANES

        def scores(c, hp):
            cols = slice(hp * LANES, (hp + 1) * LANES)
            lo, masked = band_start(c)
            qp = q_ref[0, c * CHUNK:(c + 1) * CHUNK, cols]
            zero = jnp.zeros_like(qp)
            q2 = jnp.concatenate([jnp.where(first_head, qp, zero), jnp.where(first_head, zero, qp)], axis=0)
            kp = kbuf[c * CHUNK + lo:c * CHUNK + BAND, cols]
            s = lax.dot_general(q2, kp, (((1,), (1,)), ((), ())), preferred_element_type=F32)
            s = s + bias_ref[hp, :, lo:BAND]
            if masked:
                key_idx = lax.broadcasted_iota(jnp.int32, s.shape, 1)
                s = jnp.where(key_idx >= masked, s, -jnp.inf)
            return s

        def weighted_values(c, hp, s):
            cols = slice(hp * LANES, (hp + 1) * LANES)
            lo, _ = band_start(c)
            m = jnp.max(s, axis=-1, keepdims=True)
            e = jnp.exp2(s - m)
            l = jnp.sum(e, axis=-1, keepdims=True)
            vp = vbuf[c * CHUNK + lo:c * CHUNK + BAND, cols]
            pv = jnp.dot(e.astype(BF16), vp, preferred_element_type=F32) / l
            return jnp.where(first_head, pv[0:CHUNK], pv[CHUNK:2 * CHUNK])

        items = [(c, hp) for c in range(n_chunks) for hp in range(n_pairs)]
        skew = 3
        pending, outs = {}, {}
        for step in range(len(items) + skew):
            if step < len(items):
                pending[step] = scores(*items[step])
            j = step - skew
            if j >= 0:
                c, hp = items[j]
                outs[hp] = weighted_values(c, hp, pending.pop(j))
                if hp == n_pairs - 1:
                    rows = slice(c * CHUNK, (c + 1) * CHUNK)
                    o = jnp.concatenate([outs[x] for x in range(n_pairs)], axis=1)
                    yb = o * _silu(bz_ref[0, rows, :].astype(F32))
                    y_ref[0, rows, w:2 * w] = yb.astype(y_ref.dtype)

    if has_cache:
        attend(False)
    else:
        pl.when(i == 0)(functools.partial(attend, True))
        pl.when(i > 0)(functools.partial(attend, False))
    kbuf[0:BAND_PAST, :] = kbuf[tq:tq + BAND_PAST, :]
    vbuf[0:BAND_PAST, :] = vbuf[tq:tq + BAND_PAST, :]


def _even_mixer(p, conv_w, bias2, hist, kc, vc):
    b, t, n = p.shape
    w = n // 8
    tq = min(t, ROW_TILE)
    has_cache = kc is not None

    def col(j):
        return pl.BlockSpec((1, tq, w), lambda bi, i, j=j: (bi, i, j))

    in_specs = [col(j) for j in range(8)]
    in_specs += [pl.BlockSpec(conv_w.shape, lambda bi, i: (0, 0)),
                 pl.BlockSpec(bias2.shape, lambda bi, i: (0, 0, 0))]
    args = [p] * 8 + [conv_w, bias2]
    if has_cache:
        in_specs += [pl.BlockSpec((1,) + hist.shape[1:], lambda bi, i: (bi, 0, 0)),
                     pl.BlockSpec((1, BAND_PAST, w), lambda bi, i: (bi, 0, 0)),
                     pl.BlockSpec((1, BAND_PAST, w), lambda bi, i: (bi, 0, 0))]
        args += [hist, kc, vc]
    y, utail = pl.pallas_call(
        functools.partial(_even_mixer_kernel, tq=tq, has_cache=has_cache),
        grid=(b, t // tq), in_specs=in_specs,
        out_specs=[pl.BlockSpec((1, tq, 2 * w), lambda bi, i: (bi, i, 0)),
                   pl.BlockSpec((1, SUBLANES, w), lambda bi, i: (bi, 0, 0))],
        out_shape=[jax.ShapeDtypeStruct((b, t, 2 * w), BF16),
                   jax.ShapeDtypeStruct((b, SUBLANES, w), F32)],
        scratch_shapes=[pltpu.VMEM((SUBLANES + tq, w), F32),
                        pltpu.VMEM((BAND_PAST + tq, w), BF16),
                        pltpu.VMEM((BAND_PAST + tq, w), BF16)],
        compiler_params=_params("parallel", "arbitrary"), name="even_mixer")(*args)
    return y, utail


def _cumsum_rows(x):
    n = x.shape[0]
    row = lax.broadcasted_iota(jnp.int32, x.shape, 0)
    shift = 1
    while shift < n:
        x = x + jnp.where(row >= shift, pltpu.roll(x, shift, axis=0), 0.0)
        shift *= 2
    return x


def _dot(a, b):
    return jnp.dot(a.astype(BF16), b.astype(BF16), preferred_element_type=F32)


def _unit_lower_inverses(mats):
    n = mats[0].shape[0]
    row = lax.broadcasted_iota(jnp.int32, (n, n), 0)
    col = lax.broadcasted_iota(jnp.int32, (n, n), 1)
    eye = jnp.where(row == col, 1.0, 0.0)
    xs = [-a for a in mats]
    ps = [eye + x for x in xs]
    ys = [_dot(x, x) for x in xs]
    power = 2
    while 2 * power < n:
        sts = [_dot(jnp.concatenate([p, y], axis=0), y) for p, y in zip(ps, ys)]
        ps = [p + st[0:n] for p, st in zip(ps, sts)]
        ys = [st[n:2 * n] for st in sts]
        power *= 2
    return [p + _dot(p, y) for p, y in zip(ps, ys)]


def _gdn_kernel(*refs, tt, has_state):
    qr, kr, vr, zr, gr, cw_ref, ab_ref, on_ref = refs[:8]
    if has_state:
        hist_ref, s0_ref = refs[8:10]
        refs = refs[10:]
    else:
        refs = refs[8:]
    y_ref, sout_ref, cbuf, qs, ks, vs, gsc, bsc, sbuf = refs
    w = qr.shape[-1]
    n_heads = w // HD_C
    i = pl.program_id(1)

    @pl.when(i == 0)
    def _init():
        cbuf[:, 0:SUBLANES, :] = jnp.zeros((3, SUBLANES, w), F32)
        if has_state:
            for g in range(3):
                cbuf[g, SUBLANES - 3:SUBLANES, :] = hist_ref[0, :, g * w:(g + 1) * w]
            sbuf[...] = s0_ref[0]
        else:
            sbuf[...] = jnp.zeros(sbuf.shape, F32)

    for g, (raw_ref, dst) in enumerate(((qr, qs), (kr, ks), (vr, vs))):
        cbuf[g, SUBLANES:SUBLANES + tt, :] = raw_ref[0].astype(F32)
        cw = cw_ref[:, g * w:(g + 1) * w]
        x = (cbuf[g, SUBLANES - 3:SUBLANES - 3 + tt, :] * cw[0:1]
             + cbuf[g, SUBLANES - 2:SUBLANES - 2 + tt, :] * cw[1:2]
             + cbuf[g, SUBLANES - 1:SUBLANES - 1 + tt, :] * cw[2:3]
             + cbuf[g, SUBLANES:SUBLANES + tt, :] * cw[3:4])
        x = _silu(x)
        cbuf[g, 0:SUBLANES, :] = cbuf[g, tt:tt + SUBLANES, :]
        if g == 2:
            dst[...] = x
        else:
            scale = HD_C ** -0.5 if g == 0 else 1.0
            for h in range(n_heads):
                cols = slice(h * HD_C, (h + 1) * HD_C)
                xh = x[:, cols]
                ss = jnp.sum(xh * xh, axis=-1, keepdims=True)
                dst[:, cols] = xh * (lax.rsqrt(ss + EPS) * scale)

    gl = gr[0]
    bsc[...] = 1.0 / (1.0 + jnp.exp(-gl))
    gsc[...] = -jnp.exp(ab_ref[0:1, :]) * _softplus(gl + ab_ref[1:2, :])

    row = lax.broadcasted_iota(jnp.int32, (CHUNK, CHUNK), 0)
    col = lax.broadcasted_iota(jnp.int32, (CHUNK, CHUNK), 1)
    incl = row >= col
    strict = row > col

    def chunk(n, carry):
        r0 = pl.multiple_of(n * CHUNK, CHUNK)
        rows = pl.ds(r0, CHUNK)
        heads = range(n_heads)
        hcols = [slice(h * HD_C, (h + 1) * HD_C) for h in heads]
        gc = _cumsum_rows(gsc[rows, :])
        gct = gc.T
        bet = bsc[rows, :]
        gcol = [jnp.broadcast_to(gc[:, n_heads + h:n_heads + h + 1], (CHUNK, HD_C)) for h in heads]
        b = [jnp.broadcast_to(bet[:, h:h + 1], (CHUNK, HD_C)) for h in heads]
        eg = [jnp.exp(x) for x in gcol]
        kd = [jnp.exp(x[CHUNK - 1:CHUNK, :] - x) for x in gcol]
        egl = [jnp.exp(x[CHUNK - 1:CHUNK, :]) for x in gcol]
        decay = [jnp.exp(jnp.where(incl, gcol[h][:, 0:CHUNK] - gct[n_heads + h:n_heads + h + 1, :], -jnp.inf))
                 for h in heads]
        q = [qs[rows, c] for c in hcols]
        k = [ks[rows, c] for c in hcols]
        v = [vs[rows, c] for c in hcols]
        kb = [x.astype(BF16) for x in k]
        qk_kk = [lax.dot_general(jnp.concatenate([q[h].astype(BF16), kb[h]], axis=0), kb[h],
                                 (((1,), (1,)), ((), ())), preferred_element_type=F32) for h in heads]
        qk = [qk_kk[h][0:CHUNK] * decay[h] for h in heads]
        a = [jnp.where(strict, b[h][:, 0:CHUNK] * qk_kk[h][CHUNK:2 * CHUNK] * decay[h], 0.0) for h in heads]
        tinv = _unit_lower_inverses(a)
        sol = [_dot(tinv[h], jnp.concatenate([v[h] * b[h], k[h] * (b[h] * eg[h])], axis=1)) for h in heads]
        s_old = [sbuf[h] for h in heads]
        ws_qs = [_dot(jnp.concatenate([sol[h][:, HD_C:], q[h] * eg[h]], axis=0), s_old[h]) for h in heads]
        u = [sol[h][:, 0:HD_C] - ws_qs[h][0:CHUNK] for h in heads]
        o = [ws_qs[h][CHUNK:2 * CHUNK] + _dot(qk[h], u[h]) for h in heads]
        for h in heads:
            sbuf[h] = egl[h] * s_old[h] + _dot((k[h] * kd[h]).T, u[h])
        for h in heads:
            ms = jnp.mean(o[h] * o[h], axis=-1, keepdims=True)
            on = o[h] * lax.rsqrt(ms + EPS) * on_ref[...]
            y_ref[0, rows, hcols[h]] = (on * _silu(zr[0, rows, hcols[h]].astype(F32))).astype(y_ref.dtype)
        return carry

    lax.fori_loop(0, tt // CHUNK, chunk, 0)
    sout_ref[0] = sbuf[...]


def _gdn_mixer(p, gates, conv_w, ab, onorm, hist, s0):
    b, t, n = p.shape
    w = n // 4
    n_heads = w // HD_C
    tt = min(t, ROW_TILE)
    has_state = s0 is not None

    def col(j):
        return pl.BlockSpec((1, tt, w), lambda bi, i, j=j: (bi, i, j))

    in_specs = [col(j) for j in range(4)]
    in_specs += [pl.BlockSpec((1, tt, LANES), lambda bi, i: (bi, i, 0)),
                 pl.BlockSpec(conv_w.shape, lambda bi, i: (0, 0)),
                 pl.BlockSpec(ab.shape, lambda bi, i: (0, 0)),
                 pl.BlockSpec(onorm.shape, lambda bi, i: (0, 0))]
    args = [p] * 4 + [gates, conv_w, ab, onorm]
    if has_state:
        in_specs += [pl.BlockSpec((1,) + hist.shape[1:], lambda bi, i: (bi, 0, 0)),
                     pl.BlockSpec((1, n_heads, HD_C, HD_C), lambda bi, i: (bi, 0, 0, 0))]
        args += [hist, s0]
    y, s_new = pl.pallas_call(
        functools.partial(_gdn_kernel, tt=tt, has_state=has_state),
        grid=(b, t // tt), in_specs=in_specs,
        out_specs=[pl.BlockSpec((1, tt, w), lambda bi, i: (bi, i, 0)),
                   pl.BlockSpec((1, n_heads, HD_C, HD_C), lambda bi, i: (bi, 0, 0, 0))],
        out_shape=[jax.ShapeDtypeStruct((b, t, w), BF16),
                   jax.ShapeDtypeStruct((b, n_heads, HD_C, HD_C), F32)],
        scratch_shapes=[pltpu.VMEM((3, SUBLANES + tt, w), F32),
                        pltpu.VMEM((tt, w), F32), pltpu.VMEM((tt, w), F32), pltpu.VMEM((tt, w), F32),
                        pltpu.VMEM((tt, LANES), F32), pltpu.VMEM((tt, LANES), F32),
                        pltpu.VMEM((n_heads, HD_C, HD_C), F32)],
        compiler_params=_params("parallel", "arbitrary"), name="gdn_mixer")(*args)
    return y, s_new


def _band_bias(table):
    n_heads = table.shape[0]
    assert CHUNK - 1 <= REL_CLIP < BAND - 1
    low = table[:, REL_CLIP - (CHUNK - 1):]
    high = jnp.broadcast_to(table[:, 2 * REL_CLIP:], (n_heads, BAND - 1 - REL_CLIP))
    by_dist = jnp.concatenate([low, high], axis=1)[:, ::-1]
    rows = [by_dist[:, CHUNK - 1 - qi:CHUNK - 1 - qi + BAND] for qi in range(CHUNK)]
    bias = jnp.stack(rows, axis=1).astype(F32) * LOG2E
    return bias.reshape(n_heads // 2, 2 * CHUNK, BAND)


def _even_layer(x, hist, kc, vc, g_pre, g_post, w_in_bf, conv_w, bias2, w_out_bf):
    b, t, d = x.shape
    p = _norm_proj(x.reshape(b * t, d), g_pre, w_in_bf).reshape(b, t, -1)
    w = p.shape[-1] // 8
    if kc is not None:
        kc = kc.reshape(b, BAND_PAST, w).astype(BF16)
        vc = vc.reshape(b, BAND_PAST, w).astype(BF16)
    y, utail = _even_mixer(p, conv_w, bias2, hist, kc, vc)
    x_new = _out_proj(y.reshape(b * t, -1), w_out_bf, g_post, x.reshape(b * t, d)).reshape(b, t, d)
    keep = min(BAND_PAST, t)
    n_heads = w // HD_B
    new_k = p[:, t - keep:, 5 * w:6 * w].astype(F32).reshape(b, keep, n_heads, HD_B)
    new_v = p[:, t - keep:, 6 * w:7 * w].astype(F32).reshape(b, keep, n_heads, HD_B)
    return x_new, utail[:, SUBLANES - 2:], new_k, new_v


def _odd_layer(x, hist, s0, g_pre, g_post, w_main_bf, w_gate_bf, conv_w, ab, onorm, w_out_bf):
    b, t, d = x.shape
    p, gates = _norm_proj(x.reshape(b * t, d), g_pre, w_main_bf, w_gate_bf)
    p = p.reshape(b, t, -1)
    w = p.shape[-1] // 4
    y, s_new = _gdn_mixer(p, gates.reshape(b, t, LANES), conv_w, ab, onorm, hist, s0)
    x_new = _out_proj(y.reshape(b * t, -1), w_out_bf, g_post, x.reshape(b * t, d)).reshape(b, t, d)
    new_hist = p[:, t - 3:, 0:3 * w].astype(F32)
    return x_new, new_hist, s_new


def kernel(x_prompt, x_sample, cache_conv_a, cache_k_b, cache_v_b, state_conv_c, state_s_c, norm_pre, norm_post, w_in_even, conv_w_a, rel_bias_b, w_out_even, w_in_odd, conv_w_c, a_log_c, dt_bias_c, out_norm_c, w_out_odd):
    depth = norm_pre.shape[0]
    yp, ys = x_prompt, x_sample
    outs_p = [[] for _ in range(5)]
    outs_s = [[] for _ in range(5)]
    for layer in range(depth):
        i = layer // 2
        g_pre, g_post = norm_pre[layer], norm_post[layer]
        if layer % 2 == 0:
            w_in = w_in_even[i]
            w_blk = w_in.shape[1] // 8
            col_scale = jnp.ones((w_in.shape[1],), F32).at[4 * w_blk:5 * w_blk].set(HD_B ** -0.5 * LOG2E)
            w_in_bf = (w_in * col_scale).astype(BF16)
            w_out_bf = w_out_even[i].astype(BF16)
            bias2 = _band_bias(rel_bias_b[i])
            shared = (g_pre, g_post, w_in_bf, conv_w_a[i], bias2, w_out_bf)
            yp, hst, nk, nv = _even_layer(yp, None, None, None, *shared)
            for lst, val in zip(outs_p[:3], (hst, nk, nv)):
                lst.append(val)
            ys, hst, nk, nv = _even_layer(ys, cache_conv_a[i], cache_k_b[i], cache_v_b[i], *shared)
            for lst, val in zip(outs_s[:3], (hst, nk, nv)):
                lst.append(val)
        else:
            w_in = w_in_odd[i]
            n_heads = a_log_c.shape[1]
            n_main = w_in.shape[1] - 2 * n_heads
            w_main_bf = w_in[:, :n_main].astype(BF16)
            w_gate_bf = jnp.pad(w_in[:, n_main:], ((0, 0), (0, LANES - 2 * n_heads))).astype(BF16)
            pad = (n_heads, LANES - 2 * n_heads)
            ab = jnp.stack([jnp.pad(a_log_c[i], pad), jnp.pad(dt_bias_c[i], pad)]).astype(F32)
            shared = (g_pre, g_post, w_main_bf, w_gate_bf, conv_w_c[i], ab, out_norm_c[i].reshape(1, -1),
                      w_out_odd[i].astype(BF16))
            yp, hst, s_new = _odd_layer(yp, None, None, *shared)
            outs_p[3].append(hst)
            outs_p[4].append(s_new)
            ys, hst, s_new = _odd_layer(ys, state_conv_c[i], state_s_c[i], *shared)
            outs_s[3].append(hst)
            outs_s[4].append(s_new)
    return (yp, ys) + tuple(jnp.stack(v) for v in outs_p) + tuple(jnp.stack(v) for v in outs_s)
```

```python
import functools
import math

import jax
import jax.numpy as jnp
from jax import lax
from jax.experimental import pallas as pl
from jax.experimental.pallas import tpu as pltpu

F32 = jnp.float32
BF16 = jnp.bfloat16

EPS = 1e-6
LOG2E = math.log2(math.e)
CHUNK = 64
BAND_PAST = 8 * CHUNK
BAND = BAND_PAST + CHUNK
REL_CLIP = 4 * CHUNK
HD_B = 64
HD_C = 128
LANES = 128
SUBLANES = 8
VMEM_LIMIT_BYTES = 56 * 1024 * 1024
STATE_VMEM_BUDGET_BYTES = 12 * 1024 * 1024
ROW_TILE = 512


def _silu(x):
    return x * (1.0 / (1.0 + jnp.exp(-x)))


def _softplus(x):
    return jnp.maximum(x, 0.0) + jnp.log1p(jnp.exp(-jnp.abs(x)))


def _params(*sem):
    return pltpu.CompilerParams(dimension_semantics=sem, vmem_limit_bytes=VMEM_LIMIT_BYTES)


def _norm_proj_kernel(x_ref, g_ref, w_ref, *rest, n_main, col_tile):
    x = x_ref[...]
    ms = jnp.mean(x * x, axis=-1, keepdims=True)
    h = (x * lax.rsqrt(ms + EPS) * g_ref[...]).astype(BF16)
    if len(rest) == 3:
        wg_ref, o_ref, og_ref = rest
        og_ref[...] = jnp.dot(h, wg_ref[...], preferred_element_type=F32)
    else:
        (o_ref,) = rest
    for j in range(n_main // col_tile):
        cols = slice(j * col_tile, (j + 1) * col_tile)
        o_ref[:, cols] = jnp.dot(h, w_ref[:, cols], preferred_element_type=F32).astype(o_ref.dtype)


def _norm_proj(x2, gain, w_bf, wg_bf=None):
    m, d = x2.shape
    n_main = w_bf.shape[1]
    tm = min(m, ROW_TILE)
    in_specs = [pl.BlockSpec((tm, d), lambda i: (i, 0)),
                pl.BlockSpec((1, d), lambda i: (0, 0)),
                pl.BlockSpec((d, n_main), lambda i: (0, 0))]
    args = [x2, gain.reshape(1, d), w_bf]
    out_shape = [jax.ShapeDtypeStruct((m, n_main), BF16)]
    out_specs = [pl.BlockSpec((tm, n_main), lambda i: (i, 0))]
    if wg_bf is not None:
        in_specs.append(pl.BlockSpec((d, LANES), lambda i: (0, 0)))
        args.append(wg_bf)
        out_shape.append(jax.ShapeDtypeStruct((m, LANES), F32))
        out_specs.append(pl.BlockSpec((tm, LANES), lambda i: (i, 0)))
    res = pl.pallas_call(
        functools.partial(_norm_proj_kernel, n_main=n_main, col_tile=1024),
        grid=(m // tm,), in_specs=in_specs, out_specs=out_specs, out_shape=out_shape,
        compiler_params=_params("parallel"), name="norm_proj")(*args)
    return res if wg_bf is not None else res[0]


def _out_proj_kernel(y_ref, w_ref, g_ref, x_ref, o_ref):
    out = jnp.dot(y_ref[...], w_ref[...], preferred_element_type=F32)
    ms = jnp.mean(out * out, axis=-1, keepdims=True)
    o_ref[...] = x_ref[...] + out * lax.rsqrt(ms + EPS) * g_ref[...]


def _out_proj(y2, w_bf, gain, x2):
    m, k = y2.shape
    d = w_bf.shape[1]
    tm = min(m, ROW_TILE)
    return pl.pallas_call(
        _out_proj_kernel, grid=(m // tm,),
        in_specs=[pl.BlockSpec((tm, k), lambda i: (i, 0)),
                  pl.BlockSpec((k, d), lambda i: (0, 0)),
                  pl.BlockSpec((1, d), lambda i: (0, 0)),
                  pl.BlockSpec((tm, d), lambda i: (i, 0))],
        out_specs=pl.BlockSpec((tm, d), lambda i: (i, 0)),
        out_shape=jax.ShapeDtypeStruct((m, d), F32),
        compiler_params=_params("parallel"), name="out_proj")(y2, w_bf, gain.reshape(1, d), x2)


def _even_mixer_kernel(*refs, tq, has_cache):
    ab, ac, ah, az, q_ref, k_ref, v_ref, bz_ref, cw_ref, bias_ref = refs[:10]
    if has_cache:
        hist_ref, kc_ref, vc_ref = refs[10:13]
        refs = refs[13:]
    else:
        refs = refs[10:]
    y_ref, utail_ref, ubuf, kbuf, vbuf = refs
    w = ab.shape[-1]
    i = pl.program_id(1)

    @pl.when(i == 0)
    def _init():
        ubuf[0:SUBLANES, :] = jnp.zeros((SUBLANES, w), F32)
        if has_cache:
            ubuf[SUBLANES - 2:SUBLANES, :] = hist_ref[0]
            kbuf[0:BAND_PAST, :] = kc_ref[0]
            vbuf[0:BAND_PAST, :] = vc_ref[0]
        else:
            kbuf[0:BAND_PAST, :] = jnp.zeros((BAND_PAST, w), BF16)
            vbuf[0:BAND_PAST, :] = jnp.zeros((BAND_PAST, w), BF16)

    u = ac[0].astype(F32) * ah[0].astype(F32)
    ubuf[SUBLANES:SUBLANES + tq, :] = u
    conv = (ubuf[SUBLANES - 2:SUBLANES - 2 + tq, :] * cw_ref[0:1, :]
            + ubuf[SUBLANES - 1:SUBLANES - 1 + tq, :] * cw_ref[1:2, :]
            + u * cw_ref[2:3, :])
    ya = ab[0].astype(F32) * conv * _silu(az[0].astype(F32))
    y_ref[0, :, 0:w] = ya.astype(y_ref.dtype)
    tail = ubuf[tq:tq + SUBLANES, :]
    utail_ref[0] = tail
    ubuf[0:SUBLANES, :] = tail

    kbuf[BAND_PAST:BAND_PAST + tq, :] = k_ref[0]
    vbuf[BAND_PAST:BAND_PAST + tq, :] = v_ref[0]
    n_chunks = tq // CHUNK
    n_pairs = w // LANES
    lane = lax.broadcasted_iota(jnp.int32, (CHUNK, LANES), 1)
    first_head = lane < HD_B

    def attend(first_tile):
        def band_start(c):
            missing = max(BAND_PAST - CHUNK * c, 0) if first_tile else 0
            return missing // LANES * LANES, missing % LANES

        def scores(c, hp):
            cols = slice(hp * LANES, (hp + 1) * LANES)
            lo, masked = band_start(c)
            qp = q_ref[0, c * CHUNK:(c + 1) * CHUNK, cols]
            zero = jnp.zeros_like(qp)
            q2 = jnp.concatenate([jnp.where(first_head, qp, zero), jnp.where(first_head, zero, qp)], axis=0)
            kp = kbuf[c * CHUNK + lo:c * CHUNK + BAND, cols]
            s = lax.dot_general(q2, kp, (((1,), (1,)), ((), ())), preferred_element_type=F32)
            s = s + bias_ref[hp, :, lo:BAND]
            if masked:
                key_idx = lax.broadcasted_iota(jnp.int32, s.shape, 1)
                s = jnp.where(key_idx >= masked, s, -jnp.inf)
            return s

        def weighted_values(c, hp, s):
            cols = slice(hp * LANES, (hp + 1) * LANES)
            lo, _ = band_start(c)
            m = jnp.max(s, axis=-1, keepdims=True)
            e = jnp.exp2(s - m)
            l = jnp.sum(e, axis=-1, keepdims=True)
            vp = vbuf[c * CHUNK + lo:c * CHUNK + BAND, cols]
            pv = jnp.dot(e.astype(BF16), vp, preferred_element_type=F32) / l
            return jnp.where(first_head, pv[0:CHUNK], pv[CHUNK:2 * CHUNK])

        items = [(c, hp) for c in range(n_chunks) for hp in range(n_pairs)]
        skew = 3
        pending, outs = {}, {}
        for step in range(len(items) + skew):
            if step < len(items):
                pending[step] = scores(*items[step])
            j = step - skew
            if j >= 0:
                c, hp = items[j]
                outs[hp] = weighted_values(c, hp, pending.pop(j))
                if hp == n_pairs - 1:
                    rows = slice(c * CHUNK, (c + 1) * CHUNK)
                    o = jnp.concatenate([outs[x] for x in range(n_pairs)], axis=1)
                    yb = o * _silu(bz_ref[0, rows, :].astype(F32))
                    y_ref[0, rows, w:2 * w] = yb.astype(y_ref.dtype)

    if has_cache:
        attend(False)
    else:
        pl.when(i == 0)(functools.partial(attend, True))
        pl.when(i > 0)(functools.partial(attend, False))
    kbuf[0:BAND_PAST, :] = kbuf[tq:tq + BAND_PAST, :]
    vbuf[0:BAND_PAST, :] = vbuf[tq:tq + BAND_PAST, :]


def _even_mixer(p, conv_w, bias2, hist, kc, vc):
    b, t, n = p.shape
    w = n // 8
    tq = min(t, ROW_TILE)
    has_cache = kc is not None

    def col(j):
        return pl.BlockSpec((1, tq, w), lambda bi, i, j=j: (bi, i, j))

    in_specs = [col(j) for j in range(8)]
    in_specs += [pl.BlockSpec(conv_w.shape, lambda bi, i: (0, 0)),
                 pl.BlockSpec(bias2.shape, lambda bi, i: (0, 0, 0))]
    args = [p] * 8 + [conv_w, bias2]
    if has_cache:
        in_specs += [pl.BlockSpec((1,) + hist.shape[1:], lambda bi, i: (bi, 0, 0)),
                     pl.BlockSpec((1, BAND_PAST, w), lambda bi, i: (bi, 0, 0)),
                     pl.BlockSpec((1, BAND_PAST, w), lambda bi, i: (bi, 0, 0))]
        args += [hist, kc, vc]
    y, utail = pl.pallas_call(
        functools.partial(_even_mixer_kernel, tq=tq, has_cache=has_cache),
        grid=(b, t // tq), in_specs=in_specs,
        out_specs=[pl.BlockSpec((1, tq, 2 * w), lambda bi, i: (bi, i, 0)),
                   pl.BlockSpec((1, SUBLANES, w), lambda bi, i: (bi, 0, 0))],
        out_shape=[jax.ShapeDtypeStruct((b, t, 2 * w), BF16),
                   jax.ShapeDtypeStruct((b, SUBLANES, w), F32)],
        scratch_shapes=[pltpu.VMEM((SUBLANES + tq, w), F32),
                        pltpu.VMEM((BAND_PAST + tq, w), BF16),
                        pltpu.VMEM((BAND_PAST + tq, w), BF16)],
        compiler_params=_params("parallel", "arbitrary"), name="even_mixer")(*args)
    return y, utail


def _cumsum_rows(x):
    n = x.shape[0]
    row = lax.broadcasted_iota(jnp.int32, x.shape, 0)
    shift = 1
    while shift < n:
        x = x + jnp.where(row >= shift, pltpu.roll(x, shift, axis=0), 0.0)
        shift *= 2
    return x


def _load_rows(ref, nb):
    parts = [ref[s] for s in range(nb)]
    return parts[0] if nb == 1 else jnp.concatenate(parts, axis=0)


def _dot(a, b):
    return jnp.dot(a.astype(BF16), b.astype(BF16), preferred_element_type=F32)


def _unit_lower_inverses(mats):
    n = mats[0].shape[0]
    row = lax.broadcasted_iota(jnp.int32, (n, n), 0)
    col = lax.broadcasted_iota(jnp.int32, (n, n), 1)
    eye = jnp.where(row == col, 1.0, 0.0)
    xs = [-a for a in mats]
    ps = [eye + x for x in xs]
    ys = [_dot(x, x) for x in xs]
    power = 2
    while 2 * power < n:
        sts = [_dot(jnp.concatenate([p, y], axis=0), y) for p, y in zip(ps, ys)]
        ps = [p + st[0:n] for p, st in zip(ps, sts)]
        ys = [st[n:2 * n] for st in sts]
        power *= 2
    return [p + _dot(p, y) for p, y in zip(ps, ys)]


def _odd_layer_kernel(*refs, nb, tt, has_state, prep_chunks):
    x_ref, gpre_ref, win_ref, wg_ref, cw_ref, ab_ref, on_ref, wout_ref, gpost_ref = refs[:9]
    if has_state:
        hist_ref, s0_ref = refs[9:11]
        refs = refs[11:]
    else:
        refs = refs[9:]
    (xo_ref, sout_ref, ctail_ref, ctails, qs, ks, vs, zs, gsc, bsc, sbuf, ybuf,
     wbuf, qtbuf, u0buf, qkbuf, kdtbuf, eglbuf) = refs
    d = x_ref.shape[-1]
    w = win_ref.shape[-1] // 4
    n_heads = w // HD_C
    n_rows = nb * tt
    i = pl.program_id(1)

    @pl.when(i == 0)
    def _init():
        ctails[...] = jnp.zeros(ctails.shape, F32)
        if has_state:
            for s in range(nb):
                for g in range(3):
                    ctails[s, g, SUBLANES - 3:SUBLANES, :] = hist_ref[s, :, g * w:(g + 1) * w]
            sbuf[...] = s0_ref[...]
        else:
            sbuf[...] = jnp.zeros(sbuf.shape, F32)

    x = _load_rows(x_ref, nb)
    ms = jnp.mean(x * x, axis=-1, keepdims=True)
    hb = (x * lax.rsqrt(ms + EPS) * gpre_ref[...]).astype(BF16)

    gl = jnp.dot(hb, wg_ref[...], preferred_element_type=F32)
    bsc[...] = 1.0 / (1.0 + jnp.exp(-gl))
    gsc[...] = -jnp.exp(ab_ref[0:1, :]) * _softplus(gl + ab_ref[1:2, :])

    for g, dst in enumerate((qs, ks, vs)):
        raw = jnp.dot(hb, win_ref[:, g * w:(g + 1) * w], preferred_element_type=F32)
        cw = cw_ref[:, g * w:(g + 1) * w]
        inv_scale_sq = float(HD_C) if g == 0 else 1.0
        sum_mat = jnp.full((HD_C, HD_C), inv_scale_sq, BF16)
        for s in range(nb):
            srows = slice(s * tt, (s + 1) * tt)
            full = jnp.concatenate([ctails[s, g], raw[srows]], axis=0)
            acc = full * cw[0:1]
            for j in range(1, cw.shape[0]):
                acc = full * cw[j:j + 1] + pltpu.roll(acc, 1, axis=0)
            c = _silu(acc[SUBLANES:, :])
            tail = full[tt:tt + SUBLANES, :]
            ctail_ref[s, :, g * w:(g + 1) * w] = tail
            ctails[s, g] = tail
            if g == 2:
                dst[srows, :] = c
            else:
                for h in range(n_heads):
                    cols = slice(h * HD_C, (h + 1) * HD_C)
                    ch = c[:, cols]
                    ss = jnp.dot((ch * ch).astype(BF16), sum_mat, preferred_element_type=F32)
                    dst[srows, cols] = ch * lax.rsqrt(ss + EPS * inv_scale_sq)
    zs[...] = _silu(jnp.dot(hb, win_ref[:, 3 * w:4 * w], preferred_element_type=F32)).astype(zs.dtype)

    row = lax.broadcasted_iota(jnp.int32, (CHUNK, CHUNK), 0)
    col = lax.broadcasted_iota(jnp.int32, (CHUNK, CHUNK), 1)
    incl = row >= col
    strict = row > col

    heads = range(n_heads)
    hcols = [slice(h * HD_C, (h + 1) * HD_C) for h in heads]
    n_chunks = n_rows // CHUNK

    def prepare_chunks(step, carry):
        items = []
        for j in range(prep_chunks):
            cidx = step * prep_chunks + j
            rows = pl.ds(pl.multiple_of(cidx * CHUNK, CHUNK), CHUNK)
            gc = _cumsum_rows(gsc[rows, :])
            gct = gc.T
            bet = bsc[rows, :]
            for h in heads:
                gcol = jnp.broadcast_to(gc[:, n_heads + h:n_heads + h + 1], (CHUNK, HD_C))
                items.append(dict(
                    cidx=cidx, rows=rows, h=h, gcol=gcol, grow=gct[n_heads + h:n_heads + h + 1, :],
                    b=jnp.broadcast_to(bet[:, h:h + 1], (CHUNK, HD_C))))
        for it in items:
            gcol, rows, c = it["gcol"], it["rows"], hcols[it["h"]]
            it["eg"] = jnp.exp(gcol)
            it["decay"] = jnp.exp(jnp.where(incl, gcol[:, 0:CHUNK] - it["grow"], -jnp.inf))
            it["q"], it["k"], it["v"] = qs[rows, c], ks[rows, c], vs[rows, c]
        for it in items:
            kb = it["k"].astype(BF16)
            it["qk_kk"] = lax.dot_general(jnp.concatenate([it["q"].astype(BF16), kb], axis=0), kb,
                                          (((1,), (1,)), ((), ())), preferred_element_type=F32)
        mats = [jnp.where(strict, it["b"][:, 0:CHUNK] * it["qk_kk"][CHUNK:2 * CHUNK] * it["decay"], 0.0)
                for it in items]
        tinvs = _unit_lower_inverses(mats)
        sols = [_dot(tinv, jnp.concatenate([it["v"] * it["b"], it["k"] * (it["b"] * it["eg"])], axis=1))
                for tinv, it in zip(tinvs, items)]
        for sol, it in zip(sols, items):
            gcol, rows, c, h = it["gcol"], it["rows"], hcols[it["h"]], it["h"]
            g_last = gcol[CHUNK - 1:CHUNK, :]
            u0buf[rows, c] = sol[:, 0:HD_C]
            wbuf[rows, c] = sol[:, HD_C:].astype(BF16)
            qtbuf[rows, c] = (it["q"] * it["eg"]).astype(BF16)
            qkbuf[h, rows, :] = (it["qk_kk"][0:CHUNK] * it["decay"]).astype(BF16)
            kdtbuf[h, it["cidx"]] = (it["k"] * jnp.exp(g_last - gcol)).T.astype(BF16)
            eglbuf[it["cidx"], h:h + 1, :] = jnp.exp(g_last)
        return carry

    lax.fori_loop(0, n_chunks // prep_chunks, prepare_chunks, 0)

    for cidx in range(n_chunks):
        s = cidx // (tt // CHUNK)
        rows = slice(cidx * CHUNK, (cidx + 1) * CHUNK)
        s_old = [sbuf[s, h] for h in heads]
        ws_qs = [_dot(jnp.concatenate([wbuf[rows, hcols[h]], qtbuf[rows, hcols[h]]], axis=0), s_old[h])
                 for h in heads]
        u = [u0buf[rows, hcols[h]] - ws_qs[h][0:CHUNK] for h in heads]
        o = [ws_qs[h][CHUNK:2 * CHUNK] + _dot(qkbuf[h, rows, :], u[h]) for h in heads]
        for h in heads:
            sbuf[s, h] = eglbuf[cidx, h:h + 1, :] * s_old[h] + _dot(kdtbuf[h, cidx], u[h])
        for h in heads:
            ms = jnp.mean(o[h] * o[h], axis=-1, keepdims=True)
            on = o[h] * lax.rsqrt(ms + EPS) * on_ref[...]
            ybuf[rows, hcols[h]] = (on * zs[rows, hcols[h]].astype(F32)).astype(ybuf.dtype)
    sout_ref[...] = sbuf[...]

    out = jnp.dot(ybuf[...], wout_ref[...], preferred_element_type=F32)
    ms = jnp.mean(out * out, axis=-1, keepdims=True)
    x_new = _load_rows(x_ref, nb) + out * lax.rsqrt(ms + EPS) * gpost_ref[...]
    for s in range(nb):
        xo_ref[s] = x_new[s * tt:(s + 1) * tt]


def _streams_per_step(b, t, state_bytes_per_stream=0):
    tt = min(t, ROW_TILE)
    nb = max(1, ROW_TILE // tt)
    if state_bytes_per_stream:
        nb = max(1, min(nb, STATE_VMEM_BUDGET_BYTES // state_bytes_per_stream))
    while b % nb:
        nb -= 1
    return nb, tt


def _resident(shape):
    return pl.BlockSpec(shape, lambda bi, i: (0,) * len(shape), pipeline_mode=pl.Buffered(1))


def _odd_layer(x, hist, s0, g_pre, g_post, w_in_bf, w_gate_bf, conv_w, ab, onorm, w_out_bf):
    b, t, d = x.shape
    w = w_in_bf.shape[1] // 4
    n_heads = w // HD_C
    has_state = s0 is not None
    state_bytes = n_heads * HD_C * HD_C * 4 * (5 if has_state else 3)
    nb, tt = _streams_per_step(b, t, state_bytes)
    n_rows = nb * tt
    n_chunks = n_rows // CHUNK
    prep_chunks = 2 if n_chunks % 2 == 0 else 1
    consts =[g_pre.reshape(1, d), w_in_bf, w_gate_bf, conv_w, ab, onorm, w_out_bf, g_post.reshape(1, d)]
    in_specs = [pl.BlockSpec((nb, tt, d), lambda bi, i: (bi, i, 0))] + [_resident(c.shape) for c in consts]
    args = [x] + consts
    if has_state:
        in_specs += [pl.BlockSpec((nb,) + hist.shape[1:], lambda bi, i: (bi, 0, 0)),
                     pl.BlockSpec((nb, n_heads, HD_C, HD_C), lambda bi, i: (bi, 0, 0, 0))]
        args += [hist, s0]
    x_new, s_new, ctail = pl.pallas_call(
        functools.partial(_odd_layer_kernel, nb=nb, tt=tt, has_state=has_state, prep_chunks=prep_chunks),
        grid=(b // nb, t // tt), in_specs=in_specs,
        out_specs=[pl.BlockSpec((nb, tt, d), lambda bi, i: (bi, i, 0)),
                   pl.BlockSpec((nb, n_heads, HD_C, HD_C), lambda bi, i: (bi, 0, 0, 0)),
                   pl.BlockSpec((nb, SUBLANES, 3 * w), lambda bi, i: (bi, 0, 0))],
        out_shape=[jax.ShapeDtypeStruct((b, t, d), F32),
                   jax.ShapeDtypeStruct((b, n_heads, HD_C, HD_C), F32),
                   jax.ShapeDtypeStruct((b, SUBLANES, 3 * w), F32)],
        scratch_shapes=[pltpu.VMEM((nb, 3, SUBLANES, w), F32),
                        pltpu.VMEM((n_rows, w), F32), pltpu.VMEM((n_rows, w), F32), pltpu.VMEM((n_rows, w), F32),
                        pltpu.VMEM((n_rows, w), BF16),
                        pltpu.VMEM((n_rows, LANES), F32), pltpu.VMEM((n_rows, LANES), F32),
                        pltpu.VMEM((nb, n_heads, HD_C, HD_C), F32),
                        pltpu.VMEM((n_rows, w), BF16),
                        pltpu.VMEM((n_rows, w), BF16), pltpu.VMEM((n_rows, w), BF16), pltpu.VMEM((n_rows, w), F32),
                        pltpu.VMEM((n_heads, n_rows, CHUNK), BF16),
                        pltpu.VMEM((n_heads, n_chunks, HD_C, CHUNK), BF16),
                        pltpu.VMEM((n_chunks, n_heads, HD_C), F32)],
        compiler_params=_params("parallel", "arbitrary"), name="odd_layer")(*args)
    return x_new, ctail[:, SUBLANES - 3:], s_new


def _band_bias(table):
    n_heads = table.shape[0]
    assert CHUNK - 1 <= REL_CLIP < BAND - 1
    low = table[:, REL_CLIP - (CHUNK - 1):]
    high = jnp.broadcast_to(table[:, 2 * REL_CLIP:], (n_heads, BAND - 1 - REL_CLIP))
    by_dist = jnp.concatenate([low, high], axis=1)[:, ::-1]
    rows = [by_dist[:, CHUNK - 1 - qi:CHUNK - 1 - qi + BAND] for qi in range(CHUNK)]
    bias = jnp.stack(rows, axis=1).astype(F32) * LOG2E
    return bias.reshape(n_heads // 2, 2 * CHUNK, BAND)


def _even_layer(x, hist, kc, vc, g_pre, g_post, w_in_bf, conv_w, bias2, w_out_bf):
    b, t, d = x.shape
    p = _norm_proj(x.reshape(b * t, d), g_pre, w_in_bf).reshape(b, t, -1)
    w = p.shape[-1] // 8
    if kc is not None:
        kc = kc.reshape(b, BAND_PAST, w).astype(BF16)
        vc = vc.reshape(b, BAND_PAST, w).astype(BF16)
    y, utail = _even_mixer(p, conv_w, bias2, hist, kc, vc)
    x_new = _out_proj(y.reshape(b * t, -1), w_out_bf, g_post, x.reshape(b * t, d)).reshape(b, t, d)
    keep = min(BAND_PAST, t)
    n_heads = w // HD_B
    new_k = p[:, t - keep:, 5 * w:6 * w].astype(F32).reshape(b, keep, n_heads, HD_B)
    new_v = p[:, t - keep:, 6 * w:7 * w].astype(F32).reshape(b, keep, n_heads, HD_B)
    return x_new, utail[:, SUBLANES - 2:], new_k, new_v


def kernel(x_prompt, x_sample, cache_conv_a, cache_k_b, cache_v_b, state_conv_c, state_s_c, norm_pre, norm_post, w_in_even, conv_w_a, rel_bias_b, w_out_even, w_in_odd, conv_w_c, a_log_c, dt_bias_c, out_norm_c, w_out_odd):
    depth = norm_pre.shape[0]
    yp, ys = x_prompt, x_sample
    outs_p = [[] for _ in range(5)]
    outs_s = [[] for _ in range(5)]
    for layer in range(depth):
        i = layer // 2
        g_pre, g_post = norm_pre[layer], norm_post[layer]
        if layer % 2 == 0:
            w_in = w_in_even[i]
            w_blk = w_in.shape[1] // 8
            col_scale = jnp.ones((w_in.shape[1],), F32).at[4 * w_blk:5 * w_blk].set(HD_B ** -0.5 * LOG2E)
            w_in_bf = (w_in * col_scale).astype(BF16)
            w_out_bf = w_out_even[i].astype(BF16)
            bias2 = _band_bias(rel_bias_b[i])
            shared = (g_pre, g_post, w_in_bf, conv_w_a[i], bias2, w_out_bf)
            yp, hst, nk, nv = _even_layer(yp, None, None, None, *shared)
            for lst, val in zip(outs_p[:3], (hst, nk, nv)):
                lst.append(val)
            ys, hst, nk, nv = _even_layer(ys, cache_conv_a[i], cache_k_b[i], cache_v_b[i], *shared)
            for lst, val in zip(outs_s[:3], (hst, nk, nv)):
                lst.append(val)
        else:
            w_in = w_in_odd[i]
            n_heads = a_log_c.shape[1]
            n_main = w_in.shape[1] - 2 * n_heads
            w_main_bf = w_in[:, :n_main].astype(BF16)
            w_gate_bf = jnp.pad(w_in[:, n_main:], ((0, 0), (0, LANES - 2 * n_heads))).astype(BF16)
            pad = (n_heads, LANES - 2 * n_heads)
            ab = jnp.stack([jnp.pad(a_log_c[i], pad), jnp.pad(dt_bias_c[i], pad)]).astype(F32)
            shared = (g_pre, g_post, w_main_bf, w_gate_bf, conv_w_c[i], ab, out_norm_c[i].reshape(1, -1),
                      w_out_odd[i].astype(BF16))
            yp, hst, s_new = _odd_layer(yp, None, None, *shared)
            outs_p[3].append(hst)
            outs_p[4].append(s_new)
            ys, hst, s_new = _odd_layer(ys, state_conv_c[i], state_s_c[i], *shared)
            outs_s[3].append(hst)
            outs_s[4].append(s_new)
    return (yp, ys) + tuple(jnp.stack(v) for v in outs_p) + tuple(jnp.stack(v) for v in outs_s)
```

```python
import functools
import math

import jax
import jax.numpy as jnp
from jax import lax
from jax.experimental import pallas as pl
from jax.experimental.pallas import tpu as pltpu

F32 = jnp.float32
BF16 = jnp.bfloat16

EPS = 1e-6
LOG2E = math.log2(math.e)
CHUNK = 64
BAND_PAST = 8 * CHUNK
BAND = BAND_PAST + CHUNK
REL_CLIP = 4 * CHUNK
HD_B = 64
HD_C = 128
LANES = 128
SUBLANES = 8
VMEM_LIMIT_BYTES = 56 * 1024 * 1024
STATE_VMEM_BUDGET_BYTES = 16 * 1024 * 1024
ROW_TILE = 512


def _silu(x):
    return x * (1.0 / (1.0 + jnp.exp(-x)))


def _softplus(x):
    return jnp.maximum(x, 0.0) + jnp.log1p(jnp.exp(-jnp.abs(x)))


def _dot(a, b):
    return jnp.dot(a.astype(BF16), b.astype(BF16), preferred_element_type=F32)


def _load_rows(ref, nb):
    parts = [ref[s] for s in range(nb)]
    return parts[0] if nb == 1 else jnp.concatenate(parts, axis=0)


def _pre_norm(x_ref, gain_ref, nb):
    x = _load_rows(x_ref, nb)
    ms = jnp.mean(x * x, axis=-1, keepdims=True)
    return (x * lax.rsqrt(ms + EPS) * gain_ref[...]).astype(BF16)


def _project_out(y, wout_ref, gain_ref, x_ref, xo_ref, nb, tt):
    out = jnp.dot(y, wout_ref[...], preferred_element_type=F32)
    ms = jnp.mean(out * out, axis=-1, keepdims=True)
    x_new = _load_rows(x_ref, nb) + out * lax.rsqrt(ms + EPS) * gain_ref[...]
    for s in range(nb):
        xo_ref[s] = x_new[s * tt:(s + 1) * tt]


def _causal_dwconv(prev_rows, new_rows, taps):
    tt = new_rows.shape[0]
    full = jnp.concatenate([prev_rows, new_rows], axis=0)
    acc = full * taps[0:1]
    for j in range(1, taps.shape[0]):
        acc = full * taps[j:j + 1] + pltpu.roll(acc, 1, axis=0)
    return acc[SUBLANES:, :], full[tt:tt + SUBLANES, :]


def _streams_per_step(b, t, state_bytes_per_stream):
    tt = min(t, ROW_TILE)
    nb = max(1, min(ROW_TILE // tt, STATE_VMEM_BUDGET_BYTES // state_bytes_per_stream))
    while b % nb:
        nb -= 1
    return nb, tt


def _resident(shape):
    return pl.BlockSpec(shape, lambda bi, i: (0,) * len(shape), pipeline_mode=pl.Buffered(1))


def _params():
    return pltpu.CompilerParams(dimension_semantics=("parallel", "arbitrary"), vmem_limit_bytes=VMEM_LIMIT_BYTES)


def _even_layer_kernel(*refs, nb, tt, has_cache):
    x_ref, gpre_ref, win_ref, cw_ref, bias_ref, wout_ref, gpost_ref = refs[:7]
    if has_cache:
        hist_ref, kc_ref, vc_ref = refs[7:10]
        refs = refs[10:]
    else:
        refs = refs[7:]
    xo_ref, utail_ref, knew_ref, vnew_ref, utails, kbuf, vbuf, qbuf, zbuf, ybuf = refs
    w = win_ref.shape[-1] // 8
    i = pl.program_id(1)

    @pl.when(i == 0)
    def _init():
        utails[...] = jnp.zeros(utails.shape, F32)
        if has_cache:
            for s in range(nb):
                utails[s, SUBLANES - 2:SUBLANES, :] = hist_ref[s]
            kbuf[:, 0:BAND_PAST, :] = kc_ref[...]
            vbuf[:, 0:BAND_PAST, :] = vc_ref[...]
        else:
            kbuf[:, 0:BAND_PAST, :] = jnp.zeros((nb, BAND_PAST, w), BF16)
            vbuf[:, 0:BAND_PAST, :] = jnp.zeros((nb, BAND_PAST, w), BF16)

    hb = _pre_norm(x_ref, gpre_ref, nb)

    def proj(j):
        return jnp.dot(hb, win_ref[:, j * w:(j + 1) * w], preferred_element_type=F32)

    u = proj(1) * proj(2)
    gate = proj(0) * _silu(proj(3))
    for s in range(nb):
        srows = slice(s * tt, (s + 1) * tt)
        conv, tail = _causal_dwconv(utails[s], u[srows], cw_ref[...])
        utail_ref[s] = tail
        utails[s] = tail
        ybuf[srows, 0:w] = (gate[srows] * conv).astype(ybuf.dtype)

    qbuf[...] = proj(4).astype(qbuf.dtype)
    k = proj(5)
    v = proj(6)
    for s in range(nb):
        srows = slice(s * tt, (s + 1) * tt)
        knew_ref[s] = k[srows]
        vnew_ref[s] = v[srows]
        kbuf[s, BAND_PAST:BAND_PAST + tt, :] = k[srows].astype(kbuf.dtype)
        vbuf[s, BAND_PAST:BAND_PAST + tt, :] = v[srows].astype(vbuf.dtype)
    zbuf[...] = _silu(proj(7)).astype(zbuf.dtype)

    chunks_per_stream = tt // CHUNK
    n_pairs = w // LANES
    lane = lax.broadcasted_iota(jnp.int32, (CHUNK, LANES), 1)
    first_head = lane < HD_B

    def attend(first_tile):
        def band_start(c):
            missing = max(BAND_PAST - CHUNK * c, 0) if first_tile else 0
            return missing // LANES * LANES, missing % LANES

        def scores(s, c, hp):
            cols = slice(hp * LANES, (hp + 1) * LANES)
            lo, masked = band_start(c)
            r0 = s * tt + c * CHUNK
            qp = qbuf[r0:r0 + CHUNK, cols]
            zero = jnp.zeros_like(qp)
            q2 = jnp.concatenate([jnp.where(first_head, qp, zero), jnp.where(first_head, zero, qp)], axis=0)
            kp = kbuf[s, c * CHUNK + lo:c * CHUNK + BAND, cols]
            sc = lax.dot_general(q2, kp, (((1,), (1,)), ((), ())), preferred_element_type=F32)
            sc = sc + bias_ref[hp, :, lo:BAND]
            if masked:
                key_idx = lax.broadcasted_iota(jnp.int32, sc.shape, 1)
                sc = jnp.where(key_idx >= masked, sc, -jnp.inf)
            return sc

        def weighted_values(s, c, hp, sc):
            cols = slice(hp * LANES, (hp + 1) * LANES)
            lo, _ = band_start(c)
            m = jnp.max(sc, axis=-1, keepdims=True)
            e = jnp.exp2(sc - m)
            l = jnp.sum(e, axis=-1, keepdims=True)
            vp = vbuf[s, c * CHUNK + lo:c * CHUNK + BAND, cols]
            pv = jnp.dot(e.astype(BF16), vp, preferred_element_type=F32) / l
            return jnp.where(first_head, pv[0:CHUNK], pv[CHUNK:2 * CHUNK])

        items = [(s, c, hp) for s in range(nb) for c in range(chunks_per_stream) for hp in range(n_pairs)]
        skew = 3
        pending, outs = {}, {}
        for step in range(len(items) + skew):
            if step < len(items):
                pending[step] = scores(*items[step])
            j = step - skew
            if j >= 0:
                s, c, hp = items[j]
                outs[hp] = weighted_values(s, c, hp, pending.pop(j))
                if hp == n_pairs - 1:
                    rows = slice(s * tt + c * CHUNK, s * tt + (c + 1) * CHUNK)
                    o = jnp.concatenate([outs[x] for x in range(n_pairs)], axis=1)
                    ybuf[rows, w:2 * w] = (o * zbuf[rows, :].astype(F32)).astype(ybuf.dtype)

    if has_cache:
        attend(False)
    else:
        pl.when(i == 0)(functools.partial(attend, True))
        pl.when(i > 0)(functools.partial(attend, False))
    for s in range(nb):
        kbuf[s, 0:BAND_PAST, :] = kbuf[s, tt:tt + BAND_PAST, :]
        vbuf[s, 0:BAND_PAST, :] = vbuf[s, tt:tt + BAND_PAST, :]

    _project_out(ybuf[...], wout_ref, gpost_ref, x_ref, xo_ref, nb, tt)


def _even_layer(x, hist, kc, vc, g_pre, g_post, w_in_bf, conv_w, bias2, w_out_bf):
    b, t, d = x.shape
    w = w_in_bf.shape[1] // 8
    n_heads = w // HD_B
    has_cache = kc is not None
    assert t % CHUNK == 0 and (t <= ROW_TILE or t % ROW_TILE == 0)
    band_bytes = 2 * (BAND_PAST + min(t, ROW_TILE)) * w * 2
    state_bytes = band_bytes + (2 * 2 * BAND_PAST * w * 2 if has_cache else 0)
    nb, tt = _streams_per_step(b, t, state_bytes)
    n_rows = nb * tt
    consts = [g_pre.reshape(1, d), w_in_bf, conv_w, bias2, w_out_bf, g_post.reshape(1, d)]
    in_specs = [pl.BlockSpec((nb, tt, d), lambda bi, i: (bi, i, 0))] + [_resident(c.shape) for c in consts]
    args = [x] + consts
    if has_cache:
        kc = kc.reshape(b, BAND_PAST, w).astype(BF16)
        vc = vc.reshape(b, BAND_PAST, w).astype(BF16)
        in_specs += [pl.BlockSpec((nb,) + hist.shape[1:], lambda bi, i: (bi, 0, 0)),
                     pl.BlockSpec((nb, BAND_PAST, w), lambda bi, i: (bi, 0, 0)),
                     pl.BlockSpec((nb, BAND_PAST, w), lambda bi, i: (bi, 0, 0))]
        args += [hist, kc, vc]
    assert tt == min(BAND_PAST, t)
    x_new, utail, k_new, v_new = pl.pallas_call(
        functools.partial(_even_layer_kernel, nb=nb, tt=tt, has_cache=has_cache),
        grid=(b // nb, t // tt), in_specs=in_specs,
        out_specs=[pl.BlockSpec((nb, tt, d), lambda bi, i: (bi, i, 0)),
                   pl.BlockSpec((nb, SUBLANES, w), lambda bi, i: (bi, 0, 0)),
                   pl.BlockSpec((nb, tt, w), lambda bi, i: (bi, 0, 0)),
                   pl.BlockSpec((nb, tt, w), lambda bi, i: (bi, 0, 0))],
        out_shape=[jax.ShapeDtypeStruct((b, t, d), F32),
                   jax.ShapeDtypeStruct((b, SUBLANES, w), F32),
                   jax.ShapeDtypeStruct((b, tt, w), F32),
                   jax.ShapeDtypeStruct((b, tt, w), F32)],
        scratch_shapes=[pltpu.VMEM((nb, SUBLANES, w), F32),
                        pltpu.VMEM((nb, BAND_PAST + tt, w), BF16), pltpu.VMEM((nb, BAND_PAST + tt, w), BF16),
                        pltpu.VMEM((n_rows, w), BF16), pltpu.VMEM((n_rows, w), BF16),
                        pltpu.VMEM((n_rows, 2 * w), BF16)],
        compiler_params=_params(), name="even_layer")(*args)
    return (x_new, utail[:, SUBLANES - 2:], k_new.reshape(b, tt, n_heads, HD_B), v_new.reshape(b, tt, n_heads, HD_B))


def _cumsum_rows(x):
    n = x.shape[0]
    row = lax.broadcasted_iota(jnp.int32, x.shape, 0)
    shift = 1
    while shift < n:
        x = x + jnp.where(row >= shift, pltpu.roll(x, shift, axis=0), 0.0)
        shift *= 2
    return x


def _unit_lower_inverses(mats):
    n = mats[0].shape[0]
    row = lax.broadcasted_iota(jnp.int32, (n, n), 0)
    col = lax.broadcasted_iota(jnp.int32, (n, n), 1)
    eye = jnp.where(row == col, 1.0, 0.0)
    xs = [-a for a in mats]
    ps = [eye + x for x in xs]
    ys = [_dot(x, x) for x in xs]
    power = 2
    while 2 * power < n:
        sts = [_dot(jnp.concatenate([p, y], axis=0), y) for p, y in zip(ps, ys)]
        ps = [p + st[0:n] for p, st in zip(ps, sts)]
        ys = [st[n:2 * n] for st in sts]
        power *= 2
    return [p + _dot(p, y) for p, y in zip(ps, ys)]


def _odd_layer_kernel(*refs, nb, tt, has_state, prep_chunks):
    x_ref, gpre_ref, win_ref, wg_ref, cw_ref, ab_ref, on_ref, wout_ref, gpost_ref = refs[:9]
    if has_state:
        hist_ref, s0_ref = refs[9:11]
        refs = refs[11:]
    else:
        refs = refs[9:]
    (xo_ref, sout_ref, ctail_ref, ctails, qs, ks, vs, zs, gsc, bsc, sbuf, ybuf,
     wbuf, qtbuf, u0buf, qkbuf, kdtbuf, eglbuf) = refs
    w = win_ref.shape[-1] // 4
    n_heads = w // HD_C
    n_rows = nb * tt
    i = pl.program_id(1)

    @pl.when(i == 0)
    def _init():
        ctails[...] = jnp.zeros(ctails.shape, F32)
        if has_state:
            for s in range(nb):
                for g in range(3):
                    ctails[s, g, SUBLANES - 3:SUBLANES, :] = hist_ref[s, :, g * w:(g + 1) * w]
            sbuf[...] = s0_ref[...]
        else:
            sbuf[...] = jnp.zeros(sbuf.shape, F32)

    hb = _pre_norm(x_ref, gpre_ref, nb)

    gl = jnp.dot(hb, wg_ref[...], preferred_element_type=F32)
    bsc[...] = 1.0 / (1.0 + jnp.exp(-gl))
    gsc[...] = -jnp.exp(ab_ref[0:1, :]) * _softplus(gl + ab_ref[1:2, :])

    for g, dst in enumerate((qs, ks, vs)):
        raw = jnp.dot(hb, win_ref[:, g * w:(g + 1) * w], preferred_element_type=F32)
        inv_scale_sq = float(HD_C) if g == 0 else 1.0
        sum_mat = jnp.full((HD_C, HD_C), inv_scale_sq, BF16)
        for s in range(nb):
            srows = slice(s * tt, (s + 1) * tt)
            c, tail = _causal_dwconv(ctails[s, g], raw[srows], cw_ref[:, g * w:(g + 1) * w])
            c = _silu(c)
            ctail_ref[s, :, g * w:(g + 1) * w] = tail
            ctails[s, g] = tail
            if g == 2:
                dst[srows, :] = c
            else:
                for h in range(n_heads):
                    cols = slice(h * HD_C, (h + 1) * HD_C)
                    ch = c[:, cols]
                    ss = jnp.dot((ch * ch).astype(BF16), sum_mat, preferred_element_type=F32)
                    dst[srows, cols] = ch * lax.rsqrt(ss + EPS * inv_scale_sq)
    zs[...] = _silu(jnp.dot(hb, win_ref[:, 3 * w:4 * w], preferred_element_type=F32)).astype(zs.dtype)

    row = lax.broadcasted_iota(jnp.int32, (CHUNK, CHUNK), 0)
    col = lax.broadcasted_iota(jnp.int32, (CHUNK, CHUNK), 1)
    incl = row >= col
    strict = row > col

    heads = range(n_heads)
    hcols = [slice(h * HD_C, (h + 1) * HD_C) for h in heads]
    n_chunks = n_rows // CHUNK

    def prepare_chunks(step, carry):
        items = []
        for j in range(prep_chunks):
            cidx = step * prep_chunks + j
            rows = pl.ds(pl.multiple_of(cidx * CHUNK, CHUNK), CHUNK)
            gc = _cumsum_rows(gsc[rows, :])
            gct = gc.T
            bet = bsc[rows, :]
            for h in heads:
                gcol = jnp.broadcast_to(gc[:, n_heads + h:n_heads + h + 1], (CHUNK, HD_C))
                items.append(dict(
                    cidx=cidx, rows=rows, h=h, gcol=gcol, grow=gct[n_heads + h:n_heads + h + 1, :],
                    b=jnp.broadcast_to(bet[:, h:h + 1], (CHUNK, HD_C))))
        for it in items:
            gcol, rows, c = it["gcol"], it["rows"], hcols[it["h"]]
            it["eg"] = jnp.exp(gcol)
            it["decay"] = jnp.exp(jnp.where(incl, gcol[:, 0:CHUNK] - it["grow"], -jnp.inf))
            it["q"], it["k"], it["v"] = qs[rows, c], ks[rows, c], vs[rows, c]
        for it in items:
            kb = it["k"].astype(BF16)
            it["qk_kk"] = lax.dot_general(jnp.concatenate([it["q"].astype(BF16), kb], axis=0), kb,
                                          (((1,), (1,)), ((), ())), preferred_element_type=F32)
        mats = [jnp.where(strict, it["b"][:, 0:CHUNK] * it["qk_kk"][CHUNK:2 * CHUNK] * it["decay"], 0.0)
                for it in items]
        tinvs = _unit_lower_inverses(mats)
        sols = [_dot(tinv, jnp.concatenate([it["v"] * it["b"], it["k"] * (it["b"] * it["eg"])], axis=1))
                for tinv, it in zip(tinvs, items)]
        for sol, it in zip(sols, items):
            gcol, rows, c, h = it["gcol"], it["rows"], hcols[it["h"]], it["h"]
            g_last = gcol[CHUNK - 1:CHUNK, :]
            u0buf[rows, c] = sol[:, 0:HD_C]
            wbuf[rows, c] = sol[:, HD_C:].astype(BF16)
            qtbuf[rows, c] = (it["q"] * it["eg"]).astype(BF16)
            qkbuf[h, rows, :] = (it["qk_kk"][0:CHUNK] * it["decay"]).astype(BF16)
            kdtbuf[h, it["cidx"]] = (it["k"] * jnp.exp(g_last - gcol)).T.astype(BF16)
            eglbuf[it["cidx"], h:h + 1, :] = jnp.exp(g_last)
        return carry

    lax.fori_loop(0, n_chunks // prep_chunks, prepare_chunks, 0)

    for cidx in range(n_chunks):
        s = cidx // (tt // CHUNK)
        rows = slice(cidx * CHUNK, (cidx + 1) * CHUNK)
        s_old = [sbuf[s, h] for h in heads]
        ws_qs = [_dot(jnp.concatenate([wbuf[rows, hcols[h]], qtbuf[rows, hcols[h]]], axis=0), s_old[h])
                 for h in heads]
        u = [u0buf[rows, hcols[h]] - ws_qs[h][0:CHUNK] for h in heads]
        o = [ws_qs[h][CHUNK:2 * CHUNK] + _dot(qkbuf[h, rows, :], u[h]) for h in heads]
        for h in heads:
            sbuf[s, h] = eglbuf[cidx, h:h + 1, :] * s_old[h] + _dot(kdtbuf[h, cidx], u[h])
        for h in heads:
            ms = jnp.mean(o[h] * o[h], axis=-1, keepdims=True)
            on = o[h] * lax.rsqrt(ms + EPS) * on_ref[...]
            ybuf[rows, hcols[h]] = (on * zs[rows, hcols[h]].astype(F32)).astype(ybuf.dtype)
    sout_ref[...] = sbuf[...]

    _project_out(ybuf[...], wout_ref, gpost_ref, x_ref, xo_ref, nb, tt)


def _odd_layer(x, hist, s0, g_pre, g_post, w_in_bf, w_gate_bf, conv_w, ab, onorm, w_out_bf):
    b, t, d = x.shape
    w = w_in_bf.shape[1] // 4
    n_heads = w // HD_C
    has_state = s0 is not None
    assert t % CHUNK == 0 and (t <= ROW_TILE or t % ROW_TILE == 0)
    state_bytes = n_heads * HD_C * HD_C * 4 * (5 if has_state else 3)
    nb, tt = _streams_per_step(b, t, state_bytes)
    n_rows = nb * tt
    n_chunks = n_rows // CHUNK
    prep_chunks = 2 if n_chunks % 2 == 0 else 1
    consts = [g_pre.reshape(1, d), w_in_bf, w_gate_bf, conv_w, ab, onorm, w_out_bf, g_post.reshape(1, d)]
    in_specs = [pl.BlockSpec((nb, tt, d), lambda bi, i: (bi, i, 0))] + [_resident(c.shape) for c in consts]
    args = [x] + consts
    if has_state:
        in_specs += [pl.BlockSpec((nb,) + hist.shape[1:], lambda bi, i: (bi, 0, 0)),
                     pl.BlockSpec((nb, n_heads, HD_C, HD_C), lambda bi, i: (bi, 0, 0, 0))]
        args += [hist, s0]
    x_new, s_new, ctail = pl.pallas_call(
        functools.partial(_odd_layer_kernel, nb=nb, tt=tt, has_state=has_state, prep_chunks=prep_chunks),
        grid=(b // nb, t // tt), in_specs=in_specs,
        out_specs=[pl.BlockSpec((nb, tt, d), lambda bi, i: (bi, i, 0)),
                   pl.BlockSpec((nb, n_heads, HD_C, HD_C), lambda bi, i: (bi, 0, 0, 0)),
                   pl.BlockSpec((nb, SUBLANES, 3 * w), lambda bi, i: (bi, 0, 0))],
        out_shape=[jax.ShapeDtypeStruct((b, t, d), F32),
                   jax.ShapeDtypeStruct((b, n_heads, HD_C, HD_C), F32),
                   jax.ShapeDtypeStruct((b, SUBLANES, 3 * w), F32)],
        scratch_shapes=[pltpu.VMEM((nb, 3, SUBLANES, w), F32),
                        pltpu.VMEM((n_rows, w), F32), pltpu.VMEM((n_rows, w), F32), pltpu.VMEM((n_rows, w), F32),
                        pltpu.VMEM((n_rows, w), BF16),
                        pltpu.VMEM((n_rows, LANES), F32), pltpu.VMEM((n_rows, LANES), F32),
                        pltpu.VMEM((nb, n_heads, HD_C, HD_C), F32),
                        pltpu.VMEM((n_rows, w), BF16),
                        pltpu.VMEM((n_rows, w), BF16), pltpu.VMEM((n_rows, w), BF16), pltpu.VMEM((n_rows, w), F32),
                        pltpu.VMEM((n_heads, n_rows, CHUNK), BF16),
                        pltpu.VMEM((n_heads, n_chunks, HD_C, CHUNK), BF16),
                        pltpu.VMEM((n_chunks, n_heads, HD_C), F32)],
        compiler_params=_params(), name="odd_layer")(*args)
    return x_new, ctail[:, SUBLANES - 3:], s_new


def _band_bias(table):
    n_heads = table.shape[0]
    assert CHUNK - 1 <= REL_CLIP < BAND - 1
    low = table[:, REL_CLIP - (CHUNK - 1):]
    high = jnp.broadcast_to(table[:, 2 * REL_CLIP:], (n_heads, BAND - 1 - REL_CLIP))
    by_dist = jnp.concatenate([low, high], axis=1)[:, ::-1]
    rows = [by_dist[:, CHUNK - 1 - qi:CHUNK - 1 - qi + BAND] for qi in range(CHUNK)]
    bias = jnp.stack(rows, axis=1).astype(F32) * LOG2E
    return bias.reshape(n_heads // 2, 2 * CHUNK, BAND)


def kernel(x_prompt, x_sample, cache_conv_a, cache_k_b, cache_v_b, state_conv_c, state_s_c, norm_pre, norm_post, w_in_even, conv_w_a, rel_bias_b, w_out_even, w_in_odd, conv_w_c, a_log_c, dt_bias_c, out_norm_c, w_out_odd):
    depth = norm_pre.shape[0]
    yp, ys = x_prompt, x_sample
    outs_p = [[] for _ in range(5)]
    outs_s = [[] for _ in range(5)]
    for layer in range(depth):
        i = layer // 2
        g_pre, g_post = norm_pre[layer], norm_post[layer]
        if layer % 2 == 0:
            w_in = w_in_even[i]
            w_blk = w_in.shape[1] // 8
            col_scale = jnp.ones((w_in.shape[1],), F32).at[4 * w_blk:5 * w_blk].set(HD_B ** -0.5 * LOG2E)
            w_in_bf = (w_in * col_scale).astype(BF16)
            w_out_bf = w_out_even[i].astype(BF16)
            bias2 = _band_bias(rel_bias_b[i])
            shared = (g_pre, g_post, w_in_bf, conv_w_a[i], bias2, w_out_bf)
            yp, hst, nk, nv = _even_layer(yp, None, None, None, *shared)
            for lst, val in zip(outs_p[:3], (hst, nk, nv)):
                lst.append(val)
            ys, hst, nk, nv = _even_layer(ys, cache_conv_a[i], cache_k_b[i], cache_v_b[i], *shared)
            for lst, val in zip(outs_s[:3], (hst, nk, nv)):
                lst.append(val)
        else:
            w_in = w_in_odd[i]
            n_heads = a_log_c.shape[1]
            n_main = w_in.shape[1] - 2 * n_heads
            w_main_bf = w_in[:, :n_main].astype(BF16)
            w_gate_bf = jnp.pad(w_in[:, n_main:], ((0, 0), (0, LANES - 2 * n_heads))).astype(BF16)
            pad = (n_heads, LANES - 2 * n_heads)
            ab = jnp.stack([jnp.pad(a_log_c[i], pad), jnp.pad(dt_bias_c[i], pad)]).astype(F32)
            shared = (g_pre, g_post, w_main_bf, w_gate_bf, conv_w_c[i], ab, out_norm_c[i].reshape(1, -1),
                      w_out_odd[i].astype(BF16))
            yp, hst, s_new = _odd_layer(yp, None, None, *shared)
            outs_p[3].append(hst)
            outs_p[4].append(s_new)
            ys, hst, s_new = _odd_layer(ys, state_conv_c[i], state_s_c[i], *shared)
            outs_s[3].append(hst)
            outs_s[4].append(s_new)
    return (yp, ys) + tuple(jnp.stack(v) for v in outs_p) + tuple(jnp.stack(v) for v in outs_s)
```

```python
import functools
import math

import jax
import jax.numpy as jnp
from jax import lax
from jax.experimental import pallas as pl
from jax.experimental.pallas import tpu as pltpu

F32 = jnp.float32
BF16 = jnp.bfloat16

EPS = 1e-6
LOG2E = math.log2(math.e)
CHUNK = 64
BAND_PAST = 8 * CHUNK
BAND = BAND_PAST + CHUNK
REL_CLIP = 4 * CHUNK
HD_B = 64
HD_C = 128
LANES = 128
SUBLANES = 8
VMEM_LIMIT_BYTES = 56 * 1024 * 1024
STATE_VMEM_BUDGET_BYTES = 16 * 1024 * 1024
ROW_TILE = 512


def _silu(x):
    return x * (1.0 / (1.0 + jnp.exp(-x)))


def _softplus(x):
    return jnp.maximum(x, 0.0) + jnp.log1p(jnp.exp(-jnp.abs(x)))


def _dot(a, b):
    return jnp.dot(a.astype(BF16), b.astype(BF16), preferred_element_type=F32)


def _load_rows(ref, nb):
    parts = [ref[s] for s in range(nb)]
    return parts[0] if nb == 1 else jnp.concatenate(parts, axis=0)


def _pre_norm(x_ref, gain_ref, nb):
    x = _load_rows(x_ref, nb)
    ms = jnp.mean(x * x, axis=-1, keepdims=True)
    return (x * lax.rsqrt(ms + EPS) * gain_ref[...]).astype(BF16)


def _project_out(y, wout_ref, gain_ref, x_ref, xo_ref, nb, tt):
    out = jnp.dot(y, wout_ref[...], preferred_element_type=F32)
    ms = jnp.mean(out * out, axis=-1, keepdims=True)
    x_new = _load_rows(x_ref, nb) + out * lax.rsqrt(ms + EPS) * gain_ref[...]
    for s in range(nb):
        xo_ref[s] = x_new[s * tt:(s + 1) * tt]


def _causal_dwconv(prev_rows, new_rows, taps):
    tt = new_rows.shape[0]
    full = jnp.concatenate([prev_rows, new_rows], axis=0)
    acc = full * taps[0:1]
    for j in range(1, taps.shape[0]):
        acc = full * taps[j:j + 1] + pltpu.roll(acc, 1, axis=0)
    return acc[SUBLANES:, :], full[tt:tt + SUBLANES, :]


def _streams_per_step(b, t, state_bytes_per_stream):
    tt = min(t, ROW_TILE)
    nb = max(1, min(ROW_TILE // tt, STATE_VMEM_BUDGET_BYTES // state_bytes_per_stream))
    while b % nb:
        nb -= 1
    return nb, tt


def _resident(shape):
    return pl.BlockSpec(shape, lambda bi, i: (0,) * len(shape), pipeline_mode=pl.Buffered(1))


def _params():
    return pltpu.CompilerParams(dimension_semantics=("parallel", "arbitrary"), vmem_limit_bytes=VMEM_LIMIT_BYTES)


def _even_layer_kernel(*refs, nb, tt, has_cache):
    x_ref, gpre_ref, win_ref, cw_ref, bias_ref, wout_ref, gpost_ref = refs[:7]
    if has_cache:
        hist_ref, kc_ref, vc_ref = refs[7:10]
        refs = refs[10:]
    else:
        refs = refs[7:]
    xo_ref, utail_ref, knew_ref, vnew_ref, utails, kbuf, vbuf, qbuf, zbuf, ybuf = refs
    w = win_ref.shape[-1] // 8
    i = pl.program_id(1)

    @pl.when(i == 0)
    def _init():
        utails[...] = jnp.zeros(utails.shape, F32)
        if has_cache:
            for s in range(nb):
                utails[s, SUBLANES - 2:SUBLANES, :] = hist_ref[s]
            kbuf[:, 0:BAND_PAST, :] = kc_ref[...]
            vbuf[:, 0:BAND_PAST, :] = vc_ref[...]
        else:
            kbuf[:, 0:BAND_PAST, :] = jnp.zeros((nb, BAND_PAST, w), BF16)
            vbuf[:, 0:BAND_PAST, :] = jnp.zeros((nb, BAND_PAST, w), BF16)

    hb = _pre_norm(x_ref, gpre_ref, nb)

    def proj(j):
        return jnp.dot(hb, win_ref[:, j * w:(j + 1) * w], preferred_element_type=F32)

    u = proj(1) * proj(2)
    gate = proj(0) * _silu(proj(3))
    for s in range(nb):
        srows = slice(s * tt, (s + 1) * tt)
        conv, tail = _causal_dwconv(utails[s], u[srows], cw_ref[...])
        utail_ref[s] = tail
        utails[s] = tail
        ybuf[srows, 0:w] = (gate[srows] * conv).astype(ybuf.dtype)

    qbuf[...] = proj(4).astype(qbuf.dtype)
    k = proj(5)
    v = proj(6)
    for s in range(nb):
        srows = slice(s * tt, (s + 1) * tt)
        knew_ref[s] = k[srows]
        vnew_ref[s] = v[srows]
        kbuf[s, BAND_PAST:BAND_PAST + tt, :] = k[srows].astype(kbuf.dtype)
        vbuf[s, BAND_PAST:BAND_PAST + tt, :] = v[srows].astype(vbuf.dtype)
    zbuf[...] = _silu(proj(7)).astype(zbuf.dtype)

    chunks_per_stream = tt // CHUNK
    n_pairs = w // LANES
    lane = lax.broadcasted_iota(jnp.int32, (CHUNK, LANES), 1)
    first_head = lane < HD_B

    def attend(first_tile):
        def band_start(c):
            missing = max(BAND_PAST - CHUNK * c, 0) if first_tile else 0
            return missing // LANES * LANES, missing % LANES

        def scores(s, c, hp):
            cols = slice(hp * LANES, (hp + 1) * LANES)
            lo, masked = band_start(c)
            r0 = s * tt + c * CHUNK
            qp = qbuf[r0:r0 + CHUNK, cols]
            zero = jnp.zeros_like(qp)
            q2 = jnp.concatenate([jnp.where(first_head, qp, zero), jnp.where(first_head, zero, qp)], axis=0)
            kp = kbuf[s, c * CHUNK + lo:c * CHUNK + BAND, cols]
            sc = lax.dot_general(q2, kp, (((1,), (1,)), ((), ())), preferred_element_type=F32)
            sc = sc + bias_ref[hp, :, lo:BAND]
            if masked:
                key_idx = lax.broadcasted_iota(jnp.int32, sc.shape, 1)
                sc = jnp.where(key_idx >= masked, sc, -jnp.inf)
            return sc

        def weighted_values(s, c, hp, sc):
            cols = slice(hp * LANES, (hp + 1) * LANES)
            lo, _ = band_start(c)
            m = jnp.max(sc, axis=-1, keepdims=True)
            e = jnp.exp2(sc - m)
            l = jnp.sum(e, axis=-1, keepdims=True)
            vp = vbuf[s, c * CHUNK + lo:c * CHUNK + BAND, cols]
            pv = jnp.dot(e.astype(BF16), vp, preferred_element_type=F32) / l
            return jnp.where(first_head, pv[0:CHUNK], pv[CHUNK:2 * CHUNK])

        items = [(s, c, hp) for s in range(nb) for c in range(chunks_per_stream) for hp in range(n_pairs)]
        skew = 3
        pending, outs = {}, {}
        for step in range(len(items) + skew):
            if step < len(items):
                pending[step] = scores(*items[step])
            j = step - skew
            if j >= 0:
                s, c, hp = items[j]
                outs[hp] = weighted_values(s, c, hp, pending.pop(j))
                if hp == n_pairs - 1:
                    rows = slice(s * tt + c * CHUNK, s * tt + (c + 1) * CHUNK)
                    o = jnp.concatenate([outs[x] for x in range(n_pairs)], axis=1)
                    ybuf[rows, w:2 * w] = (o * zbuf[rows, :].astype(F32)).astype(ybuf.dtype)

    if has_cache:
        attend(False)
    else:
        pl.when(i == 0)(functools.partial(attend, True))
        pl.when(i > 0)(functools.partial(attend, False))
    for s in range(nb):
        kbuf[s, 0:BAND_PAST, :] = kbuf[s, tt:tt + BAND_PAST, :]
        vbuf[s, 0:BAND_PAST, :] = vbuf[s, tt:tt + BAND_PAST, :]

    _project_out(ybuf[...], wout_ref, gpost_ref, x_ref, xo_ref, nb, tt)


def _even_layer(x, hist, kc, vc, g_pre, g_post, w_in_bf, conv_w, bias2, w_out_bf):
    b, t, d = x.shape
    w = w_in_bf.shape[1] // 8
    n_heads = w // HD_B
    has_cache = kc is not None
    assert t % CHUNK == 0 and (t <= ROW_TILE or t % ROW_TILE == 0)
    band_bytes = 2 * (BAND_PAST + min(t, ROW_TILE)) * w * 2
    state_bytes = band_bytes + (2 * 2 * BAND_PAST * w * 2 if has_cache else 0)
    nb, tt = _streams_per_step(b, t, state_bytes)
    n_rows = nb * tt
    consts = [g_pre.reshape(1, d), w_in_bf, conv_w, bias2, w_out_bf, g_post.reshape(1, d)]
    in_specs = [pl.BlockSpec((nb, tt, d), lambda bi, i: (bi, i, 0))] + [_resident(c.shape) for c in consts]
    args = [x] + consts
    if has_cache:
        kc = kc.reshape(b, BAND_PAST, w).astype(BF16)
        vc = vc.reshape(b, BAND_PAST, w).astype(BF16)
        in_specs += [pl.BlockSpec((nb,) + hist.shape[1:], lambda bi, i: (bi, 0, 0)),
                     pl.BlockSpec((nb, BAND_PAST, w), lambda bi, i: (bi, 0, 0)),
                     pl.BlockSpec((nb, BAND_PAST, w), lambda bi, i: (bi, 0, 0))]
        args += [hist, kc, vc]
    assert tt == min(BAND_PAST, t)
    x_new, utail, k_new, v_new = pl.pallas_call(
        functools.partial(_even_layer_kernel, nb=nb, tt=tt, has_cache=has_cache),
        grid=(b // nb, t // tt), in_specs=in_specs,
        out_specs=[pl.BlockSpec((nb, tt, d), lambda bi, i: (bi, i, 0)),
                   pl.BlockSpec((nb, SUBLANES, w), lambda bi, i: (bi, 0, 0)),
                   pl.BlockSpec((nb, tt, w), lambda bi, i: (bi, 0, 0)),
                   pl.BlockSpec((nb, tt, w), lambda bi, i: (bi, 0, 0))],
        out_shape=[jax.ShapeDtypeStruct((b, t, d), F32),
                   jax.ShapeDtypeStruct((b, SUBLANES, w), F32),
                   jax.ShapeDtypeStruct((b, tt, w), F32),
                   jax.ShapeDtypeStruct((b, tt, w), F32)],
        scratch_shapes=[pltpu.VMEM((nb, SUBLANES, w), F32),
                        pltpu.VMEM((nb, BAND_PAST + tt, w), BF16), pltpu.VMEM((nb, BAND_PAST + tt, w), BF16),
                        pltpu.VMEM((n_rows, w), BF16), pltpu.VMEM((n_rows, w), BF16),
                        pltpu.VMEM((n_rows, 2 * w), BF16)],
        compiler_params=_params(), name="even_layer")(*args)
    return (x_new, utail[:, SUBLANES - 2:], k_new.reshape(b, tt, n_heads, HD_B), v_new.reshape(b, tt, n_heads, HD_B))


def _cumsum_rows(x):
    n = x.shape[0]
    row = lax.broadcasted_iota(jnp.int32, x.shape, 0)
    shift = 1
    while shift < n:
        x = x + jnp.where(row >= shift, pltpu.roll(x, shift, axis=0), 0.0)
        shift *= 2
    return x


def _pair_masks():
    row = lax.broadcasted_iota(jnp.int32, (CHUNK, 2 * CHUNK), 0)
    lane = lax.broadcasted_iota(jnp.int32, (CHUNK, 2 * CHUNK), 1)
    first = lane < CHUNK
    col = jnp.where(first, lane, lane - CHUNK)
    return first, row, col


def _block_diag(pk, first):
    tiled = jnp.concatenate([pk, pk], axis=0).astype(BF16)
    keep = jnp.concatenate([first, jnp.logical_not(first)], axis=0)
    return jnp.where(keep, tiled, jnp.zeros_like(tiled))


def _unit_lower_inverses(mats, out, first, row, col):
    n = mats[0].shape[0]
    eye = jnp.where(row == col, 1.0, 0.0)
    xs = [-a for a in mats]
    ps = [eye + x for x in xs]
    ys = [jnp.dot(x.astype(BF16), _block_diag(x, first), preferred_element_type=F32) for x in xs]
    yield
    power = 2
    while 2 * power < n:
        sts = [jnp.dot(jnp.concatenate([p, y], axis=0).astype(BF16), _block_diag(y, first),
                       preferred_element_type=F32) for p, y in zip(ps, ys)]
        ps = [p + st[0:n] for p, st in zip(ps, sts)]
        ys = [st[n:2 * n] for st in sts]
        power *= 2
        yield
    out[:] = [p + jnp.dot(p.astype(BF16), _block_diag(y, first), preferred_element_type=F32)
              for p, y in zip(ps, ys)]
    yield


def _interleave(*generators):
    live = list(generators)
    while live:
        for g in list(live):
            try:
                next(g)
            except StopIteration:
                live.remove(g)


def _odd_layer_kernel(*refs, nb, tt, has_state, prep_chunks):
    x_ref, gpre_ref, win_ref, wg_ref, cw_ref, ab_ref, on_ref, wout_ref, gpost_ref = refs[:9]
    if has_state:
        hist_ref, s0_ref = refs[9:11]
        refs = refs[11:]
    else:
        refs = refs[9:]
    (xo_ref, sout_ref, ctail_ref, ctails, qs, ks, vs, zs, gsc, bsc, sbuf, ybuf,
     wbuf, qtbuf, u0buf, qkbuf, kdtbuf, eglbuf) = refs
    w = win_ref.shape[-1] // 4
    n_heads = w // HD_C
    n_rows = nb * tt
    i = pl.program_id(1)

    @pl.when(i == 0)
    def _init():
        ctails[...] = jnp.zeros(ctails.shape, F32)
        if has_state:
            for s in range(nb):
                for g in range(3):
                    ctails[s, g, SUBLANES - 3:SUBLANES, :] = hist_ref[s, :, g * w:(g + 1) * w]
            sbuf[...] = s0_ref[...]
        else:
            sbuf[...] = jnp.zeros(sbuf.shape, F32)

    hb = _pre_norm(x_ref, gpre_ref, nb)

    gl = jnp.dot(hb, wg_ref[...], preferred_element_type=F32)
    bsc[...] = 1.0 / (1.0 + jnp.exp(-gl))
    gsc[...] = -jnp.exp(ab_ref[0:1, :]) * _softplus(gl + ab_ref[1:2, :])

    for g, dst in enumerate((qs, ks, vs)):
        raw = jnp.dot(hb, win_ref[:, g * w:(g + 1) * w], preferred_element_type=F32)
        inv_scale_sq = float(HD_C) if g == 0 else 1.0
        sum_mat = jnp.full((HD_C, HD_C), inv_scale_sq, BF16)
        for s in range(nb):
            srows = slice(s * tt, (s + 1) * tt)
            c, tail = _causal_dwconv(ctails[s, g], raw[srows], cw_ref[:, g * w:(g + 1) * w])
            c = _silu(c)
            ctail_ref[s, :, g * w:(g + 1) * w] = tail
            ctails[s, g] = tail
            if g == 2:
                dst[srows, :] = c
            else:
                for h in range(n_heads):
                    cols = slice(h * HD_C, (h + 1) * HD_C)
                    ch = c[:, cols]
                    ss = jnp.dot((ch * ch).astype(BF16), sum_mat, preferred_element_type=F32)
                    dst[srows, cols] = ch * lax.rsqrt(ss + EPS * inv_scale_sq)
    zs[...] = _silu(jnp.dot(hb, win_ref[:, 3 * w:4 * w], preferred_element_type=F32)).astype(zs.dtype)

    first, row, col = _pair_masks()
    incl = row >= col
    strict = row > col
    assert n_heads % 2 == 0 and 2 * CHUNK == HD_C
    pairs = range(n_heads // 2)

    heads = range(n_heads)
    hcols = [slice(h * HD_C, (h + 1) * HD_C) for h in heads]
    n_chunks = n_rows // CHUNK

    def chunk_rows(cidx):
        if isinstance(cidx, int):
            return slice(cidx * CHUNK, (cidx + 1) * CHUNK)
        return pl.ds(pl.multiple_of(cidx * CHUNK, CHUNK), CHUNK)

    def prepare_chunks(step):
        items = []
        duos = []
        for j in range(prep_chunks):
            cidx = step * prep_chunks + j
            rows = chunk_rows(cidx)
            gc = _cumsum_rows(gsc[rows, :])
            gct = gc.T
            bet = bsc[rows, :]
            for h in heads:
                gcol = jnp.broadcast_to(gc[:, n_heads + h:n_heads + h + 1], (CHUNK, HD_C))
                items.append(dict(cidx=cidx, rows=rows, h=h, gcol=gcol,
                                  b=jnp.broadcast_to(bet[:, h:h + 1], (CHUNK, HD_C))))
            for p in pairs:
                ia, ib = items[-n_heads + 2 * p], items[-n_heads + 2 * p + 1]
                g0 = n_heads + 2 * p
                grow = jnp.concatenate([gct[g0:g0 + 1, :], gct[g0 + 1:g0 + 2, :]], axis=1)
                gcol2 = jnp.where(first, ia["gcol"], ib["gcol"])
                duos.append(dict(cidx=cidx, rows=rows, p=p, a=ia, b=ib, beta=jnp.where(first, ia["b"], ib["b"]),
                                 decay=jnp.exp(jnp.where(incl, gcol2 - grow, -jnp.inf))))
        for it in items:
            rows, c = it["rows"], hcols[it["h"]]
            it["eg"] = jnp.exp(it["gcol"])
            it["q"], it["k"], it["v"] = qs[rows, c], ks[rows, c], vs[rows, c]
        for it in items:
            kb = it["k"].astype(BF16)
            it["qk_kk"] = lax.dot_general(jnp.concatenate([it["q"].astype(BF16), kb], axis=0), kb,
                                          (((1,), (1,)), ((), ())), preferred_element_type=F32)
        yield
        for du in duos:
            qk_kk = jnp.concatenate([du["a"]["qk_kk"], du["b"]["qk_kk"]], axis=1)
            du["qk"] = qk_kk[0:CHUNK] * du["decay"]
            du["mat"] = jnp.where(strict, du["beta"] * qk_kk[CHUNK:2 * CHUNK] * du["decay"], 0.0)
        tinvs = [None] * len(duos)
        yield from _unit_lower_inverses([du["mat"] for du in duos], tinvs, first, row, col)
        sols = []
        for tinv, du in zip(tinvs, duos):
            rhs = jnp.concatenate(
                [jnp.concatenate([it["v"] * it["b"], it["k"] * (it["b"] * it["eg"])], axis=1)
                 for it in (du["a"], du["b"])], axis=0)
            sols.append(jnp.dot(_block_diag(tinv, first), rhs.astype(BF16), preferred_element_type=F32))
        yield
        for sol, du in zip(sols, duos):
            rows = du["rows"]
            qkbuf[du["p"], rows, :] = du["qk"].astype(BF16)
            for half, it in enumerate((du["a"], du["b"])):
                gcol, c, h = it["gcol"], hcols[it["h"]], it["h"]
                g_last = gcol[CHUNK - 1:CHUNK, :]
                sol_h = sol[half * CHUNK:(half + 1) * CHUNK]
                u0buf[rows, c] = sol_h[:, 0:HD_C]
                wbuf[rows, c] = sol_h[:, HD_C:].astype(BF16)
                qtbuf[rows, c] = (it["q"] * it["eg"]).astype(BF16)
                kdtbuf[h, it["cidx"]] = (it["k"] * jnp.exp(g_last - gcol)).T.astype(BF16)
                eglbuf[it["cidx"], h:h + 1, :] = jnp.exp(g_last)

    chunks_per_stream = tt // CHUNK

    def recur_chunks(step):
        for j in range(prep_chunks):
            cidx = step * prep_chunks + j
            rows = chunk_rows(cidx)
            s = 0 if nb == 1 else cidx // chunks_per_stream
            s_old = [sbuf[s, h] for h in heads]
            ws_qs = [_dot(jnp.concatenate([wbuf[rows, hcols[h]], qtbuf[rows, hcols[h]]], axis=0), s_old[h])
                     for h in heads]
            yield
            u = [u0buf[rows, hcols[h]] - ws_qs[h][0:CHUNK] for h in heads]
            qku = [jnp.dot(_block_diag(qkbuf[p, rows, :], first),
                           jnp.concatenate([u[2 * p], u[2 * p + 1]], axis=0).astype(BF16),
                           preferred_element_type=F32) for p in pairs]
            o = [ws_qs[h][CHUNK:2 * CHUNK] + qku[h // 2][(h % 2) * CHUNK:(h % 2 + 1) * CHUNK] for h in heads]
            for h in heads:
                sbuf[s, h] = eglbuf[cidx, h:h + 1, :] * s_old[h] + _dot(kdtbuf[h, cidx], u[h])
            yield
            for h in heads:
                ms = jnp.mean(o[h] * o[h], axis=-1, keepdims=True)
                on = o[h] * lax.rsqrt(ms + EPS) * on_ref[...]
                ybuf[rows, hcols[h]] = (on * zs[rows, hcols[h]].astype(F32)).astype(ybuf.dtype)

    n_steps = n_chunks // prep_chunks
    _interleave(prepare_chunks(0))

    def pipelined(step, carry):
        _interleave(prepare_chunks(step), recur_chunks(step - 1))
        return carry

    lax.fori_loop(1, n_steps, pipelined, 0)
    _interleave(recur_chunks(n_steps - 1))
    sout_ref[...] = sbuf[...]

    _project_out(ybuf[...], wout_ref, gpost_ref, x_ref, xo_ref, nb, tt)


def _odd_layer(x, hist, s0, g_pre, g_post, w_in_bf, w_gate_bf, conv_w, ab, onorm, w_out_bf):
    b, t, d = x.shape
    w = w_in_bf.shape[1] // 4
    n_heads = w // HD_C
    has_state = s0 is not None
    assert t % CHUNK == 0 and (t <= ROW_TILE or t % ROW_TILE == 0)
    state_bytes = n_heads * HD_C * HD_C * 4 * (5 if has_state else 3)
    nb, tt = _streams_per_step(b, t, state_bytes)
    n_rows = nb * tt
    n_chunks = n_rows // CHUNK
    prep_chunks = max(c for c in (1, 2, 4) if n_chunks % c == 0)
    consts = [g_pre.reshape(1, d), w_in_bf, w_gate_bf, conv_w, ab, onorm, w_out_bf, g_post.reshape(1, d)]
    in_specs = [pl.BlockSpec((nb, tt, d), lambda bi, i: (bi, i, 0))] + [_resident(c.shape) for c in consts]
    args = [x] + consts
    if has_state:
        in_specs += [pl.BlockSpec((nb,) + hist.shape[1:], lambda bi, i: (bi, 0, 0)),
                     pl.BlockSpec((nb, n_heads, HD_C, HD_C), lambda bi, i: (bi, 0, 0, 0))]
        args += [hist, s0]
    x_new, s_new, ctail = pl.pallas_call(
        functools.partial(_odd_layer_kernel, nb=nb, tt=tt, has_state=has_state, prep_chunks=prep_chunks),
        grid=(b // nb, t // tt), in_specs=in_specs,
        out_specs=[pl.BlockSpec((nb, tt, d), lambda bi, i: (bi, i, 0)),
                   pl.BlockSpec((nb, n_heads, HD_C, HD_C), lambda bi, i: (bi, 0, 0, 0)),
                   pl.BlockSpec((nb, SUBLANES, 3 * w), lambda bi, i: (bi, 0, 0))],
        out_shape=[jax.ShapeDtypeStruct((b, t, d), F32),
                   jax.ShapeDtypeStruct((b, n_heads, HD_C, HD_C), F32),
                   jax.ShapeDtypeStruct((b, SUBLANES, 3 * w), F32)],
        scratch_shapes=[pltpu.VMEM((nb, 3, SUBLANES, w), F32),
                        pltpu.VMEM((n_rows, w), F32), pltpu.VMEM((n_rows, w), F32), pltpu.VMEM((n_rows, w), F32),
                        pltpu.VMEM((n_rows, w), BF16),
                        pltpu.VMEM((n_rows, LANES), F32), pltpu.VMEM((n_rows, LANES), F32),
                        pltpu.VMEM((nb, n_heads, HD_C, HD_C), F32),
                        pltpu.VMEM((n_rows, w), BF16),
                        pltpu.VMEM((n_rows, w), BF16), pltpu.VMEM((n_rows, w), BF16), pltpu.VMEM((n_rows, w), F32),
                        pltpu.VMEM((n_heads // 2, n_rows, 2 * CHUNK), BF16),
                        pltpu.VMEM((n_heads, n_chunks, HD_C, CHUNK), BF16),
                        pltpu.VMEM((n_chunks, n_heads, HD_C), F32)],
        compiler_params=_params(), name="odd_layer")(*args)
    return x_new, ctail[:, SUBLANES - 3:], s_new


def _band_bias(table):
    n_heads = table.shape[0]
    assert CHUNK - 1 <= REL_CLIP < BAND - 1
    low = table[:, REL_CLIP - (CHUNK - 1):]
    high = jnp.broadcast_to(table[:, 2 * REL_CLIP:], (n_heads, BAND - 1 - REL_CLIP))
    by_dist = jnp.concatenate([low, high], axis=1)[:, ::-1]
    rows = [by_dist[:, CHUNK - 1 - qi:CHUNK - 1 - qi + BAND] for qi in range(CHUNK)]
    bias = jnp.stack(rows, axis=1).astype(F32) * LOG2E
    return bias.reshape(n_heads // 2, 2 * CHUNK, BAND)


def kernel(x_prompt, x_sample, cache_conv_a, cache_k_b, cache_v_b, state_conv_c, state_s_c, norm_pre, norm_post, w_in_even, conv_w_a, rel_bias_b, w_out_even, w_in_odd, conv_w_c, a_log_c, dt_bias_c, out_norm_c, w_out_odd):
    depth = norm_pre.shape[0]
    yp, ys = x_prompt, x_sample
    outs_p = [[] for _ in range(5)]
    outs_s = [[] for _ in range(5)]
    for layer in range(depth):
        i = layer // 2
        g_pre, g_post = norm_pre[layer], norm_post[layer]
        if layer % 2 == 0:
            w_in = w_in_even[i]
            w_blk = w_in.shape[1] // 8
            col_scale = jnp.ones((w_in.shape[1],), F32).at[4 * w_blk:5 * w_blk].set(HD_B ** -0.5 * LOG2E)
            w_in_bf = (w_in * col_scale).astype(BF16)
            w_out_bf = w_out_even[i].astype(BF16)
            bias2 = _band_bias(rel_bias_b[i])
            shared = (g_pre, g_post, w_in_bf, conv_w_a[i], bias2, w_out_bf)
            yp, hst, nk, nv = _even_layer(yp, None, None, None, *shared)
            for lst, val in zip(outs_p[:3], (hst, nk, nv)):
                lst.append(val)
            ys, hst, nk, nv = _even_layer(ys, cache_conv_a[i], cache_k_b[i], cache_v_b[i], *shared)
            for lst, val in zip(outs_s[:3], (hst, nk, nv)):
                lst.append(val)
        else:
            w_in = w_in_odd[i]
            n_heads = a_log_c.shape[1]
            n_main = w_in.shape[1] - 2 * n_heads
            w_main_bf = w_in[:, :n_main].astype(BF16)
            w_gate_bf = jnp.pad(w_in[:, n_main:], ((0, 0), (0, LANES - 2 * n_heads))).astype(BF16)
            pad = (n_heads, LANES - 2 * n_heads)
            ab = jnp.stack([jnp.pad(a_log_c[i], pad), jnp.pad(dt_bias_c[i], pad)]).astype(F32)
            shared = (g_pre, g_post, w_main_bf, w_gate_bf, conv_w_c[i], ab, out_norm_c[i].reshape(1, -1),
                      w_out_odd[i].astype(BF16))
            yp, hst, s_new = _odd_layer(yp, None, None, *shared)
            outs_p[3].append(hst)
            outs_p[4].append(s_new)
            ys, hst, s_new = _odd_layer(ys, state_conv_c[i], state_s_c[i], *shared)
            outs_s[3].append(hst)
            outs_s[4].append(s_new)
    return (yp, ys) + tuple(jnp.stack(v) for v in outs_p) + tuple(jnp.stack(v) for v in outs_s)
```

```python
import functools
import math

import jax
import jax.numpy as jnp
from jax import lax
from jax.experimental import pallas as pl
from jax.experimental.pallas import tpu as pltpu

F32 = jnp.float32
BF16 = jnp.bfloat16

EPS = 1e-6
LOG2E = math.log2(math.e)
CHUNK = 64
BAND_PAST = 8 * CHUNK
BAND = BAND_PAST + CHUNK
REL_CLIP = 4 * CHUNK
HD_B = 64
HD_C = 128
LANES = 128
SUBLANES = 8
VMEM_LIMIT_BYTES = 56 * 1024 * 1024
STATE_VMEM_BUDGET_BYTES = 16 * 1024 * 1024
ROW_TILE = 512
CAST_COLS = 512


def _silu(x):
    return x * (1.0 / (1.0 + jnp.exp(-x)))


def _softplus(x):
    return jnp.maximum(x, 0.0) + jnp.log1p(jnp.exp(-jnp.abs(x)))


def _dot(a, b):
    return jnp.dot(a.astype(BF16), b.astype(BF16), preferred_element_type=F32)


def _load_rows(ref, nb):
    parts = [ref[s] for s in range(nb)]
    return parts[0] if nb == 1 else jnp.concatenate(parts, axis=0)


def _pre_norm(x_ref, gain_ref, nb):
    x = _load_rows(x_ref, nb)
    ms = jnp.mean(x * x, axis=-1, keepdims=True)
    return (x * lax.rsqrt(ms + EPS) * gain_ref[...]).astype(BF16)


def _project_out(y, wout_ref, gain_ref, x_ref, xo_ref, nb, tt):
    out = jnp.dot(y, wout_ref[...], preferred_element_type=F32)
    ms = jnp.mean(out * out, axis=-1, keepdims=True)
    x_new = _load_rows(x_ref, nb) + out * lax.rsqrt(ms + EPS) * gain_ref[...]
    for s in range(nb):
        xo_ref[s] = x_new[s * tt:(s + 1) * tt]


def _causal_dwconv(prev_rows, new_rows, taps):
    tt = new_rows.shape[0]
    full = jnp.concatenate([prev_rows, new_rows], axis=0)
    acc = full * taps[0:1]
    for j in range(1, taps.shape[0]):
        acc = full * taps[j:j + 1] + pltpu.roll(acc, 1, axis=0)
    return acc[SUBLANES:, :], full[tt:tt + SUBLANES, :]


def _streams_per_step(b, t, state_bytes_per_stream):
    tt = min(t, ROW_TILE)
    nb = max(1, min(ROW_TILE // tt, STATE_VMEM_BUDGET_BYTES // state_bytes_per_stream))
    while b % nb:
        nb -= 1
    return nb, tt


def _resident(shape):
    return pl.BlockSpec(shape, lambda bi, i: (0,) * len(shape), pipeline_mode=pl.Buffered(1))


def _params():
    return pltpu.CompilerParams(dimension_semantics=("parallel", "arbitrary"), vmem_limit_bytes=VMEM_LIMIT_BYTES)


def _even_layer_kernel(*refs, nb, tt, has_cache, n_alias):
    x_ref, gpre_ref, win_ref, cw_ref, bias_ref, wout_ref, gpost_ref = refs[:7]
    if has_cache:
        hist_ref, kc_ref, vc_ref = refs[7:10]
        refs = refs[10:]
    else:
        refs = refs[7:]
    refs = refs[n_alias:]
    xo_ref, utail_ref, knew_ref, vnew_ref, utails, kbuf, vbuf, qbuf, zbuf, ybuf = refs
    w = win_ref.shape[-1] // 8
    i = pl.program_id(1)

    @pl.when(i == 0)
    def _init():
        utails[...] = jnp.zeros(utails.shape, F32)
        if has_cache:
            for s in range(nb):
                utails[s, SUBLANES - 2:SUBLANES, :] = hist_ref[s]
            kbuf[:, 0:BAND_PAST, :] = kc_ref[...]
            vbuf[:, 0:BAND_PAST, :] = vc_ref[...]
        else:
            kbuf[:, 0:BAND_PAST, :] = jnp.zeros((nb, BAND_PAST, w), BF16)
            vbuf[:, 0:BAND_PAST, :] = jnp.zeros((nb, BAND_PAST, w), BF16)

    hb = _pre_norm(x_ref, gpre_ref, nb)

    def proj(j):
        return jnp.dot(hb, win_ref[:, j * w:(j + 1) * w], preferred_element_type=F32)

    u = proj(1) * proj(2)
    gate = proj(0) * _silu(proj(3))
    for s in range(nb):
        srows = slice(s * tt, (s + 1) * tt)
        conv, tail = _causal_dwconv(utails[s], u[srows], cw_ref[...])
        utail_ref[s] = tail
        utails[s] = tail
        ybuf[srows, 0:w] = (gate[srows] * conv).astype(ybuf.dtype)

    qbuf[...] = proj(4).astype(qbuf.dtype)
    k = proj(5)
    v = proj(6)
    for s in range(nb):
        srows = slice(s * tt, (s + 1) * tt)
        knew_ref[s] = k[srows]
        vnew_ref[s] = v[srows]
        kbuf[s, BAND_PAST:BAND_PAST + tt, :] = k[srows].astype(kbuf.dtype)
        vbuf[s, BAND_PAST:BAND_PAST + tt, :] = v[srows].astype(vbuf.dtype)
    zbuf[...] = _silu(proj(7)).astype(zbuf.dtype)

    chunks_per_stream = tt // CHUNK
    n_pairs = w // LANES
    lane = lax.broadcasted_iota(jnp.int32, (CHUNK, LANES), 1)
    first_head = lane < HD_B

    def attend(first_tile):
        def band_start(c):
            missing = max(BAND_PAST - CHUNK * c, 0) if first_tile else 0
            return missing // LANES * LANES, missing % LANES

        def scores(s, c, hp):
            cols = slice(hp * LANES, (hp + 1) * LANES)
            lo, masked = band_start(c)
            r0 = s * tt + c * CHUNK
            qp = qbuf[r0:r0 + CHUNK, cols]
            zero = jnp.zeros_like(qp)
            q2 = jnp.concatenate([jnp.where(first_head, qp, zero), jnp.where(first_head, zero, qp)], axis=0)
            kp = kbuf[s, c * CHUNK + lo:c * CHUNK + BAND, cols]
            sc = lax.dot_general(q2, kp, (((1,), (1,)), ((), ())), preferred_element_type=F32)
            sc = sc + bias_ref[hp, :, lo:BAND]
            if masked:
                key_idx = lax.broadcasted_iota(jnp.int32, sc.shape, 1)
                sc = jnp.where(key_idx >= masked, sc, -jnp.inf)
            return sc

        def weighted_values(s, c, hp, sc):
            cols = slice(hp * LANES, (hp + 1) * LANES)
            lo, _ = band_start(c)
            m = jnp.max(sc, axis=-1, keepdims=True)
            e = jnp.exp2(sc - m)
            l = jnp.sum(e, axis=-1, keepdims=True)
            vp = vbuf[s, c * CHUNK + lo:c * CHUNK + BAND, cols]
            pv = jnp.dot(e.astype(BF16), vp, preferred_element_type=F32) / l
            return jnp.where(first_head, pv[0:CHUNK], pv[CHUNK:2 * CHUNK])

        items = [(s, c, hp) for s in range(nb) for c in range(chunks_per_stream) for hp in range(n_pairs)]
        skew = 3
        pending, outs = {}, {}
        for step in range(len(items) + skew):
            if step < len(items):
                pending[step] = scores(*items[step])
            j = step - skew
            if j >= 0:
                s, c, hp = items[j]
                outs[hp] = weighted_values(s, c, hp, pending.pop(j))
                if hp == n_pairs - 1:
                    rows = slice(s * tt + c * CHUNK, s * tt + (c + 1) * CHUNK)
                    o = jnp.concatenate([outs[x] for x in range(n_pairs)], axis=1)
                    ybuf[rows, w:2 * w] = (o * zbuf[rows, :].astype(F32)).astype(ybuf.dtype)

    if has_cache:
        attend(False)
    else:
        pl.when(i == 0)(functools.partial(attend, True))
        pl.when(i > 0)(functools.partial(attend, False))
    for s in range(nb):
        kbuf[s, 0:BAND_PAST, :] = kbuf[s, tt:tt + BAND_PAST, :]
        vbuf[s, 0:BAND_PAST, :] = vbuf[s, tt:tt + BAND_PAST, :]

    _project_out(ybuf[...], wout_ref, gpost_ref, x_ref, xo_ref, nb, tt)


def _stacked_out(shape, block, slot, n_slots):
    spec = pl.BlockSpec((None,) + block, lambda bi, i: (slot, bi) + (0,) * (len(block) - 1))
    return spec, jax.ShapeDtypeStruct((n_slots,) + shape, F32)


def _even_layer(x, hist, kc, vc, g_pre, g_post, w_in_bf, conv_w, bias2, w_out_bf, slot, n_slots, kv_stacks):
    b, t, d = x.shape
    w = w_in_bf.shape[1] // 8
    n_heads = w // HD_B
    has_cache = kc is not None
    assert t % CHUNK == 0 and (t <= ROW_TILE or t % ROW_TILE == 0)
    band_bytes = 2 * (BAND_PAST + min(t, ROW_TILE)) * w * 2
    state_bytes = band_bytes + (2 * 2 * BAND_PAST * w * 2 if has_cache else 0)
    nb, tt = _streams_per_step(b, t, state_bytes)
    n_rows = nb * tt
    consts = [g_pre.reshape(1, d), w_in_bf, conv_w, bias2, w_out_bf, g_post.reshape(1, d)]
    in_specs = [pl.BlockSpec((nb, tt, d), lambda bi, i: (bi, i, 0))] + [_resident(c.shape) for c in consts]
    args = [x] + consts
    if has_cache:
        kc = kc.reshape(b, BAND_PAST, w).astype(BF16)
        vc = vc.reshape(b, BAND_PAST, w).astype(BF16)
        in_specs += [pl.BlockSpec((nb,) + hist.shape[1:], lambda bi, i: (bi, 0, 0)),
                     pl.BlockSpec((nb, BAND_PAST, w), lambda bi, i: (bi, 0, 0)),
                     pl.BlockSpec((nb, BAND_PAST, w), lambda bi, i: (bi, 0, 0))]
        args += [hist, kc, vc]
    assert tt == min(BAND_PAST, t)
    kv_spec, kv_shape = _stacked_out((b, tt, w), (nb, tt, w), slot, n_slots)
    aliases = {}
    if kv_stacks is not None:
        aliases = {len(args): 2, len(args) + 1: 3}
        in_specs += [pl.BlockSpec(memory_space=pl.ANY)] * 2
        args += list(kv_stacks)
    x_new, utail, k_stack, v_stack = pl.pallas_call(
        functools.partial(_even_layer_kernel, nb=nb, tt=tt, has_cache=has_cache, n_alias=len(aliases)),
        grid=(b // nb, t // tt), in_specs=in_specs,
        out_specs=[pl.BlockSpec((nb, tt, d), lambda bi, i: (bi, i, 0)),
                   pl.BlockSpec((nb, SUBLANES, w), lambda bi, i: (bi, 0, 0)),
                   kv_spec, kv_spec],
        out_shape=[jax.ShapeDtypeStruct((b, t, d), F32),
                   jax.ShapeDtypeStruct((b, SUBLANES, w), F32),
                   kv_shape, kv_shape],
        input_output_aliases=aliases,
        scratch_shapes=[pltpu.VMEM((nb, SUBLANES, w), F32),
                        pltpu.VMEM((nb, BAND_PAST + tt, w), BF16), pltpu.VMEM((nb, BAND_PAST + tt, w), BF16),
                        pltpu.VMEM((n_rows, w), BF16), pltpu.VMEM((n_rows, w), BF16),
                        pltpu.VMEM((n_rows, 2 * w), BF16)],
        compiler_params=_params(), name="even_layer")(*args)
    del n_heads
    return x_new, utail[:, SUBLANES - 2:], (k_stack, v_stack)


def _cumsum_rows(x):
    n = x.shape[0]
    row = lax.broadcasted_iota(jnp.int32, x.shape, 0)
    shift = 1
    while shift < n:
        x = x + jnp.where(row >= shift, pltpu.roll(x, shift, axis=0), 0.0)
        shift *= 2
    return x


def _pair_masks():
    row = lax.broadcasted_iota(jnp.int32, (CHUNK, 2 * CHUNK), 0)
    lane = lax.broadcasted_iota(jnp.int32, (CHUNK, 2 * CHUNK), 1)
    first = lane < CHUNK
    col = jnp.where(first, lane, lane - CHUNK)
    return first, row, col


def _block_diag(pk, first):
    tiled = jnp.concatenate([pk, pk], axis=0).astype(BF16)
    keep = jnp.concatenate([first, jnp.logical_not(first)], axis=0)
    return jnp.where(keep, tiled, jnp.zeros_like(tiled))


def _unit_lower_inverses(mats, out, first, row, col):
    n = mats[0].shape[0]
    eye = jnp.where(row == col, 1.0, 0.0)
    xs = [-a for a in mats]
    ps = [eye + x for x in xs]
    ys = [jnp.dot(x.astype(BF16), _block_diag(x, first), preferred_element_type=F32) for x in xs]
    yield
    power = 2
    while 2 * power < n:
        sts = [jnp.dot(jnp.concatenate([p, y], axis=0).astype(BF16), _block_diag(y, first),
                       preferred_element_type=F32) for p, y in zip(ps, ys)]
        ps = [p + st[0:n] for p, st in zip(ps, sts)]
        ys = [st[n:2 * n] for st in sts]
        power *= 2
        yield
    out[:] = [p + jnp.dot(p.astype(BF16), _block_diag(y, first), preferred_element_type=F32)
              for p, y in zip(ps, ys)]
    yield


def _interleave(*generators):
    live = list(generators)
    while live:
        for g in list(live):
            try:
                next(g)
            except StopIteration:
                live.remove(g)


def _odd_layer_kernel(*refs, nb, tt, has_state, prep_chunks, n_alias):
    x_ref, gpre_ref, win_ref, wg_ref, cw_ref, ab_ref, on_ref, wout_ref, gpost_ref = refs[:9]
    if has_state:
        hist_ref, s0_ref = refs[9:11]
        refs = refs[11:]
    else:
        refs = refs[9:]
    refs = refs[n_alias:]
    (xo_ref, sout_ref, ctail_ref, ctails, qs, ks, vs, zs, gsc, bsc, sbuf, ybuf,
     wbuf, qtbuf, u0buf, qkbuf, kdtbuf, eglbuf) = refs
    w = win_ref.shape[-1] // 4
    n_heads = w // HD_C
    n_rows = nb * tt
    i = pl.program_id(1)

    @pl.when(i == 0)
    def _init():
        ctails[...] = jnp.zeros(ctails.shape, F32)
        if has_state:
            for s in range(nb):
                for g in range(3):
                    ctails[s, g, SUBLANES - 3:SUBLANES, :] = hist_ref[s, :, g * w:(g + 1) * w]
            sbuf[...] = s0_ref[...]
        else:
            sbuf[...] = jnp.zeros(sbuf.shape, F32)

    hb = _pre_norm(x_ref, gpre_ref, nb)

    gl = jnp.dot(hb, wg_ref[...], preferred_element_type=F32)
    bsc[...] = 1.0 / (1.0 + jnp.exp(-gl))
    gsc[...] = -jnp.exp(ab_ref[0:1, :]) * _softplus(gl + ab_ref[1:2, :])

    for g, dst in enumerate((qs, ks, vs)):
        raw = jnp.dot(hb, win_ref[:, g * w:(g + 1) * w], preferred_element_type=F32)
        inv_scale_sq = float(HD_C) if g == 0 else 1.0
        sum_mat = jnp.full((HD_C, HD_C), inv_scale_sq, BF16)
        for s in range(nb):
            srows = slice(s * tt, (s + 1) * tt)
            c, tail = _causal_dwconv(ctails[s, g], raw[srows], cw_ref[:, g * w:(g + 1) * w])
            c = _silu(c)
            ctail_ref[s, :, g * w:(g + 1) * w] = tail
            ctails[s, g] = tail
            if g == 2:
                dst[srows, :] = c
            else:
                for h in range(n_heads):
                    cols = slice(h * HD_C, (h + 1) * HD_C)
                    ch = c[:, cols]
                    ss = jnp.dot((ch * ch).astype(BF16), sum_mat, preferred_element_type=F32)
                    dst[srows, cols] = ch * lax.rsqrt(ss + EPS * inv_scale_sq)
    zs[...] = _silu(jnp.dot(hb, win_ref[:, 3 * w:4 * w], preferred_element_type=F32)).astype(zs.dtype)

    first, row, col = _pair_masks()
    incl = row >= col
    strict = row > col
    assert n_heads % 2 == 0 and 2 * CHUNK == HD_C
    pairs = range(n_heads // 2)

    heads = range(n_heads)
    hcols = [slice(h * HD_C, (h + 1) * HD_C) for h in heads]
    n_chunks = n_rows // CHUNK

    def chunk_rows(cidx):
        if isinstance(cidx, int):
            return slice(cidx * CHUNK, (cidx + 1) * CHUNK)
        return pl.ds(pl.multiple_of(cidx * CHUNK, CHUNK), CHUNK)

    def prepare_chunks(step):
        items = []
        duos = []
        for j in range(prep_chunks):
            cidx = step * prep_chunks + j
            rows = chunk_rows(cidx)
            gc = _cumsum_rows(gsc[rows, :])
            gct = gc.T
            bet = bsc[rows, :]
            for h in heads:
                gcol = jnp.broadcast_to(gc[:, n_heads + h:n_heads + h + 1], (CHUNK, HD_C))
                items.append(dict(cidx=cidx, rows=rows, h=h, gcol=gcol,
                                  b=jnp.broadcast_to(bet[:, h:h + 1], (CHUNK, HD_C))))
            for p in pairs:
                ia, ib = items[-n_heads + 2 * p], items[-n_heads + 2 * p + 1]
                g0 = n_heads + 2 * p
                grow = jnp.concatenate([gct[g0:g0 + 1, :], gct[g0 + 1:g0 + 2, :]], axis=1)
                gcol2 = jnp.where(first, ia["gcol"], ib["gcol"])
                duos.append(dict(cidx=cidx, rows=rows, p=p, a=ia, b=ib, beta=jnp.where(first, ia["b"], ib["b"]),
                                 decay=jnp.exp(jnp.where(incl, gcol2 - grow, -jnp.inf))))
        for it in items:
            rows, c = it["rows"], hcols[it["h"]]
            it["eg"] = jnp.exp(it["gcol"])
            it["q"], it["k"], it["v"] = qs[rows, c], ks[rows, c], vs[rows, c]
        for it in items:
            kb = it["k"].astype(BF16)
            it["qk_kk"] = lax.dot_general(jnp.concatenate([it["q"].astype(BF16), kb], axis=0), kb,
                                          (((1,), (1,)), ((), ())), preferred_element_type=F32)
        yield
        for du in duos:
            qk_kk = jnp.concatenate([du["a"]["qk_kk"], du["b"]["qk_kk"]], axis=1)
            du["qk"] = qk_kk[0:CHUNK] * du["decay"]
            du["mat"] = jnp.where(strict, du["beta"] * qk_kk[CHUNK:2 * CHUNK] * du["decay"], 0.0)
        tinvs = [None] * len(duos)
        yield from _unit_lower_inverses([du["mat"] for du in duos], tinvs, first, row, col)
        sols = []
        for tinv, du in zip(tinvs, duos):
            rhs = jnp.concatenate(
                [jnp.concatenate([it["v"] * it["b"], it["k"] * (it["b"] * it["eg"])], axis=1)
                 for it in (du["a"], du["b"])], axis=0)
            sols.append(jnp.dot(_block_diag(tinv, first), rhs.astype(BF16), preferred_element_type=F32))
        yield
        for sol, du in zip(sols, duos):
            rows = du["rows"]
            qkbuf[du["p"], rows, :] = du["qk"].astype(BF16)
            for half, it in enumerate((du["a"], du["b"])):
                gcol, c, h = it["gcol"], hcols[it["h"]], it["h"]
                g_last = gcol[CHUNK - 1:CHUNK, :]
                sol_h = sol[half * CHUNK:(half + 1) * CHUNK]
                u0buf[rows, c] = sol_h[:, 0:HD_C]
                wbuf[rows, c] = sol_h[:, HD_C:].astype(BF16)
                qtbuf[rows, c] = (it["q"] * it["eg"]).astype(BF16)
                kdtbuf[h, it["cidx"]] = (it["k"] * jnp.exp(g_last - gcol)).T.astype(BF16)
                eglbuf[it["cidx"], h:h + 1, :] = jnp.exp(g_last)

    chunks_per_stream = tt // CHUNK

    def recur_chunks(step):
        for j in range(prep_chunks):
            cidx = step * prep_chunks + j
            rows = chunk_rows(cidx)
            s = 0 if nb == 1 else cidx // chunks_per_stream
            s_old = [sbuf[s, h] for h in heads]
            ws_qs = [_dot(jnp.concatenate([wbuf[rows, hcols[h]], qtbuf[rows, hcols[h]]], axis=0), s_old[h])
                     for h in heads]
            yield
            u = [u0buf[rows, hcols[h]] - ws_qs[h][0:CHUNK] for h in heads]
            qku = [jnp.dot(_block_diag(qkbuf[p, rows, :], first),
                           jnp.concatenate([u[2 * p], u[2 * p + 1]], axis=0).astype(BF16),
                           preferred_element_type=F32) for p in pairs]
            o = [ws_qs[h][CHUNK:2 * CHUNK] + qku[h // 2][(h % 2) * CHUNK:(h % 2 + 1) * CHUNK] for h in heads]
            for h in heads:
                sbuf[s, h] = eglbuf[cidx, h:h + 1, :] * s_old[h] + _dot(kdtbuf[h, cidx], u[h])
            yield
            for h in heads:
                ms = jnp.mean(o[h] * o[h], axis=-1, keepdims=True)
                on = o[h] * lax.rsqrt(ms + EPS) * on_ref[...]
                ybuf[rows, hcols[h]] = (on * zs[rows, hcols[h]].astype(F32)).astype(ybuf.dtype)

    n_steps = n_chunks // prep_chunks
    _interleave(prepare_chunks(0))

    def pipelined(step, carry):
        _interleave(prepare_chunks(step), recur_chunks(step - 1))
        return carry

    lax.fori_loop(1, n_steps, pipelined, 0)
    _interleave(recur_chunks(n_steps - 1))
    sout_ref[...] = sbuf[...]

    _project_out(ybuf[...], wout_ref, gpost_ref, x_ref, xo_ref, nb, tt)


def _odd_layer(x, hist, s0, g_pre, g_post, w_in_bf, w_gate_bf, conv_w, ab, onorm, w_out_bf, slot, n_slots, s_stack):
    b, t, d = x.shape
    w = w_in_bf.shape[1] // 4
    n_heads = w // HD_C
    has_state = s0 is not None
    assert t % CHUNK == 0 and (t <= ROW_TILE or t % ROW_TILE == 0)
    state_bytes = n_heads * HD_C * HD_C * 4 * (5 if has_state else 3)
    nb, tt = _streams_per_step(b, t, state_bytes)
    n_rows = nb * tt
    n_chunks = n_rows // CHUNK
    prep_chunks = max(c for c in (1, 2, 4) if n_chunks % c == 0)
    consts = [g_pre.reshape(1, d), w_in_bf, w_gate_bf, conv_w, ab, onorm, w_out_bf, g_post.reshape(1, d)]
    in_specs = [pl.BlockSpec((nb, tt, d), lambda bi, i: (bi, i, 0))] + [_resident(c.shape) for c in consts]
    args = [x] + consts
    if has_state:
        in_specs += [pl.BlockSpec((nb,) + hist.shape[1:], lambda bi, i: (bi, 0, 0)),
                     pl.BlockSpec((nb, n_heads, HD_C, HD_C), lambda bi, i: (bi, 0, 0, 0))]
        args += [hist, s0]
    s_spec, s_shape = _stacked_out((b, n_heads, HD_C, HD_C), (nb, n_heads, HD_C, HD_C), slot, n_slots)
    aliases = {}
    if s_stack is not None:
        aliases = {len(args): 1}
        in_specs.append(pl.BlockSpec(memory_space=pl.ANY))
        args.append(s_stack)
    x_new, s_new, ctail = pl.pallas_call(
        functools.partial(_odd_layer_kernel, nb=nb, tt=tt, has_state=has_state, prep_chunks=prep_chunks,
                          n_alias=len(aliases)),
        grid=(b // nb, t // tt), in_specs=in_specs,
        out_specs=[pl.BlockSpec((nb, tt, d), lambda bi, i: (bi, i, 0)),
                   s_spec,
                   pl.BlockSpec((nb, SUBLANES, 3 * w), lambda bi, i: (bi, 0, 0))],
        out_shape=[jax.ShapeDtypeStruct((b, t, d), F32),
                   s_shape,
                   jax.ShapeDtypeStruct((b, SUBLANES, 3 * w), F32)],
        input_output_aliases=aliases,
        scratch_shapes=[pltpu.VMEM((nb, 3, SUBLANES, w), F32),
                        pltpu.VMEM((n_rows, w), F32), pltpu.VMEM((n_rows, w), F32), pltpu.VMEM((n_rows, w), F32),
                        pltpu.VMEM((n_rows, w), BF16),
                        pltpu.VMEM((n_rows, LANES), F32), pltpu.VMEM((n_rows, LANES), F32),
                        pltpu.VMEM((nb, n_heads, HD_C, HD_C), F32),
                        pltpu.VMEM((n_rows, w), BF16),
                        pltpu.VMEM((n_rows, w), BF16), pltpu.VMEM((n_rows, w), BF16), pltpu.VMEM((n_rows, w), F32),
                        pltpu.VMEM((n_heads // 2, n_rows, 2 * CHUNK), BF16),
                        pltpu.VMEM((n_heads, n_chunks, HD_C, CHUNK), BF16),
                        pltpu.VMEM((n_chunks, n_heads, HD_C), F32)],
        compiler_params=_params(), name="odd_layer")(*args)
    return x_new, ctail[:, SUBLANES - 3:], s_new


def _cast_kernel(w_ref, *rest):
    if len(rest) == 2:
        scale_ref, o_ref = rest
        o_ref[...] = (w_ref[...] * scale_ref[...]).astype(o_ref.dtype)
    else:
        (o_ref,) = rest
        o_ref[...] = w_ref[...].astype(o_ref.dtype)


def _to_bf16(w_all, layer, n_cols, col_scale=None):
    k = w_all.shape[1]
    assert n_cols % CAST_COLS == 0
    in_specs = [pl.BlockSpec((None, k, CAST_COLS), lambda j: (layer, 0, j))]
    args = [w_all]
    if col_scale is not None:
        in_specs.append(pl.BlockSpec((1, CAST_COLS), lambda j: (0, j)))
        args.append(col_scale.reshape(1, n_cols))
    return pl.pallas_call(
        _cast_kernel, grid=(n_cols // CAST_COLS,), in_specs=in_specs,
        out_specs=pl.BlockSpec((k, CAST_COLS), lambda j: (0, j)),
        out_shape=jax.ShapeDtypeStruct((k, n_cols), BF16),
        compiler_params=pltpu.CompilerParams(dimension_semantics=("parallel",)), name="to_bf16")(*args)


def _band_bias(table):
    n_heads = table.shape[0]
    assert CHUNK - 1 <= REL_CLIP < BAND - 1
    low = table[:, REL_CLIP - (CHUNK - 1):]
    high = jnp.broadcast_to(table[:, 2 * REL_CLIP:], (n_heads, BAND - 1 - REL_CLIP))
    by_dist = jnp.concatenate([low, high], axis=1)[:, ::-1]
    rows = [by_dist[:, CHUNK - 1 - qi:CHUNK - 1 - qi + BAND] for qi in range(CHUNK)]
    bias = jnp.stack(rows, axis=1).astype(F32) * LOG2E
    return bias.reshape(n_heads // 2, 2 * CHUNK, BAND)


def kernel(x_prompt, x_sample, cache_conv_a, cache_k_b, cache_v_b, state_conv_c, state_s_c, norm_pre, norm_post, w_in_even, conv_w_a, rel_bias_b, w_out_even, w_in_odd, conv_w_c, a_log_c, dt_bias_c, out_norm_c, w_out_odd):
    depth = norm_pre.shape[0]
    n_even, n_odd = w_in_even.shape[0], w_in_odd.shape[0]
    n_heads_b = cache_k_b.shape[-2]
    yp, ys = x_prompt, x_sample
    conv_a_p, conv_a_s, conv_c_p, conv_c_s = [], [], [], []
    kv_p = kv_s = s_p = s_s = None
    for layer in range(depth):
        i = layer // 2
        g_pre, g_post = norm_pre[layer], norm_post[layer]
        if layer % 2 == 0:
            n_in = w_in_even.shape[2]
            w_blk = n_in // 8
            col_scale = jnp.ones((n_in,), F32).at[4 * w_blk:5 * w_blk].set(HD_B ** -0.5 * LOG2E)
            w_in_bf = _to_bf16(w_in_even, i, n_in, col_scale)
            w_out_bf = _to_bf16(w_out_even, i, w_out_even.shape[2])
            shared = (g_pre, g_post, w_in_bf, conv_w_a[i], _band_bias(rel_bias_b[i]), w_out_bf, i, n_even)
            yp, hst, kv_p = _even_layer(yp, None, None, None, *shared, kv_p)
            conv_a_p.append(hst)
            ys, hst, kv_s = _even_layer(ys, cache_conv_a[i], cache_k_b[i], cache_v_b[i], *shared, kv_s)
            conv_a_s.append(hst)
        else:
            n_heads = a_log_c.shape[1]
            n_main = w_in_odd.shape[2] - 2 * n_heads
            w_main_bf = _to_bf16(w_in_odd, i, n_main)
            w_gate_bf = jnp.pad(w_in_odd[i][:, n_main:], ((0, 0), (0, LANES - 2 * n_heads))).astype(BF16)
            pad = (n_heads, LANES - 2 * n_heads)
            ab = jnp.stack([jnp.pad(a_log_c[i], pad), jnp.pad(dt_bias_c[i], pad)]).astype(F32)
            shared = (g_pre, g_post, w_main_bf, w_gate_bf, conv_w_c[i], ab, out_norm_c[i].reshape(1, -1),
                      _to_bf16(w_out_odd, i, w_out_odd.shape[2]), i, n_odd)
            yp, hst, s_p = _odd_layer(yp, None, None, *shared, s_p)
            conv_c_p.append(hst)
            ys, hst, s_s = _odd_layer(ys, state_conv_c[i], state_s_c[i], *shared, s_s)
            conv_c_s.append(hst)

    def heads_apart(kv):
        return kv.reshape(kv.shape[:-1] + (n_heads_b, kv.shape[-1] // n_heads_b))

    return (yp, ys,
            jnp.stack(conv_a_p), heads_apart(kv_p[0]), heads_apart(kv_p[1]), jnp.stack(conv_c_p), s_p,
            jnp.stack(conv_a_s), heads_apart(kv_s[0]), heads_apart(kv_s[1]), jnp.stack(conv_c_s), s_s)
```

```python
import functools
import math

import jax
import jax.numpy as jnp
from jax import lax
from jax.experimental import pallas as pl
from jax.experimental.pallas import tpu as pltpu

F32 = jnp.float32
BF16 = jnp.bfloat16

EPS = 1e-6
LOG2E = math.log2(math.e)
CHUNK = 64
BAND_PAST = 8 * CHUNK
BAND = BAND_PAST + CHUNK
REL_CLIP = 4 * CHUNK
HD_B = 64
HD_C = 128
LANES = 128
SUBLANES = 8
VMEM_LIMIT_BYTES = 56 * 1024 * 1024
STATE_VMEM_BUDGET_BYTES = 16 * 1024 * 1024
ROW_TILE = 512
CAST_COLS = 512


def _silu(x):
    return x * (1.0 / (1.0 + jnp.exp(-x)))


def _softplus(x):
    return jnp.maximum(x, 0.0) + jnp.log1p(jnp.exp(-jnp.abs(x)))


def _dot(a, b):
    return jnp.dot(a.astype(BF16), b.astype(BF16), preferred_element_type=F32)


def _load_rows(ref, nb):
    parts = [ref[s] for s in range(nb)]
    return parts[0] if nb == 1 else jnp.concatenate(parts, axis=0)


def _pre_norm(x_ref, gain_ref, nb):
    x = _load_rows(x_ref, nb)
    ms = jnp.mean(x * x, axis=-1, keepdims=True)
    return (x * lax.rsqrt(ms + EPS) * gain_ref[...]).astype(BF16)


def _project_out(y, wout_ref, gain_ref, x_ref, xo_ref, nb, tt):
    out = jnp.dot(y, wout_ref[...], preferred_element_type=F32)
    ms = jnp.mean(out * out, axis=-1, keepdims=True)
    x_new = _load_rows(x_ref, nb) + out * lax.rsqrt(ms + EPS) * gain_ref[...]
    for s in range(nb):
        xo_ref[s] = x_new[s * tt:(s + 1) * tt]


def _causal_dwconv(prev_rows, new_rows, taps):
    tt = new_rows.shape[0]
    full = jnp.concatenate([prev_rows, new_rows], axis=0)
    acc = full * taps[0:1]
    for j in range(1, taps.shape[0]):
        acc = full * taps[j:j + 1] + pltpu.roll(acc, 1, axis=0)
    return acc[SUBLANES:, :], full[tt:tt + SUBLANES, :]


def _streams_per_step(b, t, state_bytes_per_stream):
    tt = min(t, ROW_TILE)
    nb = max(1, min(ROW_TILE // tt, STATE_VMEM_BUDGET_BYTES // state_bytes_per_stream))
    while b % nb:
        nb -= 1
    return nb, tt


def _resident(shape):
    return pl.BlockSpec(shape, lambda bi, i: (0,) * len(shape), pipeline_mode=pl.Buffered(1))


def _params():
    return pltpu.CompilerParams(dimension_semantics=("parallel", "arbitrary"), vmem_limit_bytes=VMEM_LIMIT_BYTES)


def _even_layer_kernel(*refs, nb, tt, has_cache, n_alias):
    x_ref, gpre_ref, win_ref, cw_ref, bias_ref, wout_ref, gpost_ref = refs[:7]
    if has_cache:
        hist_ref, kc_ref, vc_ref = refs[7:10]
        refs = refs[10:]
    else:
        refs = refs[7:]
    refs = refs[n_alias:]
    xo_ref, utail_ref, knew_ref, vnew_ref, utails, kbuf, vbuf, qbuf, zbuf, ybuf = refs
    w = win_ref.shape[-1] // 8
    i = pl.program_id(1)

    @pl.when(i == 0)
    def _init():
        utails[...] = jnp.zeros(utails.shape, F32)
        if has_cache:
            for s in range(nb):
                utails[s, SUBLANES - 2:SUBLANES, :] = hist_ref[s]
            kbuf[:, 0:BAND_PAST, :] = kc_ref[...]
            vbuf[:, 0:BAND_PAST, :] = vc_ref[...]
        else:
            kbuf[:, 0:BAND_PAST, :] = jnp.zeros((nb, BAND_PAST, w), BF16)
            vbuf[:, 0:BAND_PAST, :] = jnp.zeros((nb, BAND_PAST, w), BF16)

    hb = _pre_norm(x_ref, gpre_ref, nb)

    def proj(j):
        return jnp.dot(hb, win_ref[:, j * w:(j + 1) * w], preferred_element_type=F32)

    u = proj(1) * proj(2)
    gate = proj(0) * _silu(proj(3))
    for s in range(nb):
        srows = slice(s * tt, (s + 1) * tt)
        conv, tail = _causal_dwconv(utails[s], u[srows], cw_ref[...])
        utail_ref[s] = tail
        utails[s] = tail
        ybuf[srows, 0:w] = (gate[srows] * conv).astype(ybuf.dtype)

    qbuf[...] = proj(4).astype(qbuf.dtype)
    k = proj(5)
    v = proj(6)
    for s in range(nb):
        srows = slice(s * tt, (s + 1) * tt)
        knew_ref[s] = k[srows]
        vnew_ref[s] = v[srows]
        kbuf[s, BAND_PAST:BAND_PAST + tt, :] = k[srows].astype(kbuf.dtype)
        vbuf[s, BAND_PAST:BAND_PAST + tt, :] = v[srows].astype(vbuf.dtype)
    zbuf[...] = _silu(proj(7)).astype(zbuf.dtype)

    chunks_per_stream = tt // CHUNK
    n_pairs = w // LANES
    lane = lax.broadcasted_iota(jnp.int32, (CHUNK, LANES), 1)
    first_head = lane < HD_B

    def attend(first_tile):
        def band_start(c):
            missing = max(BAND_PAST - CHUNK * c, 0) if first_tile else 0
            return missing // LANES * LANES, missing % LANES

        def scores(s, c, hp):
            cols = slice(hp * LANES, (hp + 1) * LANES)
            lo, masked = band_start(c)
            r0 = s * tt + c * CHUNK
            qp = qbuf[r0:r0 + CHUNK, cols]
            zero = jnp.zeros_like(qp)
            q2 = jnp.concatenate([jnp.where(first_head, qp, zero), jnp.where(first_head, zero, qp)], axis=0)
            kp = kbuf[s, c * CHUNK + lo:c * CHUNK + BAND, cols]
            sc = lax.dot_general(q2, kp, (((1,), (1,)), ((), ())), preferred_element_type=F32)
            sc = sc + bias_ref[hp, :, lo:BAND]
            if masked:
                key_idx = lax.broadcasted_iota(jnp.int32, sc.shape, 1)
                sc = jnp.where(key_idx >= masked, sc, -jnp.inf)
            return sc

        def weighted_values(s, c, hp, sc):
            cols = slice(hp * LANES, (hp + 1) * LANES)
            lo, _ = band_start(c)
            m = jnp.max(sc, axis=-1, keepdims=True)
            e = jnp.exp2(sc - m)
            l = jnp.sum(e, axis=-1, keepdims=True)
            vp = vbuf[s, c * CHUNK + lo:c * CHUNK + BAND, cols]
            pv = jnp.dot(e.astype(BF16), vp, preferred_element_type=F32) / l
            return jnp.where(first_head, pv[0:CHUNK], pv[CHUNK:2 * CHUNK])

        items = [(s, c, hp) for s in range(nb) for c in range(chunks_per_stream) for hp in range(n_pairs)]
        skew = 3
        pending, outs = {}, {}
        for step in range(len(items) + skew):
            if step < len(items):
                pending[step] = scores(*items[step])
            j = step - skew
            if j >= 0:
                s, c, hp = items[j]
                outs[hp] = weighted_values(s, c, hp, pending.pop(j))
                if hp == n_pairs - 1:
                    rows = slice(s * tt + c * CHUNK, s * tt + (c + 1) * CHUNK)
                    o = jnp.concatenate([outs[x] for x in range(n_pairs)], axis=1)
                    ybuf[rows, w:2 * w] = (o * zbuf[rows, :].astype(F32)).astype(ybuf.dtype)

    if has_cache:
        attend(False)
    else:
        pl.when(i == 0)(functools.partial(attend, True))
        pl.when(i > 0)(functools.partial(attend, False))
    for s in range(nb):
        kbuf[s, 0:BAND_PAST, :] = kbuf[s, tt:tt + BAND_PAST, :]
        vbuf[s, 0:BAND_PAST, :] = vbuf[s, tt:tt + BAND_PAST, :]

    _project_out(ybuf[...], wout_ref, gpost_ref, x_ref, xo_ref, nb, tt)


def _layer_slot_in(stacked, nb, slot):
    rest = stacked.shape[2:]
    return pl.BlockSpec((None, nb) + rest, lambda bi, i: (slot, bi) + (0,) * len(rest))


def _stacked_out(shape, block, slot, n_slots):
    spec = pl.BlockSpec((None,) + block, lambda bi, i: (slot, bi) + (0,) * (len(block) - 1))
    return spec, jax.ShapeDtypeStruct((n_slots,) + shape, F32)


def _even_layer(x, hist, kc, vc, g_pre, g_post, w_in_bf, conv_w, bias2, w_out_bf, slot, n_slots, kv_stacks):
    b, t, d = x.shape
    w = w_in_bf.shape[1] // 8
    n_heads = w // HD_B
    has_cache = kc is not None
    assert t % CHUNK == 0 and (t <= ROW_TILE or t % ROW_TILE == 0)
    band_bytes = 2 * (BAND_PAST + min(t, ROW_TILE)) * w * 2
    state_bytes = band_bytes + (2 * 2 * BAND_PAST * w * 2 if has_cache else 0)
    nb, tt = _streams_per_step(b, t, state_bytes)
    n_rows = nb * tt
    consts = [g_pre.reshape(1, d), w_in_bf, conv_w, bias2, w_out_bf, g_post.reshape(1, d)]
    in_specs = [pl.BlockSpec((nb, tt, d), lambda bi, i: (bi, i, 0))] + [_resident(c.shape) for c in consts]
    args = [x] + consts
    if has_cache:
        kc = kc.reshape(b, BAND_PAST, w).astype(BF16)
        vc = vc.reshape(b, BAND_PAST, w).astype(BF16)
        in_specs += [_layer_slot_in(hist, nb, slot),
                     pl.BlockSpec((nb, BAND_PAST, w), lambda bi, i: (bi, 0, 0)),
                     pl.BlockSpec((nb, BAND_PAST, w), lambda bi, i: (bi, 0, 0))]
        args += [hist, kc, vc]
    assert tt == min(BAND_PAST, t)
    kv_spec, kv_shape = _stacked_out((b, tt, w), (nb, tt, w), slot, n_slots)
    aliases = {}
    if kv_stacks is not None:
        aliases = {len(args): 2, len(args) + 1: 3}
        in_specs += [pl.BlockSpec(memory_space=pl.ANY)] * 2
        args += list(kv_stacks)
    x_new, utail, k_stack, v_stack = pl.pallas_call(
        functools.partial(_even_layer_kernel, nb=nb, tt=tt, has_cache=has_cache, n_alias=len(aliases)),
        grid=(b // nb, t // tt), in_specs=in_specs,
        out_specs=[pl.BlockSpec((nb, tt, d), lambda bi, i: (bi, i, 0)),
                   pl.BlockSpec((nb, SUBLANES, w), lambda bi, i: (bi, 0, 0)),
                   kv_spec, kv_spec],
        out_shape=[jax.ShapeDtypeStruct((b, t, d), F32),
                   jax.ShapeDtypeStruct((b, SUBLANES, w), F32),
                   kv_shape, kv_shape],
        input_output_aliases=aliases,
        scratch_shapes=[pltpu.VMEM((nb, SUBLANES, w), F32),
                        pltpu.VMEM((nb, BAND_PAST + tt, w), BF16), pltpu.VMEM((nb, BAND_PAST + tt, w), BF16),
                        pltpu.VMEM((n_rows, w), BF16), pltpu.VMEM((n_rows, w), BF16),
                        pltpu.VMEM((n_rows, 2 * w), BF16)],
        compiler_params=_params(), name="even_layer")(*args)
    del n_heads
    return x_new, utail[:, SUBLANES - 2:], (k_stack, v_stack)


def _cumsum_rows(x):
    n = x.shape[0]
    row = lax.broadcasted_iota(jnp.int32, x.shape, 0)
    shift = 1
    while shift < n:
        x = x + jnp.where(row >= shift, pltpu.roll(x, shift, axis=0), 0.0)
        shift *= 2
    return x


def _pair_masks():
    row = lax.broadcasted_iota(jnp.int32, (CHUNK, 2 * CHUNK), 0)
    lane = lax.broadcasted_iota(jnp.int32, (CHUNK, 2 * CHUNK), 1)
    first = lane < CHUNK
    col = jnp.where(first, lane, lane - CHUNK)
    return first, row, col


def _block_diag(pk, first):
    tiled = jnp.concatenate([pk, pk], axis=0).astype(BF16)
    keep = jnp.concatenate([first, jnp.logical_not(first)], axis=0)
    return jnp.where(keep, tiled, jnp.zeros_like(tiled))


def _unit_lower_inverses(mats, out, first, row, col):
    n = mats[0].shape[0]
    eye = jnp.where(row == col, 1.0, 0.0)
    xs = [-a for a in mats]
    ps = [eye + x for x in xs]
    ys = [jnp.dot(x.astype(BF16), _block_diag(x, first), preferred_element_type=F32) for x in xs]
    yield
    power = 2
    while 2 * power < n:
        sts = [jnp.dot(jnp.concatenate([p, y], axis=0).astype(BF16), _block_diag(y, first),
                       preferred_element_type=F32) for p, y in zip(ps, ys)]
        ps = [p + st[0:n] for p, st in zip(ps, sts)]
        ys = [st[n:2 * n] for st in sts]
        power *= 2
        yield
    out[:] = [p + jnp.dot(p.astype(BF16), _block_diag(y, first), preferred_element_type=F32)
              for p, y in zip(ps, ys)]
    yield


def _interleave(*generators):
    live = list(generators)
    while live:
        for g in list(live):
            try:
                next(g)
            except StopIteration:
                live.remove(g)


def _odd_layer_kernel(*refs, nb, tt, has_state, prep_chunks, n_alias):
    x_ref, gpre_ref, win_ref, wg_ref, cw_ref, ab_ref, on_ref, wout_ref, gpost_ref = refs[:9]
    if has_state:
        hist_ref, s0_ref = refs[9:11]
        refs = refs[11:]
    else:
        refs = refs[9:]
    refs = refs[n_alias:]
    (xo_ref, sout_ref, ctail_ref, ctails, qs, ks, vs, zs, gsc, bsc, sbuf, ybuf,
     wbuf, qtbuf, u0buf, qkbuf, kdtbuf, eglbuf) = refs
    w = win_ref.shape[-1] // 4
    n_heads = w // HD_C
    n_rows = nb * tt
    i = pl.program_id(1)

    @pl.when(i == 0)
    def _init():
        ctails[...] = jnp.zeros(ctails.shape, F32)
        if has_state:
            for s in range(nb):
                for g in range(3):
                    ctails[s, g, SUBLANES - 3:SUBLANES, :] = hist_ref[s, :, g * w:(g + 1) * w]
            sbuf[...] = s0_ref[...]
        else:
            sbuf[...] = jnp.zeros(sbuf.shape, F32)

    hb = _pre_norm(x_ref, gpre_ref, nb)

    gl = jnp.dot(hb, wg_ref[...], preferred_element_type=F32)
    bsc[...] = 1.0 / (1.0 + jnp.exp(-gl))
    gsc[...] = -jnp.exp(ab_ref[0:1, :]) * _softplus(gl + ab_ref[1:2, :])

    for g, dst in enumerate((qs, ks, vs)):
        raw = jnp.dot(hb, win_ref[:, g * w:(g + 1) * w], preferred_element_type=F32)
        inv_scale_sq = float(HD_C) if g == 0 else 1.0
        sum_mat = jnp.full((HD_C, HD_C), inv_scale_sq, BF16)
        for s in range(nb):
            srows = slice(s * tt, (s + 1) * tt)
            c, tail = _causal_dwconv(ctails[s, g], raw[srows], cw_ref[:, g * w:(g + 1) * w])
            c = _silu(c)
            ctail_ref[s, :, g * w:(g + 1) * w] = tail
            ctails[s, g] = tail
            if g == 2:
                dst[srows, :] = c
            else:
                for h in range(n_heads):
                    cols = slice(h * HD_C, (h + 1) * HD_C)
                    ch = c[:, cols]
                    ss = jnp.dot((ch * ch).astype(BF16), sum_mat, preferred_element_type=F32)
                    dst[srows, cols] = ch * lax.rsqrt(ss + EPS * inv_scale_sq)
    zs[...] = _silu(jnp.dot(hb, win_ref[:, 3 * w:4 * w], preferred_element_type=F32)).astype(zs.dtype)

    first, row, col = _pair_masks()
    incl = row >= col
    strict = row > col
    assert n_heads % 2 == 0 and 2 * CHUNK == HD_C
    pairs = range(n_heads // 2)

    heads = range(n_heads)
    hcols = [slice(h * HD_C, (h + 1) * HD_C) for h in heads]
    n_chunks = n_rows // CHUNK

    def chunk_rows(cidx):
        if isinstance(cidx, int):
            return slice(cidx * CHUNK, (cidx + 1) * CHUNK)
        return pl.ds(pl.multiple_of(cidx * CHUNK, CHUNK), CHUNK)

    def prepare_chunks(step):
        items = []
        duos = []
        for j in range(prep_chunks):
            cidx = step * prep_chunks + j
            rows = chunk_rows(cidx)
            gc = _cumsum_rows(gsc[rows, :])
            gct = gc.T
            bet = bsc[rows, :]
            for h in heads:
                gcol = jnp.broadcast_to(gc[:, n_heads + h:n_heads + h + 1], (CHUNK, HD_C))
                items.append(dict(cidx=cidx, rows=rows, h=h, gcol=gcol,
                                  b=jnp.broadcast_to(bet[:, h:h + 1], (CHUNK, HD_C))))
            for p in pairs:
                ia, ib = items[-n_heads + 2 * p], items[-n_heads + 2 * p + 1]
                g0 = n_heads + 2 * p
                grow = jnp.concatenate([gct[g0:g0 + 1, :], gct[g0 + 1:g0 + 2, :]], axis=1)
                gcol2 = jnp.where(first, ia["gcol"], ib["gcol"])
                duos.append(dict(cidx=cidx, rows=rows, p=p, a=ia, b=ib, beta=jnp.where(first, ia["b"], ib["b"]),
                                 decay=jnp.exp(jnp.where(incl, gcol2 - grow, -jnp.inf))))
        for it in items:
            rows, c = it["rows"], hcols[it["h"]]
            it["eg"] = jnp.exp(it["gcol"])
            it["q"], it["k"], it["v"] = qs[rows, c], ks[rows, c], vs[rows, c]
        for it in items:
            kb = it["k"].astype(BF16)
            it["qk_kk"] = lax.dot_general(jnp.concatenate([it["q"].astype(BF16), kb], axis=0), kb,
                                          (((1,), (1,)), ((), ())), preferred_element_type=F32)
        yield
        for du in duos:
            qk_kk = jnp.concatenate([du["a"]["qk_kk"], du["b"]["qk_kk"]], axis=1)
            du["qk"] = qk_kk[0:CHUNK] * du["decay"]
            du["mat"] = jnp.where(strict, du["beta"] * qk_kk[CHUNK:2 * CHUNK] * du["decay"], 0.0)
        tinvs = [None] * len(duos)
        yield from _unit_lower_inverses([du["mat"] for du in duos], tinvs, first, row, col)
        sols = []
        for tinv, du in zip(tinvs, duos):
            rhs = jnp.concatenate(
                [jnp.concatenate([it["v"] * it["b"], it["k"] * (it["b"] * it["eg"])], axis=1)
                 for it in (du["a"], du["b"])], axis=0)
            sols.append(jnp.dot(_block_diag(tinv, first), rhs.astype(BF16), preferred_element_type=F32))
        yield
        for sol, du in zip(sols, duos):
            rows = du["rows"]
            qkbuf[du["p"], rows, :] = du["qk"].astype(BF16)
            for half, it in enumerate((du["a"], du["b"])):
                gcol, c, h = it["gcol"], hcols[it["h"]], it["h"]
                g_last = gcol[CHUNK - 1:CHUNK, :]
                sol_h = sol[half * CHUNK:(half + 1) * CHUNK]
                u0buf[rows, c] = sol_h[:, 0:HD_C]
                wbuf[rows, c] = sol_h[:, HD_C:].astype(BF16)
                qtbuf[rows, c] = (it["q"] * it["eg"]).astype(BF16)
                kdtbuf[h, it["cidx"]] = (it["k"] * jnp.exp(g_last - gcol)).T.astype(BF16)
                eglbuf[it["cidx"], h:h + 1, :] = jnp.exp(g_last)

    chunks_per_stream = tt // CHUNK

    def recur_chunks(step):
        for j in range(prep_chunks):
            cidx = step * prep_chunks + j
            rows = chunk_rows(cidx)
            s = 0 if nb == 1 else cidx // chunks_per_stream
            s_old = [sbuf[s, h] for h in heads]
            ws_qs = [_dot(jnp.concatenate([wbuf[rows, hcols[h]], qtbuf[rows, hcols[h]]], axis=0), s_old[h])
                     for h in heads]
            yield
            u = [u0buf[rows, hcols[h]] - ws_qs[h][0:CHUNK] for h in heads]
            qku = [jnp.dot(_block_diag(qkbuf[p, rows, :], first),
                           jnp.concatenate([u[2 * p], u[2 * p + 1]], axis=0).astype(BF16),
                           preferred_element_type=F32) for p in pairs]
            o = [ws_qs[h][CHUNK:2 * CHUNK] + qku[h // 2][(h % 2) * CHUNK:(h % 2 + 1) * CHUNK] for h in heads]
            for h in heads:
                sbuf[s, h] = eglbuf[cidx, h:h + 1, :] * s_old[h] + _dot(kdtbuf[h, cidx], u[h])
            yield
            for h in heads:
                ms = jnp.mean(o[h] * o[h], axis=-1, keepdims=True)
                on = o[h] * lax.rsqrt(ms + EPS) * on_ref[...]
                ybuf[rows, hcols[h]] = (on * zs[rows, hcols[h]].astype(F32)).astype(ybuf.dtype)

    n_steps = n_chunks // prep_chunks
    _interleave(prepare_chunks(0))

    def pipelined(step, carry):
        _interleave(prepare_chunks(step), recur_chunks(step - 1))
        return carry

    lax.fori_loop(1, n_steps, pipelined, 0)
    _interleave(recur_chunks(n_steps - 1))
    sout_ref[...] = sbuf[...]

    _project_out(ybuf[...], wout_ref, gpost_ref, x_ref, xo_ref, nb, tt)


def _odd_layer(x, hist, s0, g_pre, g_post, w_in_bf, w_gate_bf, conv_w, ab, onorm, w_out_bf, slot, n_slots, s_stack):
    b, t, d = x.shape
    w = w_in_bf.shape[1] // 4
    n_heads = w // HD_C
    has_state = s0 is not None
    assert t % CHUNK == 0 and (t <= ROW_TILE or t % ROW_TILE == 0)
    state_bytes = n_heads * HD_C * HD_C * 4 * (5 if has_state else 3)
    nb, tt = _streams_per_step(b, t, state_bytes)
    n_rows = nb * tt
    n_chunks = n_rows // CHUNK
    prep_chunks = max(c for c in (1, 2, 4) if n_chunks % c == 0)
    consts = [g_pre.reshape(1, d), w_in_bf, w_gate_bf, conv_w, ab, onorm, w_out_bf, g_post.reshape(1, d)]
    in_specs = [pl.BlockSpec((nb, tt, d), lambda bi, i: (bi, i, 0))] + [_resident(c.shape) for c in consts]
    args = [x] + consts
    if has_state:
        in_specs += [_layer_slot_in(hist, nb, slot), _layer_slot_in(s0, nb, slot)]
        args += [hist, s0]
    s_spec, s_shape = _stacked_out((b, n_heads, HD_C, HD_C), (nb, n_heads, HD_C, HD_C), slot, n_slots)
    aliases = {}
    if s_stack is not None:
        aliases = {len(args): 1}
        in_specs.append(pl.BlockSpec(memory_space=pl.ANY))
        args.append(s_stack)
    x_new, s_new, ctail = pl.pallas_call(
        functools.partial(_odd_layer_kernel, nb=nb, tt=tt, has_state=has_state, prep_chunks=prep_chunks,
                          n_alias=len(aliases)),
        grid=(b // nb, t // tt), in_specs=in_specs,
        out_specs=[pl.BlockSpec((nb, tt, d), lambda bi, i: (bi, i, 0)),
                   s_spec,
                   pl.BlockSpec((nb, SUBLANES, 3 * w), lambda bi, i: (bi, 0, 0))],
        out_shape=[jax.ShapeDtypeStruct((b, t, d), F32),
                   s_shape,
                   jax.ShapeDtypeStruct((b, SUBLANES, 3 * w), F32)],
        input_output_aliases=aliases,
        scratch_shapes=[pltpu.VMEM((nb, 3, SUBLANES, w), F32),
                        pltpu.VMEM((n_rows, w), F32), pltpu.VMEM((n_rows, w), F32), pltpu.VMEM((n_rows, w), F32),
                        pltpu.VMEM((n_rows, w), BF16),
                        pltpu.VMEM((n_rows, LANES), F32), pltpu.VMEM((n_rows, LANES), F32),
                        pltpu.VMEM((nb, n_heads, HD_C, HD_C), F32),
                        pltpu.VMEM((n_rows, w), BF16),
                        pltpu.VMEM((n_rows, w), BF16), pltpu.VMEM((n_rows, w), BF16), pltpu.VMEM((n_rows, w), F32),
                        pltpu.VMEM((n_heads // 2, n_rows, 2 * CHUNK), BF16),
                        pltpu.VMEM((n_heads, n_chunks, HD_C, CHUNK), BF16),
                        pltpu.VMEM((n_chunks, n_heads, HD_C), F32)],
        compiler_params=_params(), name="odd_layer")(*args)
    return x_new, ctail[:, SUBLANES - 3:], s_new


def _cast_kernel(w_ref, *rest):
    if len(rest) == 2:
        scale_ref, o_ref = rest
        o_ref[...] = (w_ref[...] * scale_ref[...]).astype(o_ref.dtype)
    else:
        (o_ref,) = rest
        o_ref[...] = w_ref[...].astype(o_ref.dtype)


def _to_bf16(w_all, layer, n_cols, col_scale=None):
    k = w_all.shape[1]
    assert n_cols % CAST_COLS == 0
    in_specs = [pl.BlockSpec((None, k, CAST_COLS), lambda j: (layer, 0, j))]
    args = [w_all]
    if col_scale is not None:
        in_specs.append(pl.BlockSpec((1, CAST_COLS), lambda j: (0, j)))
        args.append(col_scale.reshape(1, n_cols))
    return pl.pallas_call(
        _cast_kernel, grid=(n_cols // CAST_COLS,), in_specs=in_specs,
        out_specs=pl.BlockSpec((k, CAST_COLS), lambda j: (0, j)),
        out_shape=jax.ShapeDtypeStruct((k, n_cols), BF16),
        compiler_params=pltpu.CompilerParams(dimension_semantics=("parallel",)), name="to_bf16")(*args)


def _gate_cast_kernel(w_ref, o_ref, *, n_gate):
    lane = lax.broadcasted_iota(jnp.int32, w_ref.shape, 1)
    o_ref[...] = jnp.where(lane < n_gate, w_ref[...], 0.0).astype(o_ref.dtype)


def _gate_to_bf16(w_all, layer, first_col, n_gate):
    k = w_all.shape[1]
    assert first_col % LANES == 0 and n_gate <= LANES and first_col + n_gate == w_all.shape[2]
    return pl.pallas_call(
        functools.partial(_gate_cast_kernel, n_gate=n_gate), grid=(1,),
        in_specs=[pl.BlockSpec((None, k, LANES), lambda j: (layer, 0, first_col // LANES))],
        out_specs=pl.BlockSpec((k, LANES), lambda j: (0, 0)),
        out_shape=jax.ShapeDtypeStruct((k, LANES), BF16), name="gate_to_bf16")(w_all)


def _band_bias(table):
    n_heads = table.shape[0]
    assert CHUNK - 1 <= REL_CLIP < BAND - 1
    low = table[:, REL_CLIP - (CHUNK - 1):]
    high = jnp.broadcast_to(table[:, 2 * REL_CLIP:], (n_heads, BAND - 1 - REL_CLIP))
    by_dist = jnp.concatenate([low, high], axis=1)[:, ::-1]
    rows = [by_dist[:, CHUNK - 1 - qi:CHUNK - 1 - qi + BAND] for qi in range(CHUNK)]
    bias = jnp.stack(rows, axis=1).astype(F32) * LOG2E
    return bias.reshape(n_heads // 2, 2 * CHUNK, BAND)


def kernel(x_prompt, x_sample, cache_conv_a, cache_k_b, cache_v_b, state_conv_c, state_s_c, norm_pre, norm_post, w_in_even, conv_w_a, rel_bias_b, w_out_even, w_in_odd, conv_w_c, a_log_c, dt_bias_c, out_norm_c, w_out_odd):
    depth = norm_pre.shape[0]
    n_even, n_odd = w_in_even.shape[0], w_in_odd.shape[0]
    n_heads_b = cache_k_b.shape[-2]
    yp, ys = x_prompt, x_sample
    conv_a_p, conv_a_s, conv_c_p, conv_c_s = [], [], [], []
    kv_p = kv_s = s_p = s_s = None
    for layer in range(depth):
        i = layer // 2
        g_pre, g_post = norm_pre[layer], norm_post[layer]
        if layer % 2 == 0:
            n_in = w_in_even.shape[2]
            w_blk = n_in // 8
            col_scale = jnp.ones((n_in,), F32).at[4 * w_blk:5 * w_blk].set(HD_B ** -0.5 * LOG2E)
            w_in_bf = _to_bf16(w_in_even, i, n_in, col_scale)
            w_out_bf = _to_bf16(w_out_even, i, w_out_even.shape[2])
            shared = (g_pre, g_post, w_in_bf, conv_w_a[i], _band_bias(rel_bias_b[i]), w_out_bf, i, n_even)
            yp, hst, kv_p = _even_layer(yp, None, None, None, *shared, kv_p)
            conv_a_p.append(hst)
            ys, hst, kv_s = _even_layer(ys, cache_conv_a, cache_k_b[i], cache_v_b[i], *shared, kv_s)
            conv_a_s.append(hst)
        else:
            n_heads = a_log_c.shape[1]
            n_main = w_in_odd.shape[2] - 2 * n_heads
            w_main_bf = _to_bf16(w_in_odd, i, n_main)
            w_gate_bf = _gate_to_bf16(w_in_odd, i, n_main, 2 * n_heads)
            pad = (n_heads, LANES - 2 * n_heads)
            ab = jnp.stack([jnp.pad(a_log_c[i], pad), jnp.pad(dt_bias_c[i], pad)]).astype(F32)
            shared = (g_pre, g_post, w_main_bf, w_gate_bf, conv_w_c[i], ab, out_norm_c[i].reshape(1, -1),
                      _to_bf16(w_out_odd, i, w_out_odd.shape[2]), i, n_odd)
            yp, hst, s_p = _odd_layer(yp, None, None, *shared, s_p)
            conv_c_p.append(hst)
            ys, hst, s_s = _odd_layer(ys, state_conv_c, state_s_c, *shared, s_s)
            conv_c_s.append(hst)

    def heads_apart(kv):
        return kv.reshape(kv.shape[:-1] + (n_heads_b, kv.shape[-1] // n_heads_b))

    return (yp, ys,
            jnp.stack(conv_a_p), heads_apart(kv_p[0]), heads_apart(kv_p[1]), jnp.stack(conv_c_p), s_p,
            jnp.stack(conv_a_s), heads_apart(kv_s[0]), heads_apart(kv_s[1]), jnp.stack(conv_c_s), s_s)
```

```python
import functools
import math

import jax
import jax.numpy as jnp
from jax import lax
from jax.experimental import pallas as pl
from jax.experimental.pallas import tpu as pltpu

F32 = jnp.float32
BF16 = jnp.bfloat16

EPS = 1e-6
LOG2E = math.log2(math.e)
CHUNK = 64
BAND_PAST = 8 * CHUNK
BAND = BAND_PAST + CHUNK
REL_CLIP = 4 * CHUNK
HD_B = 64
HD_C = 128
LANES = 128
SUBLANES = 8
VMEM_LIMIT_BYTES = 56 * 1024 * 1024
STATE_VMEM_BUDGET_BYTES = 16 * 1024 * 1024
ROW_TILE = 512
CAST_COLS = 512
PROJ_BLOCKS = 4


def _silu(x):
    return x * (1.0 / (1.0 + jnp.exp(-x)))


def _softplus(x):
    return jnp.maximum(x, 0.0) + jnp.log1p(jnp.exp(-jnp.abs(x)))


def _dot(a, b):
    return jnp.dot(a.astype(BF16), b.astype(BF16), preferred_element_type=F32)


def _load_rows(ref, nb):
    parts = [ref[s] for s in range(nb)]
    return parts[0] if nb == 1 else jnp.concatenate(parts, axis=0)


def _pre_norm(x_ref, gain_ref, nb):
    x = _load_rows(x_ref, nb)
    ms = jnp.mean(x * x, axis=-1, keepdims=True)
    return (x * lax.rsqrt(ms + EPS) * gain_ref[...]).astype(BF16)


def _project_out(y, wout_ref, gain_ref, x_ref, xo_ref, nb, tt):
    out = jnp.dot(y, wout_ref[...], preferred_element_type=F32)
    ms = jnp.mean(out * out, axis=-1, keepdims=True)
    x_new = _load_rows(x_ref, nb) + out * lax.rsqrt(ms + EPS) * gain_ref[...]
    for s in range(nb):
        xo_ref[s] = x_new[s * tt:(s + 1) * tt]


def _causal_dwconv(prev_rows, new_rows, taps):
    tt = new_rows.shape[0]
    width = taps.shape[0]
    assert width <= SUBLANES
    full = jnp.concatenate([prev_rows, new_rows], axis=0)
    sums = [None, None]
    shifted = full
    for delay in range(width):
        if delay and delay % 2 == 0:
            shifted = pltpu.roll(shifted, 2, axis=0)
        term = shifted * taps[width - 1 - delay:width - delay]
        sums[delay % 2] = term if sums[delay % 2] is None else sums[delay % 2] + term
    acc = sums[0] if sums[1] is None else sums[0] + pltpu.roll(sums[1], 1, axis=0)
    return acc[SUBLANES:, :], full[tt:tt + SUBLANES, :]


def _streams_per_step(b, t, state_bytes_per_stream):
    tt = min(t, ROW_TILE)
    nb = max(1, min(ROW_TILE // tt, STATE_VMEM_BUDGET_BYTES // state_bytes_per_stream))
    while b % nb:
        nb -= 1
    return nb, tt


def _resident(shape):
    return pl.BlockSpec(shape, lambda bi, i: (0,) * len(shape), pipeline_mode=pl.Buffered(1))


def _params():
    return pltpu.CompilerParams(dimension_semantics=("parallel", "arbitrary"), vmem_limit_bytes=VMEM_LIMIT_BYTES)


def _even_layer_kernel(*refs, nb, tt, has_cache, n_alias):
    x_ref, gpre_ref, win_ref, cw_ref, bias_ref, wout_ref, gpost_ref = refs[:7]
    if has_cache:
        hist_ref, kc_ref, vc_ref = refs[7:10]
        refs = refs[10:]
    else:
        refs = refs[7:]
    refs = refs[n_alias:]
    xo_ref, utail_ref, knew_ref, vnew_ref, utails, kbuf, vbuf, qbuf, zbuf, ybuf = refs
    w = win_ref.shape[-1] // 8
    i = pl.program_id(1)

    @pl.when(i == 0)
    def _init():
        utails[...] = jnp.zeros(utails.shape, F32)
        if has_cache:
            for s in range(nb):
                utails[s, SUBLANES - 2:SUBLANES, :] = hist_ref[s]
            kbuf[:, 0:BAND_PAST, :] = kc_ref[...]
            vbuf[:, 0:BAND_PAST, :] = vc_ref[...]
        else:
            kbuf[:, 0:BAND_PAST, :] = jnp.zeros((nb, BAND_PAST, w), BF16)
            vbuf[:, 0:BAND_PAST, :] = jnp.zeros((nb, BAND_PAST, w), BF16)

    hb = _pre_norm(x_ref, gpre_ref, nb)

    def proj(j):
        return jnp.dot(hb, win_ref[:, j * w:(j + 1) * w], preferred_element_type=F32)

    u = proj(1) * proj(2)
    gate = proj(0) * _silu(proj(3))
    for s in range(nb):
        srows = slice(s * tt, (s + 1) * tt)
        conv, tail = _causal_dwconv(utails[s], u[srows], cw_ref[...])
        utail_ref[s] = tail
        utails[s] = tail
        ybuf[srows, 0:w] = (gate[srows] * conv).astype(ybuf.dtype)

    qbuf[...] = proj(4).astype(qbuf.dtype)
    k = proj(5)
    v = proj(6)
    for s in range(nb):
        srows = slice(s * tt, (s + 1) * tt)
        knew_ref[s] = k[srows]
        vnew_ref[s] = v[srows]
        kbuf[s, BAND_PAST:BAND_PAST + tt, :] = k[srows].astype(kbuf.dtype)
        vbuf[s, BAND_PAST:BAND_PAST + tt, :] = v[srows].astype(vbuf.dtype)
    zbuf[...] = _silu(proj(7)).astype(zbuf.dtype)

    chunks_per_stream = tt // CHUNK
    n_pairs = w // LANES
    lane = lax.broadcasted_iota(jnp.int32, (CHUNK, LANES), 1)
    first_head = lane < HD_B

    def attend(first_tile):
        def band_start(c):
            missing = max(BAND_PAST - CHUNK * c, 0) if first_tile else 0
            return missing // LANES * LANES, missing % LANES

        def scores(s, c, hp):
            cols = slice(hp * LANES, (hp + 1) * LANES)
            lo, masked = band_start(c)
            r0 = s * tt + c * CHUNK
            qp = qbuf[r0:r0 + CHUNK, cols]
            zero = jnp.zeros_like(qp)
            q2 = jnp.concatenate([jnp.where(first_head, qp, zero), jnp.where(first_head, zero, qp)], axis=0)
            kp = kbuf[s, c * CHUNK + lo:c * CHUNK + BAND, cols]
            sc = lax.dot_general(q2, kp, (((1,), (1,)), ((), ())), preferred_element_type=F32)
            sc = sc + bias_ref[hp, :, lo:BAND]
            if masked:
                key_idx = lax.broadcasted_iota(jnp.int32, sc.shape, 1)
                sc = jnp.where(key_idx >= masked, sc, -jnp.inf)
            return sc

        def weighted_values(s, c, hp, sc):
            cols = slice(hp * LANES, (hp + 1) * LANES)
            lo, _ = band_start(c)
            m = jnp.max(sc, axis=-1, keepdims=True)
            e = jnp.exp2(sc - m)
            l = jnp.sum(e, axis=-1, keepdims=True)
            vp = vbuf[s, c * CHUNK + lo:c * CHUNK + BAND, cols]
            pv = jnp.dot(e.astype(BF16), vp, preferred_element_type=F32) / l
            return jnp.where(first_head, pv[0:CHUNK], pv[CHUNK:2 * CHUNK])

        items = [(s, c, hp) for s in range(nb) for c in range(chunks_per_stream) for hp in range(n_pairs)]
        skew = 3
        pending, outs = {}, {}
        for step in range(len(items) + skew):
            if step < len(items):
                pending[step] = scores(*items[step])
            j = step - skew
            if j >= 0:
                s, c, hp = items[j]
                outs[hp] = weighted_values(s, c, hp, pending.pop(j))
                if hp == n_pairs - 1:
                    rows = slice(s * tt + c * CHUNK, s * tt + (c + 1) * CHUNK)
                    o = jnp.concatenate([outs[x] for x in range(n_pairs)], axis=1)
                    ybuf[rows, w:2 * w] = (o * zbuf[rows, :].astype(F32)).astype(ybuf.dtype)

    if has_cache:
        attend(False)
    else:
        pl.when(i == 0)(functools.partial(attend, True))
        pl.when(i > 0)(functools.partial(attend, False))
    for s in range(nb):
        kbuf[s, 0:BAND_PAST, :] = kbuf[s, tt:tt + BAND_PAST, :]
        vbuf[s, 0:BAND_PAST, :] = vbuf[s, tt:tt + BAND_PAST, :]

    _project_out(ybuf[...], wout_ref, gpost_ref, x_ref, xo_ref, nb, tt)


def _layer_slot_in(stacked, nb, slot):
    rest = stacked.shape[2:]
    return pl.BlockSpec((None, nb) + rest, lambda bi, i: (slot, bi) + (0,) * len(rest))


def _stacked_out(shape, block, slot, n_slots):
    spec = pl.BlockSpec((None,) + block, lambda bi, i: (slot, bi) + (0,) * (len(block) - 1))
    return spec, jax.ShapeDtypeStruct((n_slots,) + shape, F32)


def _even_layer(x, hist, kc, vc, g_pre, g_post, w_in_bf, conv_w, bias2, w_out_bf, slot, n_slots, kv_stacks):
    b, t, d = x.shape
    w = w_in_bf.shape[1] // 8
    n_heads = w // HD_B
    has_cache = kc is not None
    assert t % CHUNK == 0 and (t <= ROW_TILE or t % ROW_TILE == 0)
    band_bytes = 2 * (BAND_PAST + min(t, ROW_TILE)) * w * 2
    state_bytes = band_bytes + (2 * 2 * BAND_PAST * w * 2 if has_cache else 0)
    nb, tt = _streams_per_step(b, t, state_bytes)
    n_rows = nb * tt
    consts = [g_pre.reshape(1, d), w_in_bf, conv_w, bias2, w_out_bf, g_post.reshape(1, d)]
    in_specs = [pl.BlockSpec((nb, tt, d), lambda bi, i: (bi, i, 0))] + [_resident(c.shape) for c in consts]
    args = [x] + consts
    if has_cache:
        kc = kc.reshape(b, BAND_PAST, w).astype(BF16)
        vc = vc.reshape(b, BAND_PAST, w).astype(BF16)
        in_specs += [_layer_slot_in(hist, nb, slot),
                     pl.BlockSpec((nb, BAND_PAST, w), lambda bi, i: (bi, 0, 0)),
                     pl.BlockSpec((nb, BAND_PAST, w), lambda bi, i: (bi, 0, 0))]
        args += [hist, kc, vc]
    assert tt == min(BAND_PAST, t)
    kv_spec, kv_shape = _stacked_out((b, tt, w), (nb, tt, w), slot, n_slots)
    aliases = {}
    if kv_stacks is not None:
        aliases = {len(args): 2, len(args) + 1: 3}
        in_specs += [pl.BlockSpec(memory_space=pl.ANY)] * 2
        args += list(kv_stacks)
    x_new, utail, k_stack, v_stack = pl.pallas_call(
        functools.partial(_even_layer_kernel, nb=nb, tt=tt, has_cache=has_cache, n_alias=len(aliases)),
        grid=(b // nb, t // tt), in_specs=in_specs,
        out_specs=[pl.BlockSpec((nb, tt, d), lambda bi, i: (bi, i, 0)),
                   pl.BlockSpec((nb, SUBLANES, w), lambda bi, i: (bi, 0, 0)),
                   kv_spec, kv_spec],
        out_shape=[jax.ShapeDtypeStruct((b, t, d), F32),
                   jax.ShapeDtypeStruct((b, SUBLANES, w), F32),
                   kv_shape, kv_shape],
        input_output_aliases=aliases,
        scratch_shapes=[pltpu.VMEM((nb, SUBLANES, w), F32),
                        pltpu.VMEM((nb, BAND_PAST + tt, w), BF16), pltpu.VMEM((nb, BAND_PAST + tt, w), BF16),
                        pltpu.VMEM((n_rows, w), BF16), pltpu.VMEM((n_rows, w), BF16),
                        pltpu.VMEM((n_rows, 2 * w), BF16)],
        compiler_params=_params(), name="even_layer")(*args)
    del n_heads
    return x_new, utail[:, SUBLANES - 2:], (k_stack, v_stack)


def _cumsum_rows(x):
    n = x.shape[0]
    row = lax.broadcasted_iota(jnp.int32, x.shape, 0)
    shift = 1
    while shift < n:
        x = x + jnp.where(row >= shift, pltpu.roll(x, shift, axis=0), 0.0)
        shift *= 2
    return x


def _pair_masks():
    row = lax.broadcasted_iota(jnp.int32, (CHUNK, 2 * CHUNK), 0)
    lane = lax.broadcasted_iota(jnp.int32, (CHUNK, 2 * CHUNK), 1)
    first = lane < CHUNK
    col = jnp.where(first, lane, lane - CHUNK)
    return first, row, col


def _block_diag(pk, first):
    tiled = jnp.concatenate([pk, pk], axis=0).astype(BF16)
    keep = jnp.concatenate([first, jnp.logical_not(first)], axis=0)
    return jnp.where(keep, tiled, jnp.zeros_like(tiled))


def _unit_lower_inverses(mats, out, first, row, col):
    n = mats[0].shape[0]
    eye = jnp.where(row == col, 1.0, 0.0)
    xs = [-a for a in mats]
    ps = [eye + x for x in xs]
    ys = [jnp.dot(x.astype(BF16), _block_diag(x, first), preferred_element_type=F32) for x in xs]
    yield
    power = 2
    while 2 * power < n:
        sts = [jnp.dot(jnp.concatenate([p, y], axis=0).astype(BF16), _block_diag(y, first),
                       preferred_element_type=F32) for p, y in zip(ps, ys)]
        ps = [p + st[0:n] for p, st in zip(ps, sts)]
        ys = [st[n:2 * n] for st in sts]
        power *= 2
        yield
    out[:] = [p + jnp.dot(p.astype(BF16), _block_diag(y, first), preferred_element_type=F32)
              for p, y in zip(ps, ys)]
    yield


def _interleave(*generators):
    live = list(generators)
    while live:
        for g in list(live):
            try:
                next(g)
            except StopIteration:
                live.remove(g)


def _odd_layer_kernel(*refs, nb, tt, has_state, prep_chunks, n_alias):
    x_ref, gpre_ref, win_ref, wg_ref, cw_ref, ab_ref, on_ref, wout_ref, gpost_ref = refs[:9]
    if has_state:
        hist_ref, s0_ref = refs[9:11]
        refs = refs[11:]
    else:
        refs = refs[9:]
    refs = refs[n_alias:]
    (xo_ref, sout_ref, ctail_ref, ctails, qs, ks, vs, zs, gsc, bsc, sbuf, ybuf,
     wbuf, qtbuf, u0buf, qkbuf, kdtbuf, eglbuf) = refs
    w = win_ref.shape[-1] // 4
    n_heads = w // HD_C
    n_rows = nb * tt
    i = pl.program_id(1)

    @pl.when(i == 0)
    def _init():
        ctails[...] = jnp.zeros(ctails.shape, F32)
        if has_state:
            for s in range(nb):
                for g in range(3):
                    ctails[s, g, SUBLANES - 3:SUBLANES, :] = hist_ref[s, :, g * w:(g + 1) * w]
            sbuf[...] = s0_ref[...]
        else:
            sbuf[...] = jnp.zeros(sbuf.shape, F32)

    hb = _pre_norm(x_ref, gpre_ref, nb)

    gl = jnp.dot(hb, wg_ref[...], preferred_element_type=F32)
    bsc[...] = 1.0 / (1.0 + jnp.exp(-gl))
    gsc[...] = -jnp.exp(ab_ref[0:1, :]) * _softplus(gl + ab_ref[1:2, :])

    def project_group(g, dst, n_blocks):
        inv_scale_sq = float(HD_C) if g == 0 else 1.0
        sum_mat = jnp.full((HD_C, HD_C), inv_scale_sq, BF16)
        bw = w // n_blocks
        for blk in range(n_blocks):
            gcols = slice(g * w + blk * bw, g * w + (blk + 1) * bw)
            raw = jnp.dot(hb, win_ref[:, gcols], preferred_element_type=F32)
            for s in range(nb):
                srows = slice(s * tt, (s + 1) * tt)
                c, tail = _causal_dwconv(ctails[s, g, :, blk * bw:(blk + 1) * bw], raw[srows], cw_ref[:, gcols])
                c = _silu(c)
                ctail_ref[s, :, gcols] = tail
                ctails[s, g, :, blk * bw:(blk + 1) * bw] = tail
                if g == 2:
                    dst[srows, blk * bw:(blk + 1) * bw] = c
                else:
                    for h in range(bw // HD_C):
                        ch = c[:, h * HD_C:(h + 1) * HD_C]
                        ss = jnp.dot((ch * ch).astype(BF16), sum_mat, preferred_element_type=F32)
                        dst[srows, blk * bw + h * HD_C:blk * bw + (h + 1) * HD_C] = ch * lax.rsqrt(
                            ss + EPS * inv_scale_sq)
            yield

    def project_gate(n_blocks):
        bw = w // n_blocks
        for blk in range(n_blocks):
            z = jnp.dot(hb, win_ref[:, 3 * w + blk * bw:3 * w + (blk + 1) * bw], preferred_element_type=F32)
            zs[:, blk * bw:(blk + 1) * bw] = _silu(z).astype(zs.dtype)
            yield

    _interleave(project_group(0, qs, 1))
    _interleave(project_group(1, ks, 1))

    first, row, col = _pair_masks()
    incl = row >= col
    strict = row > col
    assert n_heads % 2 == 0 and 2 * CHUNK == HD_C
    pairs = range(n_heads // 2)

    heads = range(n_heads)
    hcols = [slice(h * HD_C, (h + 1) * HD_C) for h in heads]
    n_chunks = n_rows // CHUNK

    def chunk_rows(cidx):
        if isinstance(cidx, int):
            return slice(cidx * CHUNK, (cidx + 1) * CHUNK)
        return pl.ds(pl.multiple_of(cidx * CHUNK, CHUNK), CHUNK)

    def prepare_chunks(step):
        items = []
        duos = []
        for j in range(prep_chunks):
            cidx = step * prep_chunks + j
            rows = chunk_rows(cidx)
            gc = _cumsum_rows(gsc[rows, :])
            gct = gc.T
            bet = bsc[rows, :]
            for h in heads:
                gcol = jnp.broadcast_to(gc[:, n_heads + h:n_heads + h + 1], (CHUNK, HD_C))
                items.append(dict(cidx=cidx, rows=rows, h=h, gcol=gcol,
                                  b=jnp.broadcast_to(bet[:, h:h + 1], (CHUNK, HD_C))))
            for p in pairs:
                ia, ib = items[-n_heads + 2 * p], items[-n_heads + 2 * p + 1]
                g0 = n_heads + 2 * p
                grow = jnp.concatenate([gct[g0:g0 + 1, :], gct[g0 + 1:g0 + 2, :]], axis=1)
                gcol2 = jnp.where(first, ia["gcol"], ib["gcol"])
                duos.append(dict(cidx=cidx, rows=rows, p=p, a=ia, b=ib, beta=jnp.where(first, ia["b"], ib["b"]),
                                 decay=jnp.exp(jnp.where(incl, gcol2 - grow, -jnp.inf))))
        for it in items:
            rows, c = it["rows"], hcols[it["h"]]
            it["eg"] = jnp.exp(it["gcol"])
            it["q"], it["k"] = qs[rows, c], ks[rows, c]
        for it in items:
            kb = it["k"].astype(BF16)
            it["qk_kk"] = lax.dot_general(jnp.concatenate([it["q"].astype(BF16), kb], axis=0), kb,
                                          (((1,), (1,)), ((), ())), preferred_element_type=F32)
        yield
        for du in duos:
            qk_kk = jnp.concatenate([du["a"]["qk_kk"], du["b"]["qk_kk"]], axis=1)
            du["qk"] = qk_kk[0:CHUNK] * du["decay"]
            du["mat"] = jnp.where(strict, du["beta"] * qk_kk[CHUNK:2 * CHUNK] * du["decay"], 0.0)
        tinvs = [None] * len(duos)
        yield from _unit_lower_inverses([du["mat"] for du in duos], tinvs, first, row, col)
        sols = []
        for tinv, du in zip(tinvs, duos):
            rhs = jnp.concatenate(
                [jnp.concatenate([vs[it["rows"], hcols[it["h"]]] * it["b"], it["k"] * (it["b"] * it["eg"])], axis=1)
                 for it in (du["a"], du["b"])], axis=0)
            sols.append(jnp.dot(_block_diag(tinv, first), rhs.astype(BF16), preferred_element_type=F32))
        yield
        for sol, du in zip(sols, duos):
            rows = du["rows"]
            qkbuf[du["p"], rows, :] = du["qk"].astype(BF16)
            for half, it in enumerate((du["a"], du["b"])):
                gcol, c, h = it["gcol"], hcols[it["h"]], it["h"]
                g_last = gcol[CHUNK - 1:CHUNK, :]
                sol_h = sol[half * CHUNK:(half + 1) * CHUNK]
                u0buf[rows, c] = sol_h[:, 0:HD_C]
                wbuf[rows, c] = sol_h[:, HD_C:].astype(BF16)
                qtbuf[rows, c] = (it["q"] * it["eg"]).astype(BF16)
                kdtbuf[h, it["cidx"]] = (it["k"] * jnp.exp(g_last - gcol)).T.astype(BF16)
                eglbuf[it["cidx"], h:h + 1, :] = jnp.exp(g_last)

    chunks_per_stream = tt // CHUNK

    def recur_chunks(step):
        for j in range(prep_chunks):
            cidx = step * prep_chunks + j
            rows = chunk_rows(cidx)
            s = 0 if nb == 1 else cidx // chunks_per_stream
            s_old = [sbuf[s, h] for h in heads]
            ws_qs = [_dot(jnp.concatenate([wbuf[rows, hcols[h]], qtbuf[rows, hcols[h]]], axis=0), s_old[h])
                     for h in heads]
            yield
            u = [u0buf[rows, hcols[h]] - ws_qs[h][0:CHUNK] for h in heads]
            qku = [jnp.dot(_block_diag(qkbuf[p, rows, :], first),
                           jnp.concatenate([u[2 * p], u[2 * p + 1]], axis=0).astype(BF16),
                           preferred_element_type=F32) for p in pairs]
            o = [ws_qs[h][CHUNK:2 * CHUNK] + qku[h // 2][(h % 2) * CHUNK:(h % 2 + 1) * CHUNK] for h in heads]
            for h in heads:
                sbuf[s, h] = eglbuf[cidx, h:h + 1, :] * s_old[h] + _dot(kdtbuf[h, cidx], u[h])
            yield
            for h in heads:
                ms = jnp.mean(o[h] * o[h], axis=-1, keepdims=True)
                on = o[h] * lax.rsqrt(ms + EPS) * on_ref[...]
                ybuf[rows, hcols[h]] = (on * zs[rows, hcols[h]].astype(F32)).astype(ybuf.dtype)

    n_steps = n_chunks // prep_chunks
    _interleave(prepare_chunks(0), project_group(2, vs, PROJ_BLOCKS), project_gate(PROJ_BLOCKS))

    def pipelined(step, carry):
        _interleave(prepare_chunks(step), recur_chunks(step - 1))
        return carry

    lax.fori_loop(1, n_steps, pipelined, 0)
    _interleave(recur_chunks(n_steps - 1))
    sout_ref[...] = sbuf[...]

    _project_out(ybuf[...], wout_ref, gpost_ref, x_ref, xo_ref, nb, tt)


def _odd_layer(x, hist, s0, g_pre, g_post, w_in_bf, w_gate_bf, conv_w, ab, onorm, w_out_bf, slot, n_slots, s_stack):
    b, t, d = x.shape
    w = w_in_bf.shape[1] // 4
    n_heads = w // HD_C
    has_state = s0 is not None
    assert t % CHUNK == 0 and (t <= ROW_TILE or t % ROW_TILE == 0)
    state_bytes = n_heads * HD_C * HD_C * 4 * (5 if has_state else 3)
    nb, tt = _streams_per_step(b, t, state_bytes)
    n_rows = nb * tt
    n_chunks = n_rows // CHUNK
    prep_chunks = max(c for c in (1, 2, 4) if n_chunks % c == 0)
    consts = [g_pre.reshape(1, d), w_in_bf, w_gate_bf, conv_w, ab, onorm, w_out_bf, g_post.reshape(1, d)]
    in_specs = [pl.BlockSpec((nb, tt, d), lambda bi, i: (bi, i, 0))] + [_resident(c.shape) for c in consts]
    args = [x] + consts
    if has_state:
        in_specs += [_layer_slot_in(hist, nb, slot), _layer_slot_in(s0, nb, slot)]
        args += [hist, s0]
    s_spec, s_shape = _stacked_out((b, n_heads, HD_C, HD_C), (nb, n_heads, HD_C, HD_C), slot, n_slots)
    aliases = {}
    if s_stack is not None:
        aliases = {len(args): 1}
        in_specs.append(pl.BlockSpec(memory_space=pl.ANY))
        args.append(s_stack)
    x_new, s_new, ctail = pl.pallas_call(
        functools.partial(_odd_layer_kernel, nb=nb, tt=tt, has_state=has_state, prep_chunks=prep_chunks,
                          n_alias=len(aliases)),
        grid=(b // nb, t // tt), in_specs=in_specs,
        out_specs=[pl.BlockSpec((nb, tt, d), lambda bi, i: (bi, i, 0)),
                   s_spec,
                   pl.BlockSpec((nb, SUBLANES, 3 * w), lambda bi, i: (bi, 0, 0))],
        out_shape=[jax.ShapeDtypeStruct((b, t, d), F32),
                   s_shape,
                   jax.ShapeDtypeStruct((b, SUBLANES, 3 * w), F32)],
        input_output_aliases=aliases,
        scratch_shapes=[pltpu.VMEM((nb, 3, SUBLANES, w), F32),
                        pltpu.VMEM((n_rows, w), F32), pltpu.VMEM((n_rows, w), F32), pltpu.VMEM((n_rows, w), F32),
                        pltpu.VMEM((n_rows, w), BF16),
                        pltpu.VMEM((n_rows, LANES), F32), pltpu.VMEM((n_rows, LANES), F32),
                        pltpu.VMEM((nb, n_heads, HD_C, HD_C), F32),
                        pltpu.VMEM((n_rows, w), BF16),
                        pltpu.VMEM((n_rows, w), BF16), pltpu.VMEM((n_rows, w), BF16), pltpu.VMEM((n_rows, w), F32),
                        pltpu.VMEM((n_heads // 2, n_rows, 2 * CHUNK), BF16),
                        pltpu.VMEM((n_heads, n_chunks, HD_C, CHUNK), BF16),
                        pltpu.VMEM((n_chunks, n_heads, HD_C), F32)],
        compiler_params=_params(), name="odd_layer")(*args)
    return x_new, ctail[:, SUBLANES - 3:], s_new


def _cast_kernel(w_ref, *rest):
    if len(rest) == 2:
        scale_ref, o_ref = rest
        o_ref[...] = (w_ref[...] * scale_ref[...]).astype(o_ref.dtype)
    else:
        (o_ref,) = rest
        o_ref[...] = w_ref[...].astype(o_ref.dtype)


def _to_bf16(w_all, layer, n_cols, col_scale=None):
    k = w_all.shape[1]
    assert n_cols % CAST_COLS == 0
    in_specs = [pl.BlockSpec((None, k, CAST_COLS), lambda j: (layer, 0, j))]
    args = [w_all]
    if col_scale is not None:
        in_specs.append(pl.BlockSpec((1, CAST_COLS), lambda j: (0, j)))
        args.append(col_scale.reshape(1, n_cols))
    return pl.pallas_call(
        _cast_kernel, grid=(n_cols // CAST_COLS,), in_specs=in_specs,
        out_specs=pl.BlockSpec((k, CAST_COLS), lambda j: (0, j)),
        out_shape=jax.ShapeDtypeStruct((k, n_cols), BF16),
        compiler_params=pltpu.CompilerParams(dimension_semantics=("parallel",)), name="to_bf16")(*args)


def _gate_cast_kernel(w_ref, o_ref, *, n_gate):
    lane = lax.broadcasted_iota(jnp.int32, w_ref.shape, 1)
    o_ref[...] = jnp.where(lane < n_gate, w_ref[...], 0.0).astype(o_ref.dtype)


def _gate_to_bf16(w_all, layer, first_col, n_gate):
    k = w_all.shape[1]
    assert first_col % LANES == 0 and n_gate <= LANES and first_col + n_gate == w_all.shape[2]
    return pl.pallas_call(
        functools.partial(_gate_cast_kernel, n_gate=n_gate), grid=(1,),
        in_specs=[pl.BlockSpec((None, k, LANES), lambda j: (layer, 0, first_col // LANES))],
        out_specs=pl.BlockSpec((k, LANES), lambda j: (0, 0)),
        out_shape=jax.ShapeDtypeStruct((k, LANES), BF16), name="gate_to_bf16")(w_all)


def _band_bias(table):
    n_heads = table.shape[0]
    assert CHUNK - 1 <= REL_CLIP < BAND - 1
    low = table[:, REL_CLIP - (CHUNK - 1):]
    high = jnp.broadcast_to(table[:, 2 * REL_CLIP:], (n_heads, BAND - 1 - REL_CLIP))
    by_dist = jnp.concatenate([low, high], axis=1)[:, ::-1]
    rows = [by_dist[:, CHUNK - 1 - qi:CHUNK - 1 - qi + BAND] for qi in range(CHUNK)]
    bias = jnp.stack(rows, axis=1).astype(F32) * LOG2E
    return bias.reshape(n_heads // 2, 2 * CHUNK, BAND)


def kernel(x_prompt, x_sample, cache_conv_a, cache_k_b, cache_v_b, state_conv_c, state_s_c, norm_pre, norm_post, w_in_even, conv_w_a, rel_bias_b, w_out_even, w_in_odd, conv_w_c, a_log_c, dt_bias_c, out_norm_c, w_out_odd):
    depth = norm_pre.shape[0]
    n_even, n_odd = w_in_even.shape[0], w_in_odd.shape[0]
    n_heads_b = cache_k_b.shape[-2]
    yp, ys = x_prompt, x_sample
    conv_a_p, conv_a_s, conv_c_p, conv_c_s = [], [], [], []
    kv_p = kv_s = s_p = s_s = None
    for layer in range(depth):
        i = layer // 2
        g_pre, g_post = norm_pre[layer], norm_post[layer]
        if layer % 2 == 0:
            n_in = w_in_even.shape[2]
            w_blk = n_in // 8
            col_scale = jnp.ones((n_in,), F32).at[4 * w_blk:5 * w_blk].set(HD_B ** -0.5 * LOG2E)
            w_in_bf = _to_bf16(w_in_even, i, n_in, col_scale)
            w_out_bf = _to_bf16(w_out_even, i, w_out_even.shape[2])
            shared = (g_pre, g_post, w_in_bf, conv_w_a[i], _band_bias(rel_bias_b[i]), w_out_bf, i, n_even)
            yp, hst, kv_p = _even_layer(yp, None, None, None, *shared, kv_p)
            conv_a_p.append(hst)
            ys, hst, kv_s = _even_layer(ys, cache_conv_a, cache_k_b[i], cache_v_b[i], *shared, kv_s)
            conv_a_s.append(hst)
        else:
            n_heads = a_log_c.shape[1]
            n_main = w_in_odd.shape[2] - 2 * n_heads
            w_main_bf = _to_bf16(w_in_odd, i, n_main)
            w_gate_bf = _gate_to_bf16(w_in_odd, i, n_main, 2 * n_heads)
            pad = (n_heads, LANES - 2 * n_heads)
            ab = jnp.stack([jnp.pad(a_log_c[i], pad), jnp.pad(dt_bias_c[i], pad)]).astype(F32)
            shared = (g_pre, g_post, w_main_bf, w_gate_bf, conv_w_c[i], ab, out_norm_c[i].reshape(1, -1),
                      _to_bf16(w_out_odd, i, w_out_odd.shape[2]), i, n_odd)
            yp, hst, s_p = _odd_layer(yp, None, None, *shared, s_p)
            conv_c_p.append(hst)
            ys, hst, s_s = _odd_layer(ys, state_conv_c, state_s_c, *shared, s_s)
            conv_c_s.append(hst)

    def heads_apart(kv):
        return kv.reshape(kv.shape[:-1] + (n_heads_b, kv.shape[-1] // n_heads_b))

    return (yp, ys,
            jnp.stack(conv_a_p), heads_apart(kv_p[0]), heads_apart(kv_p[1]), jnp.stack(conv_c_p), s_p,
            jnp.stack(conv_a_s), heads_apart(kv_s[0]), heads_apart(kv_s[1]), jnp.stack(conv_c_s), s_s)
```

```python
import functools
import itertools
import math

import jax
import jax.numpy as jnp
from jax import lax
from jax.experimental import pallas as pl
from jax.experimental.pallas import tpu as pltpu

F32 = jnp.float32
BF16 = jnp.bfloat16

EPS = 1e-6
LOG2E = math.log2(math.e)
CHUNK = 64
BAND_PAST = 8 * CHUNK
BAND = BAND_PAST + CHUNK
REL_CLIP = 4 * CHUNK
HD_B = 64
HD_C = 128
LANES = 128
SUBLANES = 8
VMEM_LIMIT_BYTES = 56 * 1024 * 1024
STATE_VMEM_BUDGET_BYTES = 16 * 1024 * 1024
ROW_TILE = 512
CAST_COLS = 512
PROJ_BLOCKS = 4


def _silu(x):
    return x * (1.0 / (1.0 + jnp.exp(-x)))


def _softplus(x):
    return jnp.maximum(x, 0.0) + jnp.log1p(jnp.exp(-jnp.abs(x)))


def _dot(a, b):
    return jnp.dot(a.astype(BF16), b.astype(BF16), preferred_element_type=F32)


def _load_rows(ref, nb):
    parts = [ref[s] for s in range(nb)]
    return parts[0] if nb == 1 else jnp.concatenate(parts, axis=0)


def _pre_norm(x_ref, gain_ref, nb):
    x = _load_rows(x_ref, nb)
    ms = jnp.mean(x * x, axis=-1, keepdims=True)
    return (x * lax.rsqrt(ms + EPS) * gain_ref[...]).astype(BF16)


def _project_out(ybuf, wout_ref, gain_ref, x_ref, xo_ref, tt, r0, r1):
    out = jnp.dot(ybuf[r0:r1, :], wout_ref[...], preferred_element_type=F32)
    ms = jnp.mean(out * out, axis=-1, keepdims=True)
    delta = out * lax.rsqrt(ms + EPS) * gain_ref[...]
    for s in range(x_ref.shape[0]):
        lo, hi = max(r0, s * tt), min(r1, (s + 1) * tt)
        if lo < hi:
            frames = slice(lo - s * tt, hi - s * tt)
            xo_ref[s, frames, :] = x_ref[s, frames, :] + delta[lo - r0:hi - r0]
    yield


def _causal_dwconv(prev_rows, new_rows, taps):
    tt = new_rows.shape[0]
    width = taps.shape[0]
    assert width <= SUBLANES
    full = jnp.concatenate([prev_rows, new_rows], axis=0)
    sums = [None, None]
    shifted = full
    for delay in range(width):
        if delay and delay % 2 == 0:
            shifted = pltpu.roll(shifted, 2, axis=0)
        term = shifted * taps[width - 1 - delay:width - delay]
        sums[delay % 2] = term if sums[delay % 2] is None else sums[delay % 2] + term
    acc = sums[0] if sums[1] is None else sums[0] + pltpu.roll(sums[1], 1, axis=0)
    return acc[SUBLANES:, :], full[tt:tt + SUBLANES, :]


def _streams_per_step(b, t, state_bytes_per_stream):
    tt = min(t, ROW_TILE)
    nb = max(1, min(ROW_TILE // tt, STATE_VMEM_BUDGET_BYTES // state_bytes_per_stream))
    while b % nb:
        nb -= 1
    return nb, tt


def _resident(shape):
    return pl.BlockSpec(shape, lambda bi, i: (0,) * len(shape), pipeline_mode=pl.Buffered(1))


def _params():
    return pltpu.CompilerParams(dimension_semantics=("parallel", "arbitrary"), vmem_limit_bytes=VMEM_LIMIT_BYTES)


def _even_layer_kernel(*refs, nb, tt, has_cache, n_alias):
    x_ref, gpre_ref, win_ref, cw_ref, bias_ref, wout_ref, gpost_ref = refs[:7]
    if has_cache:
        hist_ref, kc_ref, vc_ref = refs[7:10]
        refs = refs[10:]
    else:
        refs = refs[7:]
    refs = refs[n_alias:]
    xo_ref, utail_ref, knew_ref, vnew_ref, utails, kbuf, vbuf, qbuf, zbuf, ybuf = refs
    w = win_ref.shape[-1] // 8
    i = pl.program_id(1)

    @pl.when(i == 0)
    def _init():
        utails[...] = jnp.zeros(utails.shape, F32)
        if has_cache:
            for s in range(nb):
                utails[s, SUBLANES - 2:SUBLANES, :] = hist_ref[s]
            kbuf[:, 0:BAND_PAST, :] = kc_ref[...]
            vbuf[:, 0:BAND_PAST, :] = vc_ref[...]
        else:
            kbuf[:, 0:BAND_PAST, :] = jnp.zeros((nb, BAND_PAST, w), BF16)
            vbuf[:, 0:BAND_PAST, :] = jnp.zeros((nb, BAND_PAST, w), BF16)

    hb = _pre_norm(x_ref, gpre_ref, nb)

    def proj(j):
        return jnp.dot(hb, win_ref[:, j * w:(j + 1) * w], preferred_element_type=F32)

    u = proj(1) * proj(2)
    gate = proj(0) * _silu(proj(3))
    for s in range(nb):
        srows = slice(s * tt, (s + 1) * tt)
        conv, tail = _causal_dwconv(utails[s], u[srows], cw_ref[...])
        utail_ref[s] = tail
        utails[s] = tail
        ybuf[srows, 0:w] = (gate[srows] * conv).astype(ybuf.dtype)

    qbuf[...] = proj(4).astype(qbuf.dtype)
    k = proj(5)
    v = proj(6)
    for s in range(nb):
        srows = slice(s * tt, (s + 1) * tt)
        knew_ref[s] = k[srows]
        vnew_ref[s] = v[srows]
        kbuf[s, BAND_PAST:BAND_PAST + tt, :] = k[srows].astype(kbuf.dtype)
        vbuf[s, BAND_PAST:BAND_PAST + tt, :] = v[srows].astype(vbuf.dtype)
    zbuf[...] = _silu(proj(7)).astype(zbuf.dtype)

    chunks_per_stream = tt // CHUNK
    n_pairs = w // LANES
    lane = lax.broadcasted_iota(jnp.int32, (CHUNK, LANES), 1)
    first_head = lane < HD_B

    def attend(first_tile):
        def band_start(c):
            missing = max(BAND_PAST - CHUNK * c, 0) if first_tile else 0
            return missing // LANES * LANES, missing % LANES

        def scores(s, c, hp):
            cols = slice(hp * LANES, (hp + 1) * LANES)
            lo, masked = band_start(c)
            r0 = s * tt + c * CHUNK
            qp = qbuf[r0:r0 + CHUNK, cols]
            zero = jnp.zeros_like(qp)
            q2 = jnp.concatenate([jnp.where(first_head, qp, zero), jnp.where(first_head, zero, qp)], axis=0)
            kp = kbuf[s, c * CHUNK + lo:c * CHUNK + BAND, cols]
            sc = lax.dot_general(q2, kp, (((1,), (1,)), ((), ())), preferred_element_type=F32)
            sc = sc + bias_ref[hp, :, lo:BAND]
            if masked:
                key_idx = lax.broadcasted_iota(jnp.int32, sc.shape, 1)
                sc = jnp.where(key_idx >= masked, sc, -jnp.inf)
            return sc

        def weighted_values(s, c, hp, sc):
            cols = slice(hp * LANES, (hp + 1) * LANES)
            lo, _ = band_start(c)
            m = jnp.max(sc, axis=-1, keepdims=True)
            e = jnp.exp2(sc - m)
            l = jnp.sum(e, axis=-1, keepdims=True)
            vp = vbuf[s, c * CHUNK + lo:c * CHUNK + BAND, cols]
            pv = jnp.dot(e.astype(BF16), vp, preferred_element_type=F32) / l
            return jnp.where(first_head, pv[0:CHUNK], pv[CHUNK:2 * CHUNK])

        items = [(s, c, hp) for s in range(nb) for c in range(chunks_per_stream) for hp in range(n_pairs)]
        skew = 3
        pending, outs = {}, {}
        for step in range(len(items) + skew):
            if step < len(items):
                pending[step] = scores(*items[step])
            j = step - skew
            if j >= 0:
                s, c, hp = items[j]
                outs[hp] = weighted_values(s, c, hp, pending.pop(j))
                if hp == n_pairs - 1:
                    rows = slice(s * tt + c * CHUNK, s * tt + (c + 1) * CHUNK)
                    o = jnp.concatenate([outs[x] for x in range(n_pairs)], axis=1)
                    ybuf[rows, w:2 * w] = (o * zbuf[rows, :].astype(F32)).astype(ybuf.dtype)

    if has_cache:
        attend(False)
    else:
        pl.when(i == 0)(functools.partial(attend, True))
        pl.when(i > 0)(functools.partial(attend, False))
    for s in range(nb):
        kbuf[s, 0:BAND_PAST, :] = kbuf[s, tt:tt + BAND_PAST, :]
        vbuf[s, 0:BAND_PAST, :] = vbuf[s, tt:tt + BAND_PAST, :]

    _interleave(_project_out(ybuf, wout_ref, gpost_ref, x_ref, xo_ref, tt, 0, nb * tt))


def _layer_slot_in(stacked, nb, slot):
    rest = stacked.shape[2:]
    return pl.BlockSpec((None, nb) + rest, lambda bi, i: (slot, bi) + (0,) * len(rest))


def _stacked_out(shape, block, slot, n_slots):
    spec = pl.BlockSpec((None,) + block, lambda bi, i: (slot, bi) + (0,) * (len(block) - 1))
    return spec, jax.ShapeDtypeStruct((n_slots,) + shape, F32)


def _even_layer(x, hist, kc, vc, g_pre, g_post, w_in_bf, conv_w, bias2, w_out_bf, slot, n_slots, kv_stacks):
    b, t, d = x.shape
    w = w_in_bf.shape[1] // 8
    n_heads = w // HD_B
    has_cache = kc is not None
    assert t % CHUNK == 0 and (t <= ROW_TILE or t % ROW_TILE == 0)
    band_bytes = 2 * (BAND_PAST + min(t, ROW_TILE)) * w * 2
    state_bytes = band_bytes + (2 * 2 * BAND_PAST * w * 2 if has_cache else 0)
    nb, tt = _streams_per_step(b, t, state_bytes)
    n_rows = nb * tt
    consts = [g_pre.reshape(1, d), w_in_bf, conv_w, bias2, w_out_bf, g_post.reshape(1, d)]
    in_specs = [pl.BlockSpec((nb, tt, d), lambda bi, i: (bi, i, 0))] + [_resident(c.shape) for c in consts]
    args = [x] + consts
    if has_cache:
        kc = kc.reshape(b, BAND_PAST, w).astype(BF16)
        vc = vc.reshape(b, BAND_PAST, w).astype(BF16)
        in_specs += [_layer_slot_in(hist, nb, slot),
                     pl.BlockSpec((nb, BAND_PAST, w), lambda bi, i: (bi, 0, 0)),
                     pl.BlockSpec((nb, BAND_PAST, w), lambda bi, i: (bi, 0, 0))]
        args += [hist, kc, vc]
    assert tt == min(BAND_PAST, t)
    kv_spec, kv_shape = _stacked_out((b, tt, w), (nb, tt, w), slot, n_slots)
    aliases = {}
    if kv_stacks is not None:
        aliases = {len(args): 2, len(args) + 1: 3}
        in_specs += [pl.BlockSpec(memory_space=pl.ANY)] * 2
        args += list(kv_stacks)
    x_new, utail, k_stack, v_stack = pl.pallas_call(
        functools.partial(_even_layer_kernel, nb=nb, tt=tt, has_cache=has_cache, n_alias=len(aliases)),
        grid=(b // nb, t // tt), in_specs=in_specs,
        out_specs=[pl.BlockSpec((nb, tt, d), lambda bi, i: (bi, i, 0)),
                   pl.BlockSpec((nb, SUBLANES, w), lambda bi, i: (bi, 0, 0)),
                   kv_spec, kv_spec],
        out_shape=[jax.ShapeDtypeStruct((b, t, d), F32),
                   jax.ShapeDtypeStruct((b, SUBLANES, w), F32),
                   kv_shape, kv_shape],
        input_output_aliases=aliases,
        scratch_shapes=[pltpu.VMEM((nb, SUBLANES, w), F32),
                        pltpu.VMEM((nb, BAND_PAST + tt, w), BF16), pltpu.VMEM((nb, BAND_PAST + tt, w), BF16),
                        pltpu.VMEM((n_rows, w), BF16), pltpu.VMEM((n_rows, w), BF16),
                        pltpu.VMEM((n_rows, 2 * w), BF16)],
        compiler_params=_params(), name="even_layer")(*args)
    del n_heads
    return x_new, utail[:, SUBLANES - 2:], (k_stack, v_stack)


def _cumsum_rows(x):
    n = x.shape[0]
    row = lax.broadcasted_iota(jnp.int32, x.shape, 0)
    shift = 1
    while shift < n:
        x = x + jnp.where(row >= shift, pltpu.roll(x, shift, axis=0), 0.0)
        shift *= 2
    return x


def _pair_masks():
    row = lax.broadcasted_iota(jnp.int32, (CHUNK, 2 * CHUNK), 0)
    lane = lax.broadcasted_iota(jnp.int32, (CHUNK, 2 * CHUNK), 1)
    first = lane < CHUNK
    col = jnp.where(first, lane, lane - CHUNK)
    return first, row, col


def _block_diag(pk, first):
    tiled = jnp.concatenate([pk, pk], axis=0).astype(BF16)
    keep = jnp.concatenate([first, jnp.logical_not(first)], axis=0)
    return jnp.where(keep, tiled, jnp.zeros_like(tiled))


def _unit_lower_inverses(mats, out, first, row, col):
    n = mats[0].shape[0]
    eye = jnp.where(row == col, 1.0, 0.0)
    xs = [-a for a in mats]
    ps = [eye + x for x in xs]
    ys = [jnp.dot(x.astype(BF16), _block_diag(x, first), preferred_element_type=F32) for x in xs]
    yield
    power = 2
    while 2 * power < n:
        sts = [jnp.dot(jnp.concatenate([p, y], axis=0).astype(BF16), _block_diag(y, first),
                       preferred_element_type=F32) for p, y in zip(ps, ys)]
        ps = [p + st[0:n] for p, st in zip(ps, sts)]
        ys = [st[n:2 * n] for st in sts]
        power *= 2
        yield
    out[:] = [p + jnp.dot(p.astype(BF16), _block_diag(y, first), preferred_element_type=F32)
              for p, y in zip(ps, ys)]
    yield


def _weave(main, fillers, period):
    filler = itertools.chain(*fillers)
    for n, _ in enumerate(main, start=1):
        if n % period == 0:
            next(filler, None)
    for _ in filler:
        pass


def _interleave(*generators):
    live = list(generators)
    while live:
        for g in list(live):
            try:
                next(g)
            except StopIteration:
                live.remove(g)


def _odd_layer_kernel(*refs, nb, tt, has_state, prep_chunks, n_alias):
    x_ref, gpre_ref, win_ref, wg_ref, cw_ref, ab_ref, on_ref, wout_ref, gpost_ref = refs[:9]
    if has_state:
        hist_ref, s0_ref = refs[9:11]
        refs = refs[11:]
    else:
        refs = refs[9:]
    refs = refs[n_alias:]
    (xo_ref, sout_ref, ctail_ref, ctails, qs, ks, vs, zs, gsc, bsc, sbuf, ybuf,
     wbuf, qtbuf, u0buf, qkbuf, kdtbuf, eglbuf) = refs
    w = win_ref.shape[-1] // 4
    n_heads = w // HD_C
    n_rows = nb * tt
    i = pl.program_id(1)

    @pl.when(i == 0)
    def _init():
        ctails[...] = jnp.zeros(ctails.shape, F32)
        if has_state:
            for s in range(nb):
                for g in range(3):
                    ctails[s, g, SUBLANES - 3:SUBLANES, :] = hist_ref[s, :, g * w:(g + 1) * w]
            sbuf[...] = s0_ref[...]
        else:
            sbuf[...] = jnp.zeros(sbuf.shape, F32)

    hb = _pre_norm(x_ref, gpre_ref, nb)

    gl = jnp.dot(hb, wg_ref[...], preferred_element_type=F32)
    bsc[...] = 1.0 / (1.0 + jnp.exp(-gl))
    gsc[...] = -jnp.exp(ab_ref[0:1, :]) * _softplus(gl + ab_ref[1:2, :])

    def project_group(g, dst, n_blocks):
        inv_scale_sq = float(HD_C) if g == 0 else 1.0
        sum_mat = jnp.full((HD_C, HD_C), inv_scale_sq, BF16)
        bw = w // n_blocks
        for blk in range(n_blocks):
            gcols = slice(g * w + blk * bw, g * w + (blk + 1) * bw)
            raw = jnp.dot(hb, win_ref[:, gcols], preferred_element_type=F32)
            for s in range(nb):
                srows = slice(s * tt, (s + 1) * tt)
                c, tail = _causal_dwconv(ctails[s, g, :, blk * bw:(blk + 1) * bw], raw[srows], cw_ref[:, gcols])
                c = _silu(c)
                ctail_ref[s, :, gcols] = tail
                ctails[s, g, :, blk * bw:(blk + 1) * bw] = tail
                if g == 2:
                    dst[srows, blk * bw:(blk + 1) * bw] = c
                else:
                    for h in range(bw // HD_C):
                        ch = c[:, h * HD_C:(h + 1) * HD_C]
                        ss = jnp.dot((ch * ch).astype(BF16), sum_mat, preferred_element_type=F32)
                        dst[srows, blk * bw + h * HD_C:blk * bw + (h + 1) * HD_C] = ch * lax.rsqrt(
                            ss + EPS * inv_scale_sq)
            yield

    def project_gate(n_blocks):
        bw = w // n_blocks
        for blk in range(n_blocks):
            z = jnp.dot(hb, win_ref[:, 3 * w + blk * bw:3 * w + (blk + 1) * bw], preferred_element_type=F32)
            zs[:, blk * bw:(blk + 1) * bw] = _silu(z).astype(zs.dtype)
            yield

    _interleave(project_group(0, qs, 1))
    _interleave(project_group(1, ks, 1))

    first, row, col = _pair_masks()
    incl = row >= col
    strict = row > col
    assert n_heads % 2 == 0 and 2 * CHUNK == HD_C
    pairs = range(n_heads // 2)

    heads = range(n_heads)
    hcols = [slice(h * HD_C, (h + 1) * HD_C) for h in heads]
    n_chunks = n_rows // CHUNK

    def chunk_rows(cidx):
        if isinstance(cidx, int):
            return slice(cidx * CHUNK, (cidx + 1) * CHUNK)
        return pl.ds(pl.multiple_of(cidx * CHUNK, CHUNK), CHUNK)

    def prepare_chunks(step):
        items = []
        duos = []
        for j in range(prep_chunks):
            cidx = step * prep_chunks + j
            rows = chunk_rows(cidx)
            gc = _cumsum_rows(gsc[rows, :])
            gct = gc.T
            bet = bsc[rows, :]
            for h in heads:
                gcol = jnp.broadcast_to(gc[:, n_heads + h:n_heads + h + 1], (CHUNK, HD_C))
                items.append(dict(cidx=cidx, rows=rows, h=h, gcol=gcol,
                                  b=jnp.broadcast_to(bet[:, h:h + 1], (CHUNK, HD_C))))
            for p in pairs:
                ia, ib = items[-n_heads + 2 * p], items[-n_heads + 2 * p + 1]
                g0 = n_heads + 2 * p
                grow = jnp.concatenate([gct[g0:g0 + 1, :], gct[g0 + 1:g0 + 2, :]], axis=1)
                gcol2 = jnp.where(first, ia["gcol"], ib["gcol"])
                duos.append(dict(cidx=cidx, rows=rows, p=p, a=ia, b=ib, beta=jnp.where(first, ia["b"], ib["b"]),
                                 decay=jnp.exp(jnp.where(incl, gcol2 - grow, -jnp.inf))))
        for it in items:
            rows, c = it["rows"], hcols[it["h"]]
            it["eg"] = jnp.exp(it["gcol"])
            it["q"], it["k"] = qs[rows, c], ks[rows, c]
        for it in items:
            kb = it["k"].astype(BF16)
            it["qk_kk"] = lax.dot_general(jnp.concatenate([it["q"].astype(BF16), kb], axis=0), kb,
                                          (((1,), (1,)), ((), ())), preferred_element_type=F32)
        yield
        for du in duos:
            qk_kk = jnp.concatenate([du["a"]["qk_kk"], du["b"]["qk_kk"]], axis=1)
            du["qk"] = qk_kk[0:CHUNK] * du["decay"]
            du["mat"] = jnp.where(strict, du["beta"] * qk_kk[CHUNK:2 * CHUNK] * du["decay"], 0.0)
        tinvs = [None] * len(duos)
        yield from _unit_lower_inverses([du["mat"] for du in duos], tinvs, first, row, col)
        sols = []
        for tinv, du in zip(tinvs, duos):
            rhs = jnp.concatenate(
                [jnp.concatenate([vs[it["rows"], hcols[it["h"]]] * it["b"], it["k"] * (it["b"] * it["eg"])], axis=1)
                 for it in (du["a"], du["b"])], axis=0)
            sols.append(jnp.dot(_block_diag(tinv, first), rhs.astype(BF16), preferred_element_type=F32))
        yield
        for sol, du in zip(sols, duos):
            rows = du["rows"]
            qkbuf[du["p"], rows, :] = du["qk"].astype(BF16)
            for half, it in enumerate((du["a"], du["b"])):
                gcol, c, h = it["gcol"], hcols[it["h"]], it["h"]
                g_last = gcol[CHUNK - 1:CHUNK, :]
                sol_h = sol[half * CHUNK:(half + 1) * CHUNK]
                u0buf[rows, c] = sol_h[:, 0:HD_C]
                wbuf[rows, c] = sol_h[:, HD_C:].astype(BF16)
                qtbuf[rows, c] = (it["q"] * it["eg"]).astype(BF16)
                kdtbuf[h, it["cidx"]] = (it["k"] * jnp.exp(g_last - gcol)).T.astype(BF16)
                eglbuf[it["cidx"], h:h + 1, :] = jnp.exp(g_last)

    chunks_per_stream = tt // CHUNK

    def recur_chunks(step):
        for j in range(prep_chunks):
            cidx = step * prep_chunks + j
            rows = chunk_rows(cidx)
            s = 0 if nb == 1 else cidx // chunks_per_stream
            s_old = [sbuf[s, h] for h in heads]
            ws_qs = [_dot(jnp.concatenate([wbuf[rows, hcols[h]], qtbuf[rows, hcols[h]]], axis=0), s_old[h])
                     for h in heads]
            yield
            u = [u0buf[rows, hcols[h]] - ws_qs[h][0:CHUNK] for h in heads]
            qku = [jnp.dot(_block_diag(qkbuf[p, rows, :], first),
                           jnp.concatenate([u[2 * p], u[2 * p + 1]], axis=0).astype(BF16),
                           preferred_element_type=F32) for p in pairs]
            o = [ws_qs[h][CHUNK:2 * CHUNK] + qku[h // 2][(h % 2) * CHUNK:(h % 2 + 1) * CHUNK] for h in heads]
            for h in heads:
                sbuf[s, h] = eglbuf[cidx, h:h + 1, :] * s_old[h] + _dot(kdtbuf[h, cidx], u[h])
            yield
            for h in heads:
                ms = jnp.mean(o[h] * o[h], axis=-1, keepdims=True)
                on = o[h] * lax.rsqrt(ms + EPS) * on_ref[...]
                ybuf[rows, hcols[h]] = (on * zs[rows, hcols[h]].astype(F32)).astype(ybuf.dtype)

    n_steps = n_chunks // prep_chunks
    _weave(prepare_chunks(0), [project_group(2, vs, PROJ_BLOCKS), project_gate(PROJ_BLOCKS)], period=1)

    def pipelined(step, carry):
        _interleave(prepare_chunks(step), recur_chunks(step - 1))
        return carry

    lax.fori_loop(1, n_steps, pipelined, 0)
    _interleave(recur_chunks(n_steps - 1))
    sout_ref[...] = sbuf[...]
    _interleave(_project_out(ybuf, wout_ref, gpost_ref, x_ref, xo_ref, tt, 0, n_rows))


def _odd_layer(x, hist, s0, g_pre, g_post, w_in_bf, w_gate_bf, conv_w, ab, onorm, w_out_bf, slot, n_slots, s_stack):
    b, t, d = x.shape
    w = w_in_bf.shape[1] // 4
    n_heads = w // HD_C
    has_state = s0 is not None
    assert t % CHUNK == 0 and (t <= ROW_TILE or t % ROW_TILE == 0)
    state_bytes = n_heads * HD_C * HD_C * 4 * (5 if has_state else 3)
    nb, tt = _streams_per_step(b, t, state_bytes)
    n_rows = nb * tt
    n_chunks = n_rows // CHUNK
    prep_chunks = max(c for c in (1, 2, 4) if n_chunks % c == 0)
    consts = [g_pre.reshape(1, d), w_in_bf, w_gate_bf, conv_w, ab, onorm, w_out_bf, g_post.reshape(1, d)]
    in_specs = [pl.BlockSpec((nb, tt, d), lambda bi, i: (bi, i, 0))] + [_resident(c.shape) for c in consts]
    args = [x] + consts
    if has_state:
        in_specs += [_layer_slot_in(hist, nb, slot), _layer_slot_in(s0, nb, slot)]
        args += [hist, s0]
    s_spec, s_shape = _stacked_out((b, n_heads, HD_C, HD_C), (nb, n_heads, HD_C, HD_C), slot, n_slots)
    aliases = {}
    if s_stack is not None:
        aliases = {len(args): 1}
        in_specs.append(pl.BlockSpec(memory_space=pl.ANY))
        args.append(s_stack)
    x_new, s_new, ctail = pl.pallas_call(
        functools.partial(_odd_layer_kernel, nb=nb, tt=tt, has_state=has_state, prep_chunks=prep_chunks,
                          n_alias=len(aliases)),
        grid=(b // nb, t // tt), in_specs=in_specs,
        out_specs=[pl.BlockSpec((nb, tt, d), lambda bi, i: (bi, i, 0)),
                   s_spec,
                   pl.BlockSpec((nb, SUBLANES, 3 * w), lambda bi, i: (bi, 0, 0))],
        out_shape=[jax.ShapeDtypeStruct((b, t, d), F32),
                   s_shape,
                   jax.ShapeDtypeStruct((b, SUBLANES, 3 * w), F32)],
        input_output_aliases=aliases,
        scratch_shapes=[pltpu.VMEM((nb, 3, SUBLANES, w), F32),
                        pltpu.VMEM((n_rows, w), F32), pltpu.VMEM((n_rows, w), F32), pltpu.VMEM((n_rows, w), F32),
                        pltpu.VMEM((n_rows, w), BF16),
                        pltpu.VMEM((n_rows, LANES), F32), pltpu.VMEM((n_rows, LANES), F32),
                        pltpu.VMEM((nb, n_heads, HD_C, HD_C), F32),
                        pltpu.VMEM((n_rows, w), BF16),
                        pltpu.VMEM((n_rows, w), BF16), pltpu.VMEM((n_rows, w), BF16), pltpu.VMEM((n_rows, w), F32),
                        pltpu.VMEM((n_heads // 2, n_rows, 2 * CHUNK), BF16),
                        pltpu.VMEM((n_heads, n_chunks, HD_C, CHUNK), BF16),
                        pltpu.VMEM((n_chunks, n_heads, HD_C), F32)],
        compiler_params=_params(), name="odd_layer")(*args)
    return x_new, ctail[:, SUBLANES - 3:], s_new


def _cast_kernel(w_ref, *rest):
    if len(rest) == 2:
        scale_ref, o_ref = rest
        o_ref[...] = (w_ref[...] * scale_ref[...]).astype(o_ref.dtype)
    else:
        (o_ref,) = rest
        o_ref[...] = w_ref[...].astype(o_ref.dtype)


def _to_bf16(w_all, layer, n_cols, col_scale=None):
    k = w_all.shape[1]
    assert n_cols % CAST_COLS == 0
    in_specs = [pl.BlockSpec((None, k, CAST_COLS), lambda j: (layer, 0, j))]
    args = [w_all]
    if col_scale is not None:
        in_specs.append(pl.BlockSpec((1, CAST_COLS), lambda j: (0, j)))
        args.append(col_scale.reshape(1, n_cols))
    return pl.pallas_call(
        _cast_kernel, grid=(n_cols // CAST_COLS,), in_specs=in_specs,
        out_specs=pl.BlockSpec((k, CAST_COLS), lambda j: (0, j)),
        out_shape=jax.ShapeDtypeStruct((k, n_cols), BF16),
        compiler_params=pltpu.CompilerParams(dimension_semantics=("parallel",)), name="to_bf16")(*args)


def _gate_cast_kernel(w_ref, o_ref, *, n_gate):
    lane = lax.broadcasted_iota(jnp.int32, w_ref.shape, 1)
    o_ref[...] = jnp.where(lane < n_gate, w_ref[...], 0.0).astype(o_ref.dtype)


def _gate_to_bf16(w_all, layer, first_col, n_gate):
    k = w_all.shape[1]
    assert first_col % LANES == 0 and n_gate <= LANES and first_col + n_gate == w_all.shape[2]
    return pl.pallas_call(
        functools.partial(_gate_cast_kernel, n_gate=n_gate), grid=(1,),
        in_specs=[pl.BlockSpec((None, k, LANES), lambda j: (layer, 0, first_col // LANES))],
        out_specs=pl.BlockSpec((k, LANES), lambda j: (0, 0)),
        out_shape=jax.ShapeDtypeStruct((k, LANES), BF16), name="gate_to_bf16")(w_all)


def _band_bias(table):
    n_heads = table.shape[0]
    assert CHUNK - 1 <= REL_CLIP < BAND - 1
    low = table[:, REL_CLIP - (CHUNK - 1):]
    high = jnp.broadcast_to(table[:, 2 * REL_CLIP:], (n_heads, BAND - 1 - REL_CLIP))
    by_dist = jnp.concatenate([low, high], axis=1)[:, ::-1]
    rows = [by_dist[:, CHUNK - 1 - qi:CHUNK - 1 - qi + BAND] for qi in range(CHUNK)]
    bias = jnp.stack(rows, axis=1).astype(F32) * LOG2E
    return bias.reshape(n_heads // 2, 2 * CHUNK, BAND)


def kernel(x_prompt, x_sample, cache_conv_a, cache_k_b, cache_v_b, state_conv_c, state_s_c, norm_pre, norm_post, w_in_even, conv_w_a, rel_bias_b, w_out_even, w_in_odd, conv_w_c, a_log_c, dt_bias_c, out_norm_c, w_out_odd):
    depth = norm_pre.shape[0]
    n_even, n_odd = w_in_even.shape[0], w_in_odd.shape[0]
    n_heads_b = cache_k_b.shape[-2]
    yp, ys = x_prompt, x_sample
    conv_a_p, conv_a_s, conv_c_p, conv_c_s = [], [], [], []
    kv_p = kv_s = s_p = s_s = None
    for layer in range(depth):
        i = layer // 2
        g_pre, g_post = norm_pre[layer], norm_post[layer]
        if layer % 2 == 0:
            n_in = w_in_even.shape[2]
            w_blk = n_in // 8
            col_scale = jnp.ones((n_in,), F32).at[4 * w_blk:5 * w_blk].set(HD_B ** -0.5 * LOG2E)
            w_in_bf = _to_bf16(w_in_even, i, n_in, col_scale)
            w_out_bf = _to_bf16(w_out_even, i, w_out_even.shape[2])
            shared = (g_pre, g_post, w_in_bf, conv_w_a[i], _band_bias(rel_bias_b[i]), w_out_bf, i, n_even)
            yp, hst, kv_p = _even_layer(yp, None, None, None, *shared, kv_p)
            conv_a_p.append(hst)
            ys, hst, kv_s = _even_layer(ys, cache_conv_a, cache_k_b[i], cache_v_b[i], *shared, kv_s)
            conv_a_s.append(hst)
        else:
            n_heads = a_log_c.shape[1]
            n_main = w_in_odd.shape[2] - 2 * n_heads
            w_main_bf = _to_bf16(w_in_odd, i, n_main)
            w_gate_bf = _gate_to_bf16(w_in_odd, i, n_main, 2 * n_heads)
            pad = (n_heads, LANES - 2 * n_heads)
            ab = jnp.stack([jnp.pad(a_log_c[i], pad), jnp.pad(dt_bias_c[i], pad)]).astype(F32)
            shared = (g_pre, g_post, w_main_bf, w_gate_bf, conv_w_c[i], ab, out_norm_c[i].reshape(1, -1),
                      _to_bf16(w_out_odd, i, w_out_odd.shape[2]), i, n_odd)
            yp, hst, s_p = _odd_layer(yp, None, None, *shared, s_p)
            conv_c_p.append(hst)
            ys, hst, s_s = _odd_layer(ys, state_conv_c, state_s_c, *shared, s_s)
            conv_c_s.append(hst)

    def heads_apart(kv):
        return kv.reshape(kv.shape[:-1] + (n_heads_b, kv.shape[-1] // n_heads_b))

    return (yp, ys,
            jnp.stack(conv_a_p), heads_apart(kv_p[0]), heads_apart(kv_p[1]), jnp.stack(conv_c_p), s_p,
            jnp.stack(conv_a_s), heads_apart(kv_s[0]), heads_apart(kv_s[1]), jnp.stack(conv_c_s), s_s)
```

```python
import functools
import itertools
import math

import jax
import jax.numpy as jnp
from jax import lax
from jax.experimental import pallas as pl
from jax.experimental.pallas import tpu as pltpu

F32 = jnp.float32
BF16 = jnp.bfloat16

EPS = 1e-6
LOG2E = math.log2(math.e)
CHUNK = 64
BAND_PAST = 8 * CHUNK
BAND = BAND_PAST + CHUNK
REL_CLIP = 4 * CHUNK
HD_B = 64
HD_C = 128
LANES = 128
SUBLANES = 8
VMEM_LIMIT_BYTES = 56 * 1024 * 1024
STATE_VMEM_BUDGET_BYTES = 16 * 1024 * 1024
ROW_TILE = 512
CAST_COLS = 512
PROJ_BLOCKS = 4


def _silu(x):
    return x * (1.0 / (1.0 + jnp.exp(-x)))


def _softplus(x):
    return jnp.maximum(x, 0.0) + jnp.log1p(jnp.exp(-jnp.abs(x)))


def _dot(a, b):
    return jnp.dot(a.astype(BF16), b.astype(BF16), preferred_element_type=F32)


def _load_rows(ref, nb):
    parts = [ref[s] for s in range(nb)]
    return parts[0] if nb == 1 else jnp.concatenate(parts, axis=0)


def _pre_norm(x_ref, gain_ref, nb):
    x = _load_rows(x_ref, nb)
    ms = jnp.mean(x * x, axis=-1, keepdims=True)
    return (x * lax.rsqrt(ms + EPS) * gain_ref[...]).astype(BF16)


def _project_out(ybuf, wout_ref, gain_ref, x_ref, xo_ref, tt, r0, r1):
    out = jnp.dot(ybuf[r0:r1, :], wout_ref[...], preferred_element_type=F32)
    ms = jnp.mean(out * out, axis=-1, keepdims=True)
    delta = out * lax.rsqrt(ms + EPS) * gain_ref[...]
    for s in range(x_ref.shape[0]):
        lo, hi = max(r0, s * tt), min(r1, (s + 1) * tt)
        if lo < hi:
            frames = slice(lo - s * tt, hi - s * tt)
            xo_ref[s, frames, :] = x_ref[s, frames, :] + delta[lo - r0:hi - r0]
    yield


def _causal_dwconv(prev_rows, new_rows, taps):
    tt = new_rows.shape[0]
    width = taps.shape[0]
    assert width <= SUBLANES
    full = jnp.concatenate([prev_rows, new_rows], axis=0)
    sums = [None, None]
    shifted = full
    for delay in range(width):
        if delay and delay % 2 == 0:
            shifted = pltpu.roll(shifted, 2, axis=0)
        term = shifted * taps[width - 1 - delay:width - delay]
        sums[delay % 2] = term if sums[delay % 2] is None else sums[delay % 2] + term
    acc = sums[0] if sums[1] is None else sums[0] + pltpu.roll(sums[1], 1, axis=0)
    return acc[SUBLANES:, :], full[tt:tt + SUBLANES, :]


def _streams_per_step(b, t, state_bytes_per_stream):
    tt = min(t, ROW_TILE)
    nb = max(1, min(ROW_TILE // tt, STATE_VMEM_BUDGET_BYTES // state_bytes_per_stream))
    while b % nb:
        nb -= 1
    return nb, tt


def _resident(shape):
    return pl.BlockSpec(shape, lambda bi, i: (0,) * len(shape), pipeline_mode=pl.Buffered(1))


def _params():
    return pltpu.CompilerParams(dimension_semantics=("parallel", "arbitrary"), vmem_limit_bytes=VMEM_LIMIT_BYTES)


def _even_layer_kernel(*refs, nb, tt, has_cache, n_alias):
    x_ref, gpre_ref, win_ref, cw_ref, bias_ref, wout_ref, gpost_ref = refs[:7]
    if has_cache:
        hist_ref, kc_ref, vc_ref = refs[7:10]
        refs = refs[10:]
    else:
        refs = refs[7:]
    refs = refs[n_alias:]
    xo_ref, utail_ref, knew_ref, vnew_ref, utails, kbuf, vbuf, qbuf, zbuf, ybuf = refs
    w = win_ref.shape[-1] // 8
    i = pl.program_id(1)

    @pl.when(i == 0)
    def _init():
        utails[...] = jnp.zeros(utails.shape, F32)
        if has_cache:
            for s in range(nb):
                utails[s, SUBLANES - 2:SUBLANES, :] = hist_ref[s]
            kbuf[:, 0:BAND_PAST, :] = kc_ref[...]
            vbuf[:, 0:BAND_PAST, :] = vc_ref[...]
        else:
            kbuf[:, 0:BAND_PAST, :] = jnp.zeros((nb, BAND_PAST, w), BF16)
            vbuf[:, 0:BAND_PAST, :] = jnp.zeros((nb, BAND_PAST, w), BF16)

    hb = _pre_norm(x_ref, gpre_ref, nb)

    def proj(j):
        return jnp.dot(hb, win_ref[:, j * w:(j + 1) * w], preferred_element_type=F32)

    u = proj(1) * proj(2)
    gate = proj(0) * _silu(proj(3))
    for s in range(nb):
        srows = slice(s * tt, (s + 1) * tt)
        conv, tail = _causal_dwconv(utails[s], u[srows], cw_ref[...])
        utail_ref[s] = tail
        utails[s] = tail
        ybuf[srows, 0:w] = (gate[srows] * conv).astype(ybuf.dtype)

    qbuf[...] = proj(4).astype(qbuf.dtype)
    k = proj(5)
    v = proj(6)
    for s in range(nb):
        srows = slice(s * tt, (s + 1) * tt)
        knew_ref[s] = k[srows]
        vnew_ref[s] = v[srows]
        kbuf[s, BAND_PAST:BAND_PAST + tt, :] = k[srows].astype(kbuf.dtype)
        vbuf[s, BAND_PAST:BAND_PAST + tt, :] = v[srows].astype(vbuf.dtype)
    zbuf[...] = _silu(proj(7)).astype(zbuf.dtype)

    chunks_per_stream = tt // CHUNK
    n_pairs = w // LANES
    lane = lax.broadcasted_iota(jnp.int32, (CHUNK, LANES), 1)
    first_head = lane < HD_B

    def attend(first_tile):
        def band_start(c):
            missing = max(BAND_PAST - CHUNK * c, 0) if first_tile else 0
            return missing // LANES * LANES, missing % LANES

        def scores(s, c, hp):
            cols = slice(hp * LANES, (hp + 1) * LANES)
            lo, masked = band_start(c)
            r0 = s * tt + c * CHUNK
            qp = qbuf[r0:r0 + CHUNK, cols]
            zero = jnp.zeros_like(qp)
            q2 = jnp.concatenate([jnp.where(first_head, qp, zero), jnp.where(first_head, zero, qp)], axis=0)
            kp = kbuf[s, c * CHUNK + lo:c * CHUNK + BAND, cols]
            sc = lax.dot_general(q2, kp, (((1,), (1,)), ((), ())), preferred_element_type=F32)
            sc = sc + bias_ref[hp, :, lo:BAND]
            if masked:
                key_idx = lax.broadcasted_iota(jnp.int32, sc.shape, 1)
                sc = jnp.where(key_idx >= masked, sc, -jnp.inf)
            return sc

        def weighted_values(s, c, hp, sc):
            cols = slice(hp * LANES, (hp + 1) * LANES)
            lo, _ = band_start(c)
            m = jnp.max(sc, axis=-1, keepdims=True)
            e = jnp.exp2(sc - m)
            l = jnp.sum(e, axis=-1, keepdims=True)
            vp = vbuf[s, c * CHUNK + lo:c * CHUNK + BAND, cols]
            pv = jnp.dot(e.astype(BF16), vp, preferred_element_type=F32) / l
            return jnp.where(first_head, pv[0:CHUNK], pv[CHUNK:2 * CHUNK])

        items = [(s, c, hp) for s in range(nb) for c in range(chunks_per_stream) for hp in range(n_pairs)]
        skew = 3
        pending, outs = {}, {}
        for step in range(len(items) + skew):
            if step < len(items):
                pending[step] = scores(*items[step])
            j = step - skew
            if j >= 0:
                s, c, hp = items[j]
                outs[hp] = weighted_values(s, c, hp, pending.pop(j))
                if hp == n_pairs - 1:
                    rows = slice(s * tt + c * CHUNK, s * tt + (c + 1) * CHUNK)
                    o = jnp.concatenate([outs[x] for x in range(n_pairs)], axis=1)
                    ybuf[rows, w:2 * w] = (o * zbuf[rows, :].astype(F32)).astype(ybuf.dtype)

    if has_cache:
        attend(False)
    else:
        pl.when(i == 0)(functools.partial(attend, True))
        pl.when(i > 0)(functools.partial(attend, False))
    for s in range(nb):
        kbuf[s, 0:BAND_PAST, :] = kbuf[s, tt:tt + BAND_PAST, :]
        vbuf[s, 0:BAND_PAST, :] = vbuf[s, tt:tt + BAND_PAST, :]

    _interleave(_project_out(ybuf, wout_ref, gpost_ref, x_ref, xo_ref, tt, 0, nb * tt))


def _layer_slot_in(stacked, nb, slot):
    rest = stacked.shape[2:]
    return pl.BlockSpec((None, nb) + rest, lambda bi, i: (slot, bi) + (0,) * len(rest))


def _stacked_out(shape, block, slot, n_slots):
    spec = pl.BlockSpec((None,) + block, lambda bi, i: (slot, bi) + (0,) * (len(block) - 1))
    return spec, jax.ShapeDtypeStruct((n_slots,) + shape, F32)


def _even_layer(x, hist, kc, vc, g_pre, g_post, w_in_bf, conv_w, bias2, w_out_bf, slot, n_slots, kv_stacks):
    b, t, d = x.shape
    w = w_in_bf.shape[1] // 8
    n_heads = w // HD_B
    has_cache = kc is not None
    assert t % CHUNK == 0 and (t <= ROW_TILE or t % ROW_TILE == 0)
    band_bytes = 2 * (BAND_PAST + min(t, ROW_TILE)) * w * 2
    state_bytes = band_bytes + (2 * 2 * BAND_PAST * w * 2 if has_cache else 0)
    nb, tt = _streams_per_step(b, t, state_bytes)
    n_rows = nb * tt
    consts = [g_pre.reshape(1, d), w_in_bf, conv_w, bias2, w_out_bf, g_post.reshape(1, d)]
    in_specs = [pl.BlockSpec((nb, tt, d), lambda bi, i: (bi, i, 0))] + [_resident(c.shape) for c in consts]
    args = [x] + consts
    if has_cache:
        in_specs += [_layer_slot_in(hist, nb, slot), _layer_slot_in(kc, nb, slot), _layer_slot_in(vc, nb, slot)]
        args += [hist, kc, vc]
    assert tt == min(BAND_PAST, t)
    kv_spec, kv_shape = _stacked_out((b, tt, w), (nb, tt, w), slot, n_slots)
    aliases = {}
    if kv_stacks is not None:
        aliases = {len(args): 2, len(args) + 1: 3}
        in_specs += [pl.BlockSpec(memory_space=pl.ANY)] * 2
        args += list(kv_stacks)
    x_new, utail, k_stack, v_stack = pl.pallas_call(
        functools.partial(_even_layer_kernel, nb=nb, tt=tt, has_cache=has_cache, n_alias=len(aliases)),
        grid=(b // nb, t // tt), in_specs=in_specs,
        out_specs=[pl.BlockSpec((nb, tt, d), lambda bi, i: (bi, i, 0)),
                   pl.BlockSpec((nb, SUBLANES, w), lambda bi, i: (bi, 0, 0)),
                   kv_spec, kv_spec],
        out_shape=[jax.ShapeDtypeStruct((b, t, d), F32),
                   jax.ShapeDtypeStruct((b, SUBLANES, w), F32),
                   kv_shape, kv_shape],
        input_output_aliases=aliases,
        scratch_shapes=[pltpu.VMEM((nb, SUBLANES, w), F32),
                        pltpu.VMEM((nb, BAND_PAST + tt, w), BF16), pltpu.VMEM((nb, BAND_PAST + tt, w), BF16),
                        pltpu.VMEM((n_rows, w), BF16), pltpu.VMEM((n_rows, w), BF16),
                        pltpu.VMEM((n_rows, 2 * w), BF16)],
        compiler_params=_params(), name="even_layer")(*args)
    del n_heads
    return x_new, utail[:, SUBLANES - 2:], (k_stack, v_stack)


def _cumsum_rows(x):
    n = x.shape[0]
    row = lax.broadcasted_iota(jnp.int32, x.shape, 0)
    shift = 1
    while shift < n:
        x = x + jnp.where(row >= shift, pltpu.roll(x, shift, axis=0), 0.0)
        shift *= 2
    return x


def _pair_masks():
    row = lax.broadcasted_iota(jnp.int32, (CHUNK, 2 * CHUNK), 0)
    lane = lax.broadcasted_iota(jnp.int32, (CHUNK, 2 * CHUNK), 1)
    first = lane < CHUNK
    col = jnp.where(first, lane, lane - CHUNK)
    return first, row, col


def _block_diag(pk, first):
    tiled = jnp.concatenate([pk, pk], axis=0).astype(BF16)
    keep = jnp.concatenate([first, jnp.logical_not(first)], axis=0)
    return jnp.where(keep, tiled, jnp.zeros_like(tiled))


def _unit_lower_inverses(mats, out, first, row, col):
    n = mats[0].shape[0]
    eye = jnp.where(row == col, 1.0, 0.0)
    xs = [-a for a in mats]
    ps = [eye + x for x in xs]
    ys = [jnp.dot(x.astype(BF16), _block_diag(x, first), preferred_element_type=F32) for x in xs]
    yield
    power = 2
    while 2 * power < n:
        sts = [jnp.dot(jnp.concatenate([p, y], axis=0).astype(BF16), _block_diag(y, first),
                       preferred_element_type=F32) for p, y in zip(ps, ys)]
        ps = [p + st[0:n] for p, st in zip(ps, sts)]
        ys = [st[n:2 * n] for st in sts]
        power *= 2
        yield
    out[:] = [p + jnp.dot(p.astype(BF16), _block_diag(y, first), preferred_element_type=F32)
              for p, y in zip(ps, ys)]
    yield


def _weave(main, fillers, period):
    filler = itertools.chain(*fillers)
    for n, _ in enumerate(main, start=1):
        if n % period == 0:
            next(filler, None)
    for _ in filler:
        pass


def _interleave(*generators):
    live = list(generators)
    while live:
        for g in list(live):
            try:
                next(g)
            except StopIteration:
                live.remove(g)


def _odd_layer_kernel(*refs, nb, tt, has_state, prep_chunks, n_alias):
    x_ref, gpre_ref, win_ref, wg_ref, cw_ref, ab_ref, on_ref, wout_ref, gpost_ref = refs[:9]
    if has_state:
        hist_ref, s0_ref = refs[9:11]
        refs = refs[11:]
    else:
        refs = refs[9:]
    refs = refs[n_alias:]
    (xo_ref, sout_ref, ctail_ref, ctails, qs, ks, vs, zs, gsc, bsc, sbuf, ybuf,
     wbuf, qtbuf, u0buf, qkbuf, kdtbuf, eglbuf) = refs
    w = win_ref.shape[-1] // 4
    n_heads = w // HD_C
    n_rows = nb * tt
    i = pl.program_id(1)

    @pl.when(i == 0)
    def _init():
        ctails[...] = jnp.zeros(ctails.shape, F32)
        if has_state:
            for s in range(nb):
                for g in range(3):
                    ctails[s, g, SUBLANES - 3:SUBLANES, :] = hist_ref[s, :, g * w:(g + 1) * w]
            sbuf[...] = s0_ref[...]
        else:
            sbuf[...] = jnp.zeros(sbuf.shape, F32)

    hb = _pre_norm(x_ref, gpre_ref, nb)

    gl = jnp.dot(hb, wg_ref[...], preferred_element_type=F32)
    bsc[...] = 1.0 / (1.0 + jnp.exp(-gl))
    gsc[...] = -jnp.exp(ab_ref[0:1, :]) * _softplus(gl + ab_ref[1:2, :])

    def project_group(g, dst, n_blocks):
        inv_scale_sq = float(HD_C) if g == 0 else 1.0
        sum_mat = jnp.full((HD_C, HD_C), inv_scale_sq, BF16)
        bw = w // n_blocks
        for blk in range(n_blocks):
            gcols = slice(g * w + blk * bw, g * w + (blk + 1) * bw)
            raw = jnp.dot(hb, win_ref[:, gcols], preferred_element_type=F32)
            for s in range(nb):
                srows = slice(s * tt, (s + 1) * tt)
                c, tail = _causal_dwconv(ctails[s, g, :, blk * bw:(blk + 1) * bw], raw[srows], cw_ref[:, gcols])
                c = _silu(c)
                ctail_ref[s, :, gcols] = tail
                ctails[s, g, :, blk * bw:(blk + 1) * bw] = tail
                if g == 2:
                    dst[srows, blk * bw:(blk + 1) * bw] = c
                else:
                    for h in range(bw // HD_C):
                        ch = c[:, h * HD_C:(h + 1) * HD_C]
                        ss = jnp.dot((ch * ch).astype(BF16), sum_mat, preferred_element_type=F32)
                        dst[srows, blk * bw + h * HD_C:blk * bw + (h + 1) * HD_C] = ch * lax.rsqrt(
                            ss + EPS * inv_scale_sq)
            yield

    def project_gate(n_blocks):
        bw = w // n_blocks
        for blk in range(n_blocks):
            z = jnp.dot(hb, win_ref[:, 3 * w + blk * bw:3 * w + (blk + 1) * bw], preferred_element_type=F32)
            zs[:, blk * bw:(blk + 1) * bw] = _silu(z).astype(zs.dtype)
            yield

    _interleave(project_group(0, qs, 1))
    _interleave(project_group(1, ks, 1))

    first, row, col = _pair_masks()
    incl = row >= col
    strict = row > col
    assert n_heads % 2 == 0 and 2 * CHUNK == HD_C
    pairs = range(n_heads // 2)

    heads = range(n_heads)
    hcols = [slice(h * HD_C, (h + 1) * HD_C) for h in heads]
    n_chunks = n_rows // CHUNK

    def chunk_rows(cidx):
        if isinstance(cidx, int):
            return slice(cidx * CHUNK, (cidx + 1) * CHUNK)
        return pl.ds(pl.multiple_of(cidx * CHUNK, CHUNK), CHUNK)

    def prepare_chunks(step):
        items = []
        duos = []
        for j in range(prep_chunks):
            cidx = step * prep_chunks + j
            rows = chunk_rows(cidx)
            gc = _cumsum_rows(gsc[rows, :])
            gct = gc.T
            bet = bsc[rows, :]
            for h in heads:
                gcol = jnp.broadcast_to(gc[:, n_heads + h:n_heads + h + 1], (CHUNK, HD_C))
                items.append(dict(cidx=cidx, rows=rows, h=h, gcol=gcol,
                                  b=jnp.broadcast_to(bet[:, h:h + 1], (CHUNK, HD_C))))
            for p in pairs:
                ia, ib = items[-n_heads + 2 * p], items[-n_heads + 2 * p + 1]
                g0 = n_heads + 2 * p
                grow = jnp.concatenate([gct[g0:g0 + 1, :], gct[g0 + 1:g0 + 2, :]], axis=1)
                gcol2 = jnp.where(first, ia["gcol"], ib["gcol"])
                duos.append(dict(cidx=cidx, rows=rows, p=p, a=ia, b=ib, beta=jnp.where(first, ia["b"], ib["b"]),
                                 decay=jnp.exp(jnp.where(incl, gcol2 - grow, -jnp.inf))))
        for it in items:
            rows, c = it["rows"], hcols[it["h"]]
            it["eg"] = jnp.exp(it["gcol"])
            it["q"], it["k"] = qs[rows, c], ks[rows, c]
        for it in items:
            kb = it["k"].astype(BF16)
            it["qk_kk"] = lax.dot_general(jnp.concatenate([it["q"].astype(BF16), kb], axis=0), kb,
                                          (((1,), (1,)), ((), ())), preferred_element_type=F32)
        yield
        for du in duos:
            qk_kk = jnp.concatenate([du["a"]["qk_kk"], du["b"]["qk_kk"]], axis=1)
            du["qk"] = qk_kk[0:CHUNK] * du["decay"]
            du["mat"] = jnp.where(strict, du["beta"] * qk_kk[CHUNK:2 * CHUNK] * du["decay"], 0.0)
        tinvs = [None] * len(duos)
        yield from _unit_lower_inverses([du["mat"] for du in duos], tinvs, first, row, col)
        sols = []
        for tinv, du in zip(tinvs, duos):
            rhs = jnp.concatenate(
                [jnp.concatenate([vs[it["rows"], hcols[it["h"]]] * it["b"], it["k"] * (it["b"] * it["eg"])], axis=1)
                 for it in (du["a"], du["b"])], axis=0)
            sols.append(jnp.dot(_block_diag(tinv, first), rhs.astype(BF16), preferred_element_type=F32))
        yield
        for sol, du in zip(sols, duos):
            rows = du["rows"]
            qkbuf[du["p"], rows, :] = du["qk"].astype(BF16)
            for half, it in enumerate((du["a"], du["b"])):
                gcol, c, h = it["gcol"], hcols[it["h"]], it["h"]
                g_last = gcol[CHUNK - 1:CHUNK, :]
                sol_h = sol[half * CHUNK:(half + 1) * CHUNK]
                u0buf[rows, c] = sol_h[:, 0:HD_C]
                wbuf[rows, c] = sol_h[:, HD_C:].astype(BF16)
                qtbuf[rows, c] = (it["q"] * it["eg"]).astype(BF16)
                kdtbuf[h, it["cidx"]] = (it["k"] * jnp.exp(g_last - gcol)).T.astype(BF16)
                eglbuf[it["cidx"], h:h + 1, :] = jnp.exp(g_last)

    chunks_per_stream = tt // CHUNK

    def recur_chunks(step):
        for j in range(prep_chunks):
            cidx = step * prep_chunks + j
            rows = chunk_rows(cidx)
            s = 0 if nb == 1 else cidx // chunks_per_stream
            s_old = [sbuf[s, h] for h in heads]
            ws_qs = [_dot(jnp.concatenate([wbuf[rows, hcols[h]], qtbuf[rows, hcols[h]]], axis=0), s_old[h])
                     for h in heads]
            yield
            u = [u0buf[rows, hcols[h]] - ws_qs[h][0:CHUNK] for h in heads]
            qku = [jnp.dot(_block_diag(qkbuf[p, rows, :], first),
                           jnp.concatenate([u[2 * p], u[2 * p + 1]], axis=0).astype(BF16),
                           preferred_element_type=F32) for p in pairs]
            o = [ws_qs[h][CHUNK:2 * CHUNK] + qku[h // 2][(h % 2) * CHUNK:(h % 2 + 1) * CHUNK] for h in heads]
            for h in heads:
                sbuf[s, h] = eglbuf[cidx, h:h + 1, :] * s_old[h] + _dot(kdtbuf[h, cidx], u[h])
            yield
            for h in heads:
                ms = jnp.mean(o[h] * o[h], axis=-1, keepdims=True)
                on = o[h] * lax.rsqrt(ms + EPS) * on_ref[...]
                ybuf[rows, hcols[h]] = (on * zs[rows, hcols[h]].astype(F32)).astype(ybuf.dtype)

    n_steps = n_chunks // prep_chunks
    _weave(prepare_chunks(0), [project_group(2, vs, PROJ_BLOCKS), project_gate(PROJ_BLOCKS)], period=1)

    def pipelined(step, carry):
        _interleave(prepare_chunks(step), recur_chunks(step - 1))
        return carry

    lax.fori_loop(1, n_steps, pipelined, 0)
    _interleave(recur_chunks(n_steps - 1))
    sout_ref[...] = sbuf[...]
    _interleave(_project_out(ybuf, wout_ref, gpost_ref, x_ref, xo_ref, tt, 0, n_rows))


def _odd_layer(x, hist, s0, g_pre, g_post, w_in_bf, w_gate_bf, conv_w, ab, onorm, w_out_bf, slot, n_slots, s_stack):
    b, t, d = x.shape
    w = w_in_bf.shape[1] // 4
    n_heads = w // HD_C
    has_state = s0 is not None
    assert t % CHUNK == 0 and (t <= ROW_TILE or t % ROW_TILE == 0)
    state_bytes = n_heads * HD_C * HD_C * 4 * (5 if has_state else 3)
    nb, tt = _streams_per_step(b, t, state_bytes)
    n_rows = nb * tt
    n_chunks = n_rows // CHUNK
    prep_chunks = max(c for c in (1, 2, 4) if n_chunks % c == 0)
    consts = [g_pre.reshape(1, d), w_in_bf, w_gate_bf, conv_w, ab, onorm, w_out_bf, g_post.reshape(1, d)]
    in_specs = [pl.BlockSpec((nb, tt, d), lambda bi, i: (bi, i, 0))] + [_resident(c.shape) for c in consts]
    args = [x] + consts
    if has_state:
        in_specs += [_layer_slot_in(hist, nb, slot), _layer_slot_in(s0, nb, slot)]
        args += [hist, s0]
    s_spec, s_shape = _stacked_out((b, n_heads, HD_C, HD_C), (nb, n_heads, HD_C, HD_C), slot, n_slots)
    aliases = {}
    if s_stack is not None:
        aliases = {len(args): 1}
        in_specs.append(pl.BlockSpec(memory_space=pl.ANY))
        args.append(s_stack)
    x_new, s_new, ctail = pl.pallas_call(
        functools.partial(_odd_layer_kernel, nb=nb, tt=tt, has_state=has_state, prep_chunks=prep_chunks,
                          n_alias=len(aliases)),
        grid=(b // nb, t // tt), in_specs=in_specs,
        out_specs=[pl.BlockSpec((nb, tt, d), lambda bi, i: (bi, i, 0)),
                   s_spec,
                   pl.BlockSpec((nb, SUBLANES, 3 * w), lambda bi, i: (bi, 0, 0))],
        out_shape=[jax.ShapeDtypeStruct((b, t, d), F32),
                   s_shape,
                   jax.ShapeDtypeStruct((b, SUBLANES, 3 * w), F32)],
        input_output_aliases=aliases,
        scratch_shapes=[pltpu.VMEM((nb, 3, SUBLANES, w), F32),
                        pltpu.VMEM((n_rows, w), F32), pltpu.VMEM((n_rows, w), F32), pltpu.VMEM((n_rows, w), F32),
                        pltpu.VMEM((n_rows, w), BF16),
                        pltpu.VMEM((n_rows, LANES), F32), pltpu.VMEM((n_rows, LANES), F32),
                        pltpu.VMEM((nb, n_heads, HD_C, HD_C), F32),
                        pltpu.VMEM((n_rows, w), BF16),
                        pltpu.VMEM((n_rows, w), BF16), pltpu.VMEM((n_rows, w), BF16), pltpu.VMEM((n_rows, w), F32),
                        pltpu.VMEM((n_heads // 2, n_rows, 2 * CHUNK), BF16),
                        pltpu.VMEM((n_heads, n_chunks, HD_C, CHUNK), BF16),
                        pltpu.VMEM((n_chunks, n_heads, HD_C), F32)],
        compiler_params=_params(), name="odd_layer")(*args)
    return x_new, ctail[:, SUBLANES - 3:], s_new


def _cast_kernel(w_ref, *rest):
    if len(rest) == 2:
        scale_ref, o_ref = rest
        o_ref[...] = (w_ref[...] * scale_ref[...]).astype(o_ref.dtype)
    else:
        (o_ref,) = rest
        o_ref[...] = w_ref[...].astype(o_ref.dtype)


def _to_bf16(w_all, layer, n_cols, col_scale=None):
    k = w_all.shape[1]
    assert n_cols % CAST_COLS == 0
    in_specs = [pl.BlockSpec((None, k, CAST_COLS), lambda j: (layer, 0, j))]
    args = [w_all]
    if col_scale is not None:
        in_specs.append(pl.BlockSpec((1, CAST_COLS), lambda j: (0, j)))
        args.append(col_scale.reshape(1, n_cols))
    return pl.pallas_call(
        _cast_kernel, grid=(n_cols // CAST_COLS,), in_specs=in_specs,
        out_specs=pl.BlockSpec((k, CAST_COLS), lambda j: (0, j)),
        out_shape=jax.ShapeDtypeStruct((k, n_cols), BF16),
        compiler_params=pltpu.CompilerParams(dimension_semantics=("parallel",)), name="to_bf16")(*args)


def _gate_cast_kernel(w_ref, o_ref, *, n_gate):
    lane = lax.broadcasted_iota(jnp.int32, w_ref.shape, 1)
    o_ref[...] = jnp.where(lane < n_gate, w_ref[...], 0.0).astype(o_ref.dtype)


def _gate_to_bf16(w_all, layer, first_col, n_gate):
    k = w_all.shape[1]
    assert first_col % LANES == 0 and n_gate <= LANES and first_col + n_gate == w_all.shape[2]
    return pl.pallas_call(
        functools.partial(_gate_cast_kernel, n_gate=n_gate), grid=(1,),
        in_specs=[pl.BlockSpec((None, k, LANES), lambda j: (layer, 0, first_col // LANES))],
        out_specs=pl.BlockSpec((k, LANES), lambda j: (0, 0)),
        out_shape=jax.ShapeDtypeStruct((k, LANES), BF16), name="gate_to_bf16")(w_all)


def _band_bias(table):
    n_heads = table.shape[0]
    assert CHUNK - 1 <= REL_CLIP < BAND - 1
    low = table[:, REL_CLIP - (CHUNK - 1):]
    high = jnp.broadcast_to(table[:, 2 * REL_CLIP:], (n_heads, BAND - 1 - REL_CLIP))
    r = jnp.concatenate([low, high], axis=1)[:, ::-1].astype(F32) * LOG2E
    length = r.shape[1]
    padded = jnp.pad(r, ((0, 0), (0, 1)))
    shifted = jnp.tile(padded, (1, CHUNK))[:, :CHUNK * length].reshape(n_heads, CHUNK, length)
    bias = shifted[:, :, CHUNK - 1:CHUNK - 1 + BAND]
    return bias.reshape(n_heads // 2, 2 * CHUNK, BAND)


def kernel(x_prompt, x_sample, cache_conv_a, cache_k_b, cache_v_b, state_conv_c, state_s_c, norm_pre, norm_post, w_in_even, conv_w_a, rel_bias_b, w_out_even, w_in_odd, conv_w_c, a_log_c, dt_bias_c, out_norm_c, w_out_odd):
    depth = norm_pre.shape[0]
    n_even, n_odd = w_in_even.shape[0], w_in_odd.shape[0]
    n_heads_b = cache_k_b.shape[-2]
    kc_all = cache_k_b.reshape(cache_k_b.shape[:3] + (-1,)).astype(BF16)
    vc_all = cache_v_b.reshape(cache_v_b.shape[:3] + (-1,)).astype(BF16)
    yp, ys = x_prompt, x_sample
    conv_a_p, conv_a_s, conv_c_p, conv_c_s = [], [], [], []
    kv_p = kv_s = s_p = s_s = None
    for layer in range(depth):
        i = layer // 2
        g_pre, g_post = norm_pre[layer], norm_post[layer]
        if layer % 2 == 0:
            n_in = w_in_even.shape[2]
            w_blk = n_in // 8
            col_scale = jnp.ones((n_in,), F32).at[4 * w_blk:5 * w_blk].set(HD_B ** -0.5 * LOG2E)
            w_in_bf = _to_bf16(w_in_even, i, n_in, col_scale)
            w_out_bf = _to_bf16(w_out_even, i, w_out_even.shape[2])
            shared = (g_pre, g_post, w_in_bf, conv_w_a[i], _band_bias(rel_bias_b[i]), w_out_bf, i, n_even)
            yp, hst, kv_p = _even_layer(yp, None, None, None, *shared, kv_p)
            conv_a_p.append(hst)
            ys, hst, kv_s = _even_layer(ys, cache_conv_a, kc_all, vc_all, *shared, kv_s)
            conv_a_s.append(hst)
        else:
            n_heads = a_log_c.shape[1]
            n_main = w_in_odd.shape[2] - 2 * n_heads
            w_main_bf = _to_bf16(w_in_odd, i, n_main)
            w_gate_bf = _gate_to_bf16(w_in_odd, i, n_main, 2 * n_heads)
            pad = (n_heads, LANES - 2 * n_heads)
            ab = jnp.stack([jnp.pad(a_log_c[i], pad), jnp.pad(dt_bias_c[i], pad)]).astype(F32)
            shared = (g_pre, g_post, w_main_bf, w_gate_bf, conv_w_c[i], ab, out_norm_c[i].reshape(1, -1),
                      _to_bf16(w_out_odd, i, w_out_odd.shape[2]), i, n_odd)
            yp, hst, s_p = _odd_layer(yp, None, None, *shared, s_p)
            conv_c_p.append(hst)
            ys, hst, s_s = _odd_layer(ys, state_conv_c, state_s_c, *shared, s_s)
            conv_c_s.append(hst)

    def heads_apart(kv):
        return kv.reshape(kv.shape[:-1] + (n_heads_b, kv.shape[-1] // n_heads_b))

    return (yp, ys,
            jnp.stack(conv_a_p), heads_apart(kv_p[0]), heads_apart(kv_p[1]), jnp.stack(conv_c_p), s_p,
            jnp.stack(conv_a_s), heads_apart(kv_s[0]), heads_apart(kv_s[1]), jnp.stack(conv_c_s), s_s)
```

```python
import functools
import itertools
import math

import jax
import jax.numpy as jnp
from jax import lax
from jax.experimental import pallas as pl
from jax.experimental.pallas import tpu as pltpu

F32 = jnp.float32
BF16 = jnp.bfloat16

EPS = 1e-6
LOG2E = math.log2(math.e)
CHUNK = 64
BAND_PAST = 8 * CHUNK
BAND = BAND_PAST + CHUNK
REL_CLIP = 4 * CHUNK
HD_B = 64
HD_C = 128
LANES = 128
SUBLANES = 8
VMEM_LIMIT_BYTES = 56 * 1024 * 1024
STATE_VMEM_BUDGET_BYTES = 16 * 1024 * 1024
ROW_TILE = 512
CAST_COLS = 512
PROJ_BLOCKS = 4


def _silu(x):
    return x * (1.0 / (1.0 + jnp.exp(-x)))


def _softplus(x):
    return jnp.maximum(x, 0.0) + jnp.log1p(jnp.exp(-jnp.abs(x)))


def _dot(a, b):
    return jnp.dot(a.astype(BF16), b.astype(BF16), preferred_element_type=F32)


def _load_rows(ref, nb):
    parts = [ref[s] for s in range(nb)]
    return parts[0] if nb == 1 else jnp.concatenate(parts, axis=0)


def _pre_norm(x_ref, gain_ref, nb):
    x = _load_rows(x_ref, nb)
    ms = jnp.mean(x * x, axis=-1, keepdims=True)
    return (x * lax.rsqrt(ms + EPS) * gain_ref[...]).astype(BF16)


def _project_out(ybuf, wout_ref, gain_ref, x_ref, xo_ref, tt, r0, r1):
    out = jnp.dot(ybuf[r0:r1, :], wout_ref[...], preferred_element_type=F32)
    ms = jnp.mean(out * out, axis=-1, keepdims=True)
    delta = out * lax.rsqrt(ms + EPS) * gain_ref[...]
    for s in range(x_ref.shape[0]):
        lo, hi = max(r0, s * tt), min(r1, (s + 1) * tt)
        if lo < hi:
            frames = slice(lo - s * tt, hi - s * tt)
            xo_ref[s, frames, :] = x_ref[s, frames, :] + delta[lo - r0:hi - r0]
    yield


def _causal_dwconv(prev_rows, new_rows, taps):
    tt = new_rows.shape[0]
    width = taps.shape[0]
    assert width <= SUBLANES
    full = jnp.concatenate([prev_rows, new_rows], axis=0)
    sums = [None, None]
    shifted = full
    for delay in range(width):
        if delay and delay % 2 == 0:
            shifted = pltpu.roll(shifted, 2, axis=0)
        term = shifted * taps[width - 1 - delay:width - delay]
        sums[delay % 2] = term if sums[delay % 2] is None else sums[delay % 2] + term
    acc = sums[0] if sums[1] is None else sums[0] + pltpu.roll(sums[1], 1, axis=0)
    return acc[SUBLANES:, :], full[tt:tt + SUBLANES, :]


def _streams_per_step(b, t, state_bytes_per_stream):
    tt = min(t, ROW_TILE)
    nb = max(1, min(ROW_TILE // tt, STATE_VMEM_BUDGET_BYTES // state_bytes_per_stream))
    while b % nb:
        nb -= 1
    return nb, tt


def _resident(shape):
    return pl.BlockSpec(shape, lambda bi, i: (0,) * len(shape), pipeline_mode=pl.Buffered(1))


def _params():
    return pltpu.CompilerParams(dimension_semantics=("parallel", "arbitrary"), vmem_limit_bytes=VMEM_LIMIT_BYTES)


def _even_layer_kernel(*refs, nb, tt, has_cache, n_alias):
    x_ref, gpre_ref, win_ref, cw_ref, bias_ref, wout_ref, gpost_ref = refs[:7]
    if has_cache:
        hist_ref, kc_ref, vc_ref = refs[7:10]
        refs = refs[10:]
    else:
        refs = refs[7:]
    refs = refs[n_alias:]
    xo_ref, utail_ref, knew_ref, vnew_ref, utails, kbuf, vbuf, qbuf, zbuf, ybuf = refs
    w = win_ref.shape[-1] // 8
    i = pl.program_id(1)

    @pl.when(i == 0)
    def _init():
        utails[...] = jnp.zeros(utails.shape, F32)
        if has_cache:
            for s in range(nb):
                utails[s, SUBLANES - 2:SUBLANES, :] = hist_ref[s]
            kbuf[:, 0:BAND_PAST, :] = kc_ref[...]
            vbuf[:, 0:BAND_PAST, :] = vc_ref[...]
        else:
            kbuf[:, 0:BAND_PAST, :] = jnp.zeros((nb, BAND_PAST, w), BF16)
            vbuf[:, 0:BAND_PAST, :] = jnp.zeros((nb, BAND_PAST, w), BF16)

    hb = _pre_norm(x_ref, gpre_ref, nb)

    def proj(j):
        return jnp.dot(hb, win_ref[:, j * w:(j + 1) * w], preferred_element_type=F32)

    u = proj(1) * proj(2)
    gate = proj(0) * _silu(proj(3))
    for s in range(nb):
        srows = slice(s * tt, (s + 1) * tt)
        conv, tail = _causal_dwconv(utails[s], u[srows], cw_ref[...])
        utail_ref[s] = tail
        utails[s] = tail
        ybuf[srows, 0:w] = (gate[srows] * conv).astype(ybuf.dtype)

    qbuf[...] = proj(4).astype(qbuf.dtype)
    k = proj(5)
    v = proj(6)
    for s in range(nb):
        srows = slice(s * tt, (s + 1) * tt)
        knew_ref[s] = k[srows]
        vnew_ref[s] = v[srows]
        kbuf[s, BAND_PAST:BAND_PAST + tt, :] = k[srows].astype(kbuf.dtype)
        vbuf[s, BAND_PAST:BAND_PAST + tt, :] = v[srows].astype(vbuf.dtype)
    zbuf[...] = _silu(proj(7)).astype(zbuf.dtype)

    chunks_per_stream = tt // CHUNK
    n_pairs = w // LANES
    lane = lax.broadcasted_iota(jnp.int32, (CHUNK, LANES), 1)
    first_head = lane < HD_B

    def attend(first_tile):
        def band_start(c):
            missing = max(BAND_PAST - CHUNK * c, 0) if first_tile else 0
            return missing // LANES * LANES, missing % LANES

        def scores(s, c, hp):
            cols = slice(hp * LANES, (hp + 1) * LANES)
            lo, masked = band_start(c)
            r0 = s * tt + c * CHUNK
            qp = qbuf[r0:r0 + CHUNK, cols]
            zero = jnp.zeros_like(qp)
            q2 = jnp.concatenate([jnp.where(first_head, qp, zero), jnp.where(first_head, zero, qp)], axis=0)
            kp = kbuf[s, c * CHUNK + lo:c * CHUNK + BAND, cols]
            sc = lax.dot_general(q2, kp, (((1,), (1,)), ((), ())), preferred_element_type=F32)
            sc = sc + bias_ref[hp, :, lo:BAND]
            if masked:
                key_idx = lax.broadcasted_iota(jnp.int32, sc.shape, 1)
                sc = jnp.where(key_idx >= masked, sc, -jnp.inf)
            return sc

        def weighted_values(s, c, hp, sc):
            cols = slice(hp * LANES, (hp + 1) * LANES)
            lo, _ = band_start(c)
            m = jnp.max(sc, axis=-1, keepdims=True)
            e = jnp.exp2(sc - m)
            l = jnp.sum(e, axis=-1, keepdims=True)
            vp = vbuf[s, c * CHUNK + lo:c * CHUNK + BAND, cols]
            pv = jnp.dot(e.astype(BF16), vp, preferred_element_type=F32) / l
            return jnp.where(first_head, pv[0:CHUNK], pv[CHUNK:2 * CHUNK])

        items = [(s, c, hp) for s in range(nb) for c in range(chunks_per_stream) for hp in range(n_pairs)]
        skew = 3
        pending, outs = {}, {}
        for step in range(len(items) + skew):
            if step < len(items):
                pending[step] = scores(*items[step])
            j = step - skew
            if j >= 0:
                s, c, hp = items[j]
                outs[hp] = weighted_values(s, c, hp, pending.pop(j))
                if hp == n_pairs - 1:
                    rows = slice(s * tt + c * CHUNK, s * tt + (c + 1) * CHUNK)
                    o = jnp.concatenate([outs[x] for x in range(n_pairs)], axis=1)
                    ybuf[rows, w:2 * w] = (o * zbuf[rows, :].astype(F32)).astype(ybuf.dtype)

    if has_cache:
        attend(False)
    else:
        pl.when(i == 0)(functools.partial(attend, True))
        pl.when(i > 0)(functools.partial(attend, False))
    for s in range(nb):
        kbuf[s, 0:BAND_PAST, :] = kbuf[s, tt:tt + BAND_PAST, :]
        vbuf[s, 0:BAND_PAST, :] = vbuf[s, tt:tt + BAND_PAST, :]

    _interleave(_project_out(ybuf, wout_ref, gpost_ref, x_ref, xo_ref, tt, 0, nb * tt))


def _layer_slot_in(stacked, nb, slot):
    rest = stacked.shape[2:]
    return pl.BlockSpec((None, nb) + rest, lambda bi, i: (slot, bi) + (0,) * len(rest))


def _stacked_out(shape, block, slot, n_slots):
    spec = pl.BlockSpec((None,) + block, lambda bi, i: (slot, bi) + (0,) * (len(block) - 1))
    return spec, jax.ShapeDtypeStruct((n_slots,) + shape, F32)


def _even_layer(x, hist, kc, vc, g_pre, g_post, w_in_bf, conv_w, bias2, w_out_bf, slot, n_slots, kv_stacks):
    b, t, d = x.shape
    w = w_in_bf.shape[1] // 8
    n_heads = w // HD_B
    has_cache = kc is not None
    assert t % CHUNK == 0 and (t <= ROW_TILE or t % ROW_TILE == 0)
    band_bytes = 2 * (BAND_PAST + min(t, ROW_TILE)) * w * 2
    state_bytes = band_bytes + (2 * 2 * BAND_PAST * w * 2 if has_cache else 0)
    nb, tt = _streams_per_step(b, t, state_bytes)
    n_rows = nb * tt
    consts = [g_pre.reshape(1, d), w_in_bf, conv_w, bias2, w_out_bf, g_post.reshape(1, d)]
    in_specs = [pl.BlockSpec((nb, tt, d), lambda bi, i: (bi, i, 0))] + [_resident(c.shape) for c in consts]
    args = [x] + consts
    if has_cache:
        in_specs += [_layer_slot_in(hist, nb, slot), _layer_slot_in(kc, nb, slot), _layer_slot_in(vc, nb, slot)]
        args += [hist, kc, vc]
    assert tt == min(BAND_PAST, t)
    kv_spec, kv_shape = _stacked_out((b, tt, w), (nb, tt, w), slot, n_slots)
    if kv_stacks is None:
        kv_stacks = (jnp.zeros(kv_shape.shape, F32), jnp.zeros(kv_shape.shape, F32))
    aliases = {len(args): 2, len(args) + 1: 3}
    in_specs += [pl.BlockSpec(memory_space=pl.ANY)] * 2
    args += list(kv_stacks)
    x_new, utail, k_stack, v_stack = pl.pallas_call(
        functools.partial(_even_layer_kernel, nb=nb, tt=tt, has_cache=has_cache, n_alias=len(aliases)),
        grid=(b // nb, t // tt), in_specs=in_specs,
        out_specs=[pl.BlockSpec((nb, tt, d), lambda bi, i: (bi, i, 0)),
                   pl.BlockSpec((nb, SUBLANES, w), lambda bi, i: (bi, 0, 0)),
                   kv_spec, kv_spec],
        out_shape=[jax.ShapeDtypeStruct((b, t, d), F32),
                   jax.ShapeDtypeStruct((b, SUBLANES, w), F32),
                   kv_shape, kv_shape],
        input_output_aliases=aliases,
        scratch_shapes=[pltpu.VMEM((nb, SUBLANES, w), F32),
                        pltpu.VMEM((nb, BAND_PAST + tt, w), BF16), pltpu.VMEM((nb, BAND_PAST + tt, w), BF16),
                        pltpu.VMEM((n_rows, w), BF16), pltpu.VMEM((n_rows, w), BF16),
                        pltpu.VMEM((n_rows, 2 * w), BF16)],
        compiler_params=_params(), name="even_layer")(*args)
    del n_heads
    return x_new, utail[:, SUBLANES - 2:], (k_stack, v_stack)


def _cumsum_rows(x):
    n = x.shape[0]
    row = lax.broadcasted_iota(jnp.int32, x.shape, 0)
    shift = 1
    while shift < n:
        x = x + jnp.where(row >= shift, pltpu.roll(x, shift, axis=0), 0.0)
        shift *= 2
    return x


def _pair_masks():
    row = lax.broadcasted_iota(jnp.int32, (CHUNK, 2 * CHUNK), 0)
    lane = lax.broadcasted_iota(jnp.int32, (CHUNK, 2 * CHUNK), 1)
    first = lane < CHUNK
    col = jnp.where(first, lane, lane - CHUNK)
    return first, row, col


def _block_diag(pk, first):
    tiled = jnp.concatenate([pk, pk], axis=0).astype(BF16)
    keep = jnp.concatenate([first, jnp.logical_not(first)], axis=0)
    return jnp.where(keep, tiled, jnp.zeros_like(tiled))


def _unit_lower_inverses(mats, out, first, row, col):
    n = mats[0].shape[0]
    eye = jnp.where(row == col, 1.0, 0.0)
    xs = [-a for a in mats]
    ps = [eye + x for x in xs]
    ys = [jnp.dot(x.astype(BF16), _block_diag(x, first), preferred_element_type=F32) for x in xs]
    yield
    power = 2
    while 2 * power < n:
        sts = [jnp.dot(jnp.concatenate([p, y], axis=0).astype(BF16), _block_diag(y, first),
                       preferred_element_type=F32) for p, y in zip(ps, ys)]
        ps = [p + st[0:n] for p, st in zip(ps, sts)]
        ys = [st[n:2 * n] for st in sts]
        power *= 2
        yield
    out[:] = [p + jnp.dot(p.astype(BF16), _block_diag(y, first), preferred_element_type=F32)
              for p, y in zip(ps, ys)]
    yield


def _weave(main, fillers, period):
    filler = itertools.chain(*fillers)
    for n, _ in enumerate(main, start=1):
        if n % period == 0:
            next(filler, None)
    for _ in filler:
        pass


def _interleave(*generators):
    live = list(generators)
    while live:
        for g in list(live):
            try:
                next(g)
            except StopIteration:
                live.remove(g)


def _odd_layer_kernel(*refs, nb, tt, has_state, prep_chunks, n_alias):
    x_ref, gpre_ref, win_ref, wg_ref, cw_ref, ab_ref, on_ref, wout_ref, gpost_ref = refs[:9]
    if has_state:
        hist_ref, s0_ref = refs[9:11]
        refs = refs[11:]
    else:
        refs = refs[9:]
    refs = refs[n_alias:]
    (xo_ref, sout_ref, ctail_ref, ctails, qs, ks, vs, zs, gsc, bsc, sbuf, ybuf,
     wbuf, qtbuf, u0buf, qkbuf, kdtbuf, eglbuf) = refs
    w = win_ref.shape[-1] // 4
    n_heads = w // HD_C
    n_rows = nb * tt
    i = pl.program_id(1)

    @pl.when(i == 0)
    def _init():
        ctails[...] = jnp.zeros(ctails.shape, F32)
        if has_state:
            for s in range(nb):
                for g in range(3):
                    ctails[s, g, SUBLANES - 3:SUBLANES, :] = hist_ref[s, :, g * w:(g + 1) * w]
            sbuf[...] = s0_ref[...]
        else:
            sbuf[...] = jnp.zeros(sbuf.shape, F32)

    hb = _pre_norm(x_ref, gpre_ref, nb)

    gl = jnp.dot(hb, wg_ref[...], preferred_element_type=F32)
    bsc[...] = 1.0 / (1.0 + jnp.exp(-gl))
    gsc[...] = -jnp.exp(ab_ref[0:1, :]) * _softplus(gl + ab_ref[1:2, :])

    def project_group(g, dst, n_blocks):
        inv_scale_sq = float(HD_C) if g == 0 else 1.0
        sum_mat = jnp.full((HD_C, HD_C), inv_scale_sq, BF16)
        bw = w // n_blocks
        for blk in range(n_blocks):
            gcols = slice(g * w + blk * bw, g * w + (blk + 1) * bw)
            raw = jnp.dot(hb, win_ref[:, gcols], preferred_element_type=F32)
            for s in range(nb):
                srows = slice(s * tt, (s + 1) * tt)
                c, tail = _causal_dwconv(ctails[s, g, :, blk * bw:(blk + 1) * bw], raw[srows], cw_ref[:, gcols])
                c = _silu(c)
                ctail_ref[s, :, gcols] = tail
                ctails[s, g, :, blk * bw:(blk + 1) * bw] = tail
                if g == 2:
                    dst[srows, blk * bw:(blk + 1) * bw] = c
                else:
                    for h in range(bw // HD_C):
                        ch = c[:, h * HD_C:(h + 1) * HD_C]
                        ss = jnp.dot((ch * ch).astype(BF16), sum_mat, preferred_element_type=F32)
                        dst[srows, blk * bw + h * HD_C:blk * bw + (h + 1) * HD_C] = ch * lax.rsqrt(
                            ss + EPS * inv_scale_sq)
            yield

    def project_gate(n_blocks):
        bw = w // n_blocks
        for blk in range(n_blocks):
            z = jnp.dot(hb, win_ref[:, 3 * w + blk * bw:3 * w + (blk + 1) * bw], preferred_element_type=F32)
            zs[:, blk * bw:(blk + 1) * bw] = _silu(z).astype(zs.dtype)
            yield

    _interleave(project_group(0, qs, 1))
    _interleave(project_group(1, ks, 1))

    first, row, col = _pair_masks()
    incl = row >= col
    strict = row > col
    assert n_heads % 2 == 0 and 2 * CHUNK == HD_C
    pairs = range(n_heads // 2)

    heads = range(n_heads)
    hcols = [slice(h * HD_C, (h + 1) * HD_C) for h in heads]
    n_chunks = n_rows // CHUNK

    def chunk_rows(cidx):
        if isinstance(cidx, int):
            return slice(cidx * CHUNK, (cidx + 1) * CHUNK)
        return pl.ds(pl.multiple_of(cidx * CHUNK, CHUNK), CHUNK)

    def prepare_chunks(step):
        items = []
        duos = []
        for j in range(prep_chunks):
            cidx = step * prep_chunks + j
            rows = chunk_rows(cidx)
            gc = _cumsum_rows(gsc[rows, :])
            gct = gc.T
            bet = bsc[rows, :]
            for h in heads:
                gcol = jnp.broadcast_to(gc[:, n_heads + h:n_heads + h + 1], (CHUNK, HD_C))
                items.append(dict(cidx=cidx, rows=rows, h=h, gcol=gcol,
                                  b=jnp.broadcast_to(bet[:, h:h + 1], (CHUNK, HD_C))))
            for p in pairs:
                ia, ib = items[-n_heads + 2 * p], items[-n_heads + 2 * p + 1]
                g0 = n_heads + 2 * p
                grow = jnp.concatenate([gct[g0:g0 + 1, :], gct[g0 + 1:g0 + 2, :]], axis=1)
                gcol2 = jnp.where(first, ia["gcol"], ib["gcol"])
                duos.append(dict(cidx=cidx, rows=rows, p=p, a=ia, b=ib, beta=jnp.where(first, ia["b"], ib["b"]),
                                 decay=jnp.exp(jnp.where(incl, gcol2 - grow, -jnp.inf))))
        for it in items:
            rows, c = it["rows"], hcols[it["h"]]
            it["eg"] = jnp.exp(it["gcol"])
            it["q"], it["k"] = qs[rows, c], ks[rows, c]
        for it in items:
            kb = it["k"].astype(BF16)
            it["qk_kk"] = lax.dot_general(jnp.concatenate([it["q"].astype(BF16), kb], axis=0), kb,
                                          (((1,), (1,)), ((), ())), preferred_element_type=F32)
        yield
        for du in duos:
            qk_kk = jnp.concatenate([du["a"]["qk_kk"], du["b"]["qk_kk"]], axis=1)
            du["qk"] = qk_kk[0:CHUNK] * du["decay"]
            du["mat"] = jnp.where(strict, du["beta"] * qk_kk[CHUNK:2 * CHUNK] * du["decay"], 0.0)
        tinvs = [None] * len(duos)
        yield from _unit_lower_inverses([du["mat"] for du in duos], tinvs, first, row, col)
        sols = []
        for tinv, du in zip(tinvs, duos):
            rhs = jnp.concatenate(
                [jnp.concatenate([vs[it["rows"], hcols[it["h"]]] * it["b"], it["k"] * (it["b"] * it["eg"])], axis=1)
                 for it in (du["a"], du["b"])], axis=0)
            sols.append(jnp.dot(_block_diag(tinv, first), rhs.astype(BF16), preferred_element_type=F32))
        yield
        for sol, du in zip(sols, duos):
            rows = du["rows"]
            qkbuf[du["p"], rows, :] = du["qk"].astype(BF16)
            for half, it in enumerate((du["a"], du["b"])):
                gcol, c, h = it["gcol"], hcols[it["h"]], it["h"]
                g_last = gcol[CHUNK - 1:CHUNK, :]
                sol_h = sol[half * CHUNK:(half + 1) * CHUNK]
                u0buf[rows, c] = sol_h[:, 0:HD_C]
                wbuf[rows, c] = sol_h[:, HD_C:].astype(BF16)
                qtbuf[rows, c] = (it["q"] * it["eg"]).astype(BF16)
                kdtbuf[h, it["cidx"]] = (it["k"] * jnp.exp(g_last - gcol)).T.astype(BF16)
                eglbuf[it["cidx"], h:h + 1, :] = jnp.exp(g_last)

    chunks_per_stream = tt // CHUNK

    def recur_chunks(step):
        for j in range(prep_chunks):
            cidx = step * prep_chunks + j
            rows = chunk_rows(cidx)
            s = 0 if nb == 1 else cidx // chunks_per_stream
            s_old = [sbuf[s, h] for h in heads]
            ws_qs = [_dot(jnp.concatenate([wbuf[rows, hcols[h]], qtbuf[rows, hcols[h]]], axis=0), s_old[h])
                     for h in heads]
            yield
            u = [u0buf[rows, hcols[h]] - ws_qs[h][0:CHUNK] for h in heads]
            qku = [jnp.dot(_block_diag(qkbuf[p, rows, :], first),
                           jnp.concatenate([u[2 * p], u[2 * p + 1]], axis=0).astype(BF16),
                           preferred_element_type=F32) for p in pairs]
            o = [ws_qs[h][CHUNK:2 * CHUNK] + qku[h // 2][(h % 2) * CHUNK:(h % 2 + 1) * CHUNK] for h in heads]
            for h in heads:
                sbuf[s, h] = eglbuf[cidx, h:h + 1, :] * s_old[h] + _dot(kdtbuf[h, cidx], u[h])
            yield
            for h in heads:
                ms = jnp.mean(o[h] * o[h], axis=-1, keepdims=True)
                on = o[h] * lax.rsqrt(ms + EPS) * on_ref[...]
                ybuf[rows, hcols[h]] = (on * zs[rows, hcols[h]].astype(F32)).astype(ybuf.dtype)

    n_steps = n_chunks // prep_chunks
    _weave(prepare_chunks(0), [project_group(2, vs, PROJ_BLOCKS), project_gate(PROJ_BLOCKS)], period=1)

    def pipelined(step, carry):
        _interleave(prepare_chunks(step), recur_chunks(step - 1))
        return carry

    lax.fori_loop(1, n_steps, pipelined, 0)
    _interleave(recur_chunks(n_steps - 1))
    sout_ref[...] = sbuf[...]
    _interleave(_project_out(ybuf, wout_ref, gpost_ref, x_ref, xo_ref, tt, 0, n_rows))


def _odd_layer(x, hist, s0, g_pre, g_post, w_in_bf, w_gate_bf, conv_w, ab, onorm, w_out_bf, slot, n_slots, s_stack):
    b, t, d = x.shape
    w = w_in_bf.shape[1] // 4
    n_heads = w // HD_C
    has_state = s0 is not None
    assert t % CHUNK == 0 and (t <= ROW_TILE or t % ROW_TILE == 0)
    state_bytes = n_heads * HD_C * HD_C * 4 * (5 if has_state else 3)
    nb, tt = _streams_per_step(b, t, state_bytes)
    n_rows = nb * tt
    n_chunks = n_rows // CHUNK
    prep_chunks = max(c for c in (1, 2, 4) if n_chunks % c == 0)
    consts = [g_pre.reshape(1, d), w_in_bf, w_gate_bf, conv_w, ab, onorm, w_out_bf, g_post.reshape(1, d)]
    in_specs = [pl.BlockSpec((nb, tt, d), lambda bi, i: (bi, i, 0))] + [_resident(c.shape) for c in consts]
    args = [x] + consts
    if has_state:
        in_specs += [_layer_slot_in(hist, nb, slot), _layer_slot_in(s0, nb, slot)]
        args += [hist, s0]
    s_spec, s_shape = _stacked_out((b, n_heads, HD_C, HD_C), (nb, n_heads, HD_C, HD_C), slot, n_slots)
    if s_stack is None:
        s_stack = jnp.zeros(s_shape.shape, F32)
    aliases = {len(args): 1}
    in_specs.append(pl.BlockSpec(memory_space=pl.ANY))
    args.append(s_stack)
    x_new, s_new, ctail = pl.pallas_call(
        functools.partial(_odd_layer_kernel, nb=nb, tt=tt, has_state=has_state, prep_chunks=prep_chunks,
                          n_alias=len(aliases)),
        grid=(b // nb, t // tt), in_specs=in_specs,
        out_specs=[pl.BlockSpec((nb, tt, d), lambda bi, i: (bi, i, 0)),
                   s_spec,
                   pl.BlockSpec((nb, SUBLANES, 3 * w), lambda bi, i: (bi, 0, 0))],
        out_shape=[jax.ShapeDtypeStruct((b, t, d), F32),
                   s_shape,
                   jax.ShapeDtypeStruct((b, SUBLANES, 3 * w), F32)],
        input_output_aliases=aliases,
        scratch_shapes=[pltpu.VMEM((nb, 3, SUBLANES, w), F32),
                        pltpu.VMEM((n_rows, w), F32), pltpu.VMEM((n_rows, w), F32), pltpu.VMEM((n_rows, w), F32),
                        pltpu.VMEM((n_rows, w), BF16),
                        pltpu.VMEM((n_rows, LANES), F32), pltpu.VMEM((n_rows, LANES), F32),
                        pltpu.VMEM((nb, n_heads, HD_C, HD_C), F32),
                        pltpu.VMEM((n_rows, w), BF16),
                        pltpu.VMEM((n_rows, w), BF16), pltpu.VMEM((n_rows, w), BF16), pltpu.VMEM((n_rows, w), F32),
                        pltpu.VMEM((n_heads // 2, n_rows, 2 * CHUNK), BF16),
                        pltpu.VMEM((n_heads, n_chunks, HD_C, CHUNK), BF16),
                        pltpu.VMEM((n_chunks, n_heads, HD_C), F32)],
        compiler_params=_params(), name="odd_layer")(*args)
    return x_new, ctail[:, SUBLANES - 3:], s_new


def _cast_kernel(w_ref, *rest):
    if len(rest) == 2:
        scale_ref, o_ref = rest
        o_ref[...] = (w_ref[...] * scale_ref[...]).astype(o_ref.dtype)
    else:
        (o_ref,) = rest
        o_ref[...] = w_ref[...].astype(o_ref.dtype)


def _to_bf16(w_all, layer, n_cols, col_scale=None):
    k = w_all.shape[1]
    assert n_cols % CAST_COLS == 0
    in_specs = [pl.BlockSpec((None, k, CAST_COLS), lambda j: (layer, 0, j))]
    args = [w_all]
    if col_scale is not None:
        in_specs.append(pl.BlockSpec((1, CAST_COLS), lambda j: (0, j)))
        args.append(col_scale.reshape(1, n_cols))
    return pl.pallas_call(
        _cast_kernel, grid=(n_cols // CAST_COLS,), in_specs=in_specs,
        out_specs=pl.BlockSpec((k, CAST_COLS), lambda j: (0, j)),
        out_shape=jax.ShapeDtypeStruct((k, n_cols), BF16),
        compiler_params=pltpu.CompilerParams(dimension_semantics=("parallel",)), name="to_bf16")(*args)


def _gate_cast_kernel(w_ref, o_ref, *, n_gate):
    lane = lax.broadcasted_iota(jnp.int32, w_ref.shape, 1)
    o_ref[...] = jnp.where(lane < n_gate, w_ref[...], 0.0).astype(o_ref.dtype)


def _gate_to_bf16(w_all, layer, first_col, n_gate):
    k = w_all.shape[1]
    assert first_col % LANES == 0 and n_gate <= LANES and first_col + n_gate == w_all.shape[2]
    return pl.pallas_call(
        functools.partial(_gate_cast_kernel, n_gate=n_gate), grid=(1,),
        in_specs=[pl.BlockSpec((None, k, LANES), lambda j: (layer, 0, first_col // LANES))],
        out_specs=pl.BlockSpec((k, LANES), lambda j: (0, 0)),
        out_shape=jax.ShapeDtypeStruct((k, LANES), BF16), name="gate_to_bf16")(w_all)


def _band_bias(table):
    n_heads = table.shape[0]
    assert CHUNK - 1 <= REL_CLIP < BAND - 1
    low = table[:, REL_CLIP - (CHUNK - 1):]
    high = jnp.broadcast_to(table[:, 2 * REL_CLIP:], (n_heads, BAND - 1 - REL_CLIP))
    r = jnp.concatenate([low, high], axis=1)[:, ::-1].astype(F32) * LOG2E
    length = r.shape[1]
    padded = jnp.pad(r, ((0, 0), (0, 1)))
    shifted = jnp.tile(padded, (1, CHUNK))[:, :CHUNK * length].reshape(n_heads, CHUNK, length)
    bias = shifted[:, :, CHUNK - 1:CHUNK - 1 + BAND]
    return bias.reshape(n_heads // 2, 2 * CHUNK, BAND)


def kernel(x_prompt, x_sample, cache_conv_a, cache_k_b, cache_v_b, state_conv_c, state_s_c, norm_pre, norm_post, w_in_even, conv_w_a, rel_bias_b, w_out_even, w_in_odd, conv_w_c, a_log_c, dt_bias_c, out_norm_c, w_out_odd):
    depth = norm_pre.shape[0]
    n_even, n_odd = w_in_even.shape[0], w_in_odd.shape[0]
    n_heads_b = cache_k_b.shape[-2]
    kc_all = cache_k_b.reshape(cache_k_b.shape[:3] + (-1,)).astype(BF16)
    vc_all = cache_v_b.reshape(cache_v_b.shape[:3] + (-1,)).astype(BF16)
    yp, ys = x_prompt, x_sample
    conv_a_p, conv_a_s, conv_c_p, conv_c_s = [], [], [], []
    kv_p = kv_s = s_p = s_s = None
    for layer in range(depth):
        i = layer // 2
        g_pre, g_post = norm_pre[layer], norm_post[layer]
        if layer % 2 == 0:
            n_in = w_in_even.shape[2]
            w_blk = n_in // 8
            col_scale = jnp.ones((n_in,), F32).at[4 * w_blk:5 * w_blk].set(HD_B ** -0.5 * LOG2E)
            w_in_bf = _to_bf16(w_in_even, i, n_in, col_scale)
            w_out_bf = _to_bf16(w_out_even, i, w_out_even.shape[2])
            shared = (g_pre, g_post, w_in_bf, conv_w_a[i], _band_bias(rel_bias_b[i]), w_out_bf, i, n_even)
            yp, hst, kv_p = _even_layer(yp, None, None, None, *shared, kv_p)
            conv_a_p.append(hst)
            ys, hst, kv_s = _even_layer(ys, cache_conv_a, kc_all, vc_all, *shared, kv_s)
            conv_a_s.append(hst)
        else:
            n_heads = a_log_c.shape[1]
            n_main = w_in_odd.shape[2] - 2 * n_heads
            w_main_bf = _to_bf16(w_in_odd, i, n_main)
            w_gate_bf = _gate_to_bf16(w_in_odd, i, n_main, 2 * n_heads)
            pad = (n_heads, LANES - 2 * n_heads)
            ab = jnp.stack([jnp.pad(a_log_c[i], pad), jnp.pad(dt_bias_c[i], pad)]).astype(F32)
            shared = (g_pre, g_post, w_main_bf, w_gate_bf, conv_w_c[i], ab, out_norm_c[i].reshape(1, -1),
                      _to_bf16(w_out_odd, i, w_out_odd.shape[2]), i, n_odd)
            yp, hst, s_p = _odd_layer(yp, None, None, *shared, s_p)
            conv_c_p.append(hst)
            ys, hst, s_s = _odd_layer(ys, state_conv_c, state_s_c, *shared, s_s)
            conv_c_s.append(hst)

    def heads_apart(kv):
        return kv.reshape(kv.shape[:-1] + (n_heads_b, kv.shape[-1] // n_heads_b))

    return (yp, ys,
            jnp.stack(conv_a_p), heads_apart(kv_p[0]), heads_apart(kv_p[1]), jnp.stack(conv_c_p), s_p,
            jnp.stack(conv_a_s), heads_apart(kv_s[0]), heads_apart(kv_s[1]), jnp.stack(conv_c_s), s_s)
```

```python
import functools
import itertools
import math

import jax
import jax.numpy as jnp
from jax import lax
from jax.experimental import pallas as pl
from jax.experimental.pallas import tpu as pltpu

F32 = jnp.float32
BF16 = jnp.bfloat16

EPS = 1e-6
LOG2E = math.log2(math.e)
CHUNK = 64
BAND_PAST = 8 * CHUNK
BAND = BAND_PAST + CHUNK
REL_CLIP = 4 * CHUNK
HD_B = 64
HD_C = 128
LANES = 128
SUBLANES = 8
VMEM_LIMIT_BYTES = 56 * 1024 * 1024
STATE_VMEM_BUDGET_BYTES = 18 * 1024 * 1024
ROW_TILE = 512
CAST_COLS = 512
PROJ_BLOCKS = 4


def _silu(x):
    return x * (1.0 / (1.0 + jnp.exp(-x)))


def _softplus(x):
    return jnp.maximum(x, 0.0) + jnp.log1p(jnp.exp(-jnp.abs(x)))


def _dot(a, b):
    return jnp.dot(a.astype(BF16), b.astype(BF16), preferred_element_type=F32)


def _load_rows(ref, nb):
    parts = [ref[s] for s in range(nb)]
    return parts[0] if nb == 1 else jnp.concatenate(parts, axis=0)


def _pre_norm(x_ref, gain_ref, nb):
    x = _load_rows(x_ref, nb)
    ms = jnp.mean(x * x, axis=-1, keepdims=True)
    return (x * lax.rsqrt(ms + EPS) * gain_ref[...]).astype(BF16)


def _project_out(ybuf, wout_ref, gain_ref, x_ref, xo_ref, tt, r0, r1):
    out = jnp.dot(ybuf[r0:r1, :], wout_ref[...], preferred_element_type=F32)
    ms = jnp.mean(out * out, axis=-1, keepdims=True)
    delta = out * lax.rsqrt(ms + EPS) * gain_ref[...]
    for s in range(x_ref.shape[0]):
        lo, hi = max(r0, s * tt), min(r1, (s + 1) * tt)
        if lo < hi:
            frames = slice(lo - s * tt, hi - s * tt)
            xo_ref[s, frames, :] = x_ref[s, frames, :] + delta[lo - r0:hi - r0]
    yield


def _causal_dwconv(prev_rows, new_rows, taps):
    tt = new_rows.shape[0]
    width = taps.shape[0]
    assert width <= SUBLANES
    full = jnp.concatenate([prev_rows, new_rows], axis=0)
    sums = [None, None]
    shifted = full
    for delay in range(width):
        if delay and delay % 2 == 0:
            shifted = pltpu.roll(shifted, 2, axis=0)
        term = shifted * taps[width - 1 - delay:width - delay]
        sums[delay % 2] = term if sums[delay % 2] is None else sums[delay % 2] + term
    acc = sums[0] if sums[1] is None else sums[0] + pltpu.roll(sums[1], 1, axis=0)
    return acc[SUBLANES:, :], full[tt:tt + SUBLANES, :]


def _streams_per_step(b, t, state_bytes_per_stream):
    tt = min(t, ROW_TILE)
    nb = max(1, min(ROW_TILE // tt, STATE_VMEM_BUDGET_BYTES // state_bytes_per_stream))
    while b % nb:
        nb -= 1
    return nb, tt


def _resident(shape):
    return pl.BlockSpec(shape, lambda bi, i: (0,) * len(shape), pipeline_mode=pl.Buffered(1))


def _params():
    return pltpu.CompilerParams(dimension_semantics=("parallel", "arbitrary"), vmem_limit_bytes=VMEM_LIMIT_BYTES)


def _even_layer_kernel(*refs, nb, tt, has_cache, n_earlier, stacked):
    x_ref, gpre_ref, win_ref, cw_ref, bias_ref, wout_ref, gpost_ref = refs[:7]
    if has_cache:
        hist_ref, kc_ref, vc_ref = refs[7:10]
        refs = refs[10:]
    else:
        refs = refs[7:]
    earlier = refs[:2 * n_earlier]
    refs = refs[2 * n_earlier:]
    xo_ref, utail_ref, knew_ref, vnew_ref, utails, kbuf, vbuf, qbuf, zbuf, ybuf = refs
    if stacked:
        for j in range(n_earlier):
            knew_ref[j] = earlier[2 * j][...]
            vnew_ref[j] = earlier[2 * j + 1][...]
        knew_ref, vnew_ref = knew_ref.at[n_earlier], vnew_ref.at[n_earlier]
    w = win_ref.shape[-1] // 8
    i = pl.program_id(1)

    @pl.when(i == 0)
    def _init():
        utails[...] = jnp.zeros(utails.shape, F32)
        if has_cache:
            for s in range(nb):
                utails[s, SUBLANES - 2:SUBLANES, :] = hist_ref[s]
            kbuf[:, 0:BAND_PAST, :] = kc_ref[...]
            vbuf[:, 0:BAND_PAST, :] = vc_ref[...]
        else:
            kbuf[:, 0:BAND_PAST, :] = jnp.zeros((nb, BAND_PAST, w), BF16)
            vbuf[:, 0:BAND_PAST, :] = jnp.zeros((nb, BAND_PAST, w), BF16)

    hb = _pre_norm(x_ref, gpre_ref, nb)

    def proj(j):
        return jnp.dot(hb, win_ref[:, j * w:(j + 1) * w], preferred_element_type=F32)

    u = proj(1) * proj(2)
    gate = proj(0) * _silu(proj(3))
    for s in range(nb):
        srows = slice(s * tt, (s + 1) * tt)
        conv, tail = _causal_dwconv(utails[s], u[srows], cw_ref[...])
        utail_ref[s] = tail
        utails[s] = tail
        ybuf[srows, 0:w] = (gate[srows] * conv).astype(ybuf.dtype)

    qbuf[...] = proj(4).astype(qbuf.dtype)
    k = proj(5)
    v = proj(6)
    for s in range(nb):
        srows = slice(s * tt, (s + 1) * tt)
        knew_ref[s] = k[srows]
        vnew_ref[s] = v[srows]
        kbuf[s, BAND_PAST:BAND_PAST + tt, :] = k[srows].astype(kbuf.dtype)
        vbuf[s, BAND_PAST:BAND_PAST + tt, :] = v[srows].astype(vbuf.dtype)
    zbuf[...] = _silu(proj(7)).astype(zbuf.dtype)

    chunks_per_stream = tt // CHUNK
    n_pairs = w // LANES
    lane = lax.broadcasted_iota(jnp.int32, (CHUNK, LANES), 1)
    first_head = lane < HD_B

    def attend(first_tile):
        def band_start(c):
            missing = max(BAND_PAST - CHUNK * c, 0) if first_tile else 0
            return missing // LANES * LANES, missing % LANES

        def scores(s, c, hp):
            cols = slice(hp * LANES, (hp + 1) * LANES)
            lo, masked = band_start(c)
            r0 = s * tt + c * CHUNK
            qp = qbuf[r0:r0 + CHUNK, cols]
            zero = jnp.zeros_like(qp)
            q2 = jnp.concatenate([jnp.where(first_head, qp, zero), jnp.where(first_head, zero, qp)], axis=0)
            kp = kbuf[s, c * CHUNK + lo:c * CHUNK + BAND, cols]
            sc = lax.dot_general(q2, kp, (((1,), (1,)), ((), ())), preferred_element_type=F32)
            sc = sc + bias_ref[hp, :, lo:BAND]
            if masked:
                key_idx = lax.broadcasted_iota(jnp.int32, sc.shape, 1)
                sc = jnp.where(key_idx >= masked, sc, -jnp.inf)
            return sc

        def weighted_values(s, c, hp, sc):
            cols = slice(hp * LANES, (hp + 1) * LANES)
            lo, _ = band_start(c)
            m = jnp.max(sc, axis=-1, keepdims=True)
            e = jnp.exp2(sc - m)
            l = jnp.sum(e, axis=-1, keepdims=True)
            vp = vbuf[s, c * CHUNK + lo:c * CHUNK + BAND, cols]
            pv = jnp.dot(e.astype(BF16), vp, preferred_element_type=F32) / l
            return jnp.where(first_head, pv[0:CHUNK], pv[CHUNK:2 * CHUNK])

        items = [(s, c, hp) for s in range(nb) for c in range(chunks_per_stream) for hp in range(n_pairs)]
        skew = 3
        pending, outs = {}, {}
        for step in range(len(items) + skew):
            if step < len(items):
                pending[step] = scores(*items[step])
            j = step - skew
            if j >= 0:
                s, c, hp = items[j]
                outs[hp] = weighted_values(s, c, hp, pending.pop(j))
                if hp == n_pairs - 1:
                    rows = slice(s * tt + c * CHUNK, s * tt + (c + 1) * CHUNK)
                    o = jnp.concatenate([outs[x] for x in range(n_pairs)], axis=1)
                    ybuf[rows, w:2 * w] = (o * zbuf[rows, :].astype(F32)).astype(ybuf.dtype)

    if has_cache:
        attend(False)
    else:
        pl.when(i == 0)(functools.partial(attend, True))
        pl.when(i > 0)(functools.partial(attend, False))
    for s in range(nb):
        kbuf[s, 0:BAND_PAST, :] = kbuf[s, tt:tt + BAND_PAST, :]
        vbuf[s, 0:BAND_PAST, :] = vbuf[s, tt:tt + BAND_PAST, :]

    _interleave(_project_out(ybuf, wout_ref, gpost_ref, x_ref, xo_ref, tt, 0, nb * tt))


def _layer_slot_in(stacked, nb, slot):
    rest = stacked.shape[2:]
    return pl.BlockSpec((None, nb) + rest, lambda bi, i: (slot, bi) + (0,) * len(rest))


def _per_stream_out(shape, nb, earlier, last):
    zeros = (0,) * len(shape)
    own = pl.BlockSpec((nb,) + shape, lambda bi, i: (bi,) + zeros)
    if not last:
        return own, None, []
    n_layers = len(earlier) + 1
    spec = pl.BlockSpec((n_layers, nb) + shape, lambda bi, i: (0, bi) + zeros)
    return spec, n_layers, [own] * len(earlier)


def _even_layer(x, hist, kc, vc, g_pre, g_post, w_in_bf, conv_w, bias2, w_out_bf, slot, n_slots, earlier_kv):
    b, t, d = x.shape
    w = w_in_bf.shape[1] // 8
    n_heads = w // HD_B
    has_cache = kc is not None
    assert t % CHUNK == 0 and (t <= ROW_TILE or t % ROW_TILE == 0)
    band_bytes = 2 * (BAND_PAST + min(t, ROW_TILE)) * w * 2
    state_bytes = band_bytes + (2 * 2 * BAND_PAST * w * 2 if has_cache else 0)
    nb, tt = _streams_per_step(b, t, state_bytes)
    n_rows = nb * tt
    consts = [g_pre.reshape(1, d), w_in_bf, conv_w, bias2, w_out_bf, g_post.reshape(1, d)]
    in_specs = [pl.BlockSpec((nb, tt, d), lambda bi, i: (bi, i, 0))] + [_resident(c.shape) for c in consts]
    args = [x] + consts
    if has_cache:
        in_specs += [_layer_slot_in(hist, nb, slot), _layer_slot_in(kc, nb, slot), _layer_slot_in(vc, nb, slot)]
        args += [hist, kc, vc]
    assert tt == min(BAND_PAST, t)
    last = slot == n_slots - 1
    assert len(earlier_kv) == slot
    kv_spec, n_layers, earlier_specs = _per_stream_out((tt, w), nb, earlier_kv, last)
    kv_shape = jax.ShapeDtypeStruct(((n_layers,) if last else ()) + (b, tt, w), F32)
    n_earlier = len(earlier_specs)
    for k_prev, v_prev in earlier_kv[:n_earlier]:
        in_specs += earlier_specs[:1] * 2
        args += [k_prev, v_prev]
    x_new, utail, k_out, v_out = pl.pallas_call(
        functools.partial(_even_layer_kernel, nb=nb, tt=tt, has_cache=has_cache, n_earlier=n_earlier, stacked=last),
        grid=(b // nb, t // tt), in_specs=in_specs,
        out_specs=[pl.BlockSpec((nb, tt, d), lambda bi, i: (bi, i, 0)),
                   pl.BlockSpec((nb, SUBLANES, w), lambda bi, i: (bi, 0, 0)),
                   kv_spec, kv_spec],
        out_shape=[jax.ShapeDtypeStruct((b, t, d), F32),
                   jax.ShapeDtypeStruct((b, SUBLANES, w), F32),
                   kv_shape, kv_shape],
        scratch_shapes=[pltpu.VMEM((nb, SUBLANES, w), F32),
                        pltpu.VMEM((nb, BAND_PAST + tt, w), BF16), pltpu.VMEM((nb, BAND_PAST + tt, w), BF16),
                        pltpu.VMEM((n_rows, w), BF16), pltpu.VMEM((n_rows, w), BF16),
                        pltpu.VMEM((n_rows, 2 * w), BF16)],
        compiler_params=_params(), name="even_layer")(*args)
    del n_heads
    return x_new, utail[:, SUBLANES - 2:], (k_out, v_out)


def _cumsum_rows(x):
    n = x.shape[0]
    row = lax.broadcasted_iota(jnp.int32, x.shape, 0)
    shift = 1
    while shift < n:
        x = x + jnp.where(row >= shift, pltpu.roll(x, shift, axis=0), 0.0)
        shift *= 2
    return x


def _pair_masks():
    row = lax.broadcasted_iota(jnp.int32, (CHUNK, 2 * CHUNK), 0)
    lane = lax.broadcasted_iota(jnp.int32, (CHUNK, 2 * CHUNK), 1)
    first = lane < CHUNK
    col = jnp.where(first, lane, lane - CHUNK)
    return first, row, col


def _block_diag(pk, first):
    tiled = jnp.concatenate([pk, pk], axis=0).astype(BF16)
    keep = jnp.concatenate([first, jnp.logical_not(first)], axis=0)
    return jnp.where(keep, tiled, jnp.zeros_like(tiled))


def _unit_lower_inverses(mats, out, first, row, col):
    n = mats[0].shape[0]
    eye = jnp.where(row == col, 1.0, 0.0)
    xs = [-a for a in mats]
    ps = [eye + x for x in xs]
    ys = [jnp.dot(x.astype(BF16), _block_diag(x, first), preferred_element_type=F32) for x in xs]
    yield
    power = 2
    while 2 * power < n:
        sts = [jnp.dot(jnp.concatenate([p, y], axis=0).astype(BF16), _block_diag(y, first),
                       preferred_element_type=F32) for p, y in zip(ps, ys)]
        ps = [p + st[0:n] for p, st in zip(ps, sts)]
        ys = [st[n:2 * n] for st in sts]
        power *= 2
        yield
    out[:] = [p + jnp.dot(p.astype(BF16), _block_diag(y, first), preferred_element_type=F32)
              for p, y in zip(ps, ys)]
    yield


def _weave(main, fillers, period):
    filler = itertools.chain(*fillers)
    for n, _ in enumerate(main, start=1):
        if n % period == 0:
            next(filler, None)
    for _ in filler:
        pass


def _interleave(*generators):
    live = list(generators)
    while live:
        for g in list(live):
            try:
                next(g)
            except StopIteration:
                live.remove(g)


def _odd_layer_kernel(*refs, nb, tt, has_state, prep_chunks, n_earlier, stacked):
    x_ref, gpre_ref, win_ref, wg_ref, cw_ref, ab_ref, on_ref, wout_ref, gpost_ref = refs[:9]
    if has_state:
        hist_ref, s0_ref = refs[9:11]
        refs = refs[11:]
    else:
        refs = refs[9:]
    earlier = refs[:n_earlier]
    refs = refs[n_earlier:]
    (xo_ref, sout_ref, ctail_ref, ctails, qs, ks, vs, zs, gsc, bsc, sbuf, ybuf,
     wbuf, qtbuf, u0buf, qkbuf, kdtbuf, eglbuf) = refs
    w = win_ref.shape[-1] // 4
    n_heads = w // HD_C
    n_rows = nb * tt
    i = pl.program_id(1)

    @pl.when(i == 0)
    def _init():
        ctails[...] = jnp.zeros(ctails.shape, F32)
        if has_state:
            for s in range(nb):
                for g in range(3):
                    ctails[s, g, SUBLANES - 3:SUBLANES, :] = hist_ref[s, :, g * w:(g + 1) * w]
            sbuf[...] = s0_ref[...]
        else:
            sbuf[...] = jnp.zeros(sbuf.shape, F32)

    hb = _pre_norm(x_ref, gpre_ref, nb)

    gl = jnp.dot(hb, wg_ref[...], preferred_element_type=F32)
    bsc[...] = 1.0 / (1.0 + jnp.exp(-gl))
    gsc[...] = -jnp.exp(ab_ref[0:1, :]) * _softplus(gl + ab_ref[1:2, :])

    def project_group(g, dst, n_blocks):
        inv_scale_sq = float(HD_C) if g == 0 else 1.0
        sum_mat = jnp.full((HD_C, HD_C), inv_scale_sq, BF16)
        bw = w // n_blocks
        for blk in range(n_blocks):
            gcols = slice(g * w + blk * bw, g * w + (blk + 1) * bw)
            raw = jnp.dot(hb, win_ref[:, gcols], preferred_element_type=F32)
            for s in range(nb):
                srows = slice(s * tt, (s + 1) * tt)
                c, tail = _causal_dwconv(ctails[s, g, :, blk * bw:(blk + 1) * bw], raw[srows], cw_ref[:, gcols])
                c = _silu(c)
                ctail_ref[s, :, gcols] = tail
                ctails[s, g, :, blk * bw:(blk + 1) * bw] = tail
                if g == 2:
                    dst[srows, blk * bw:(blk + 1) * bw] = c
                else:
                    for h in range(bw // HD_C):
                        ch = c[:, h * HD_C:(h + 1) * HD_C]
                        ss = jnp.dot((ch * ch).astype(BF16), sum_mat, preferred_element_type=F32)
                        dst[srows, blk * bw + h * HD_C:blk * bw + (h + 1) * HD_C] = ch * lax.rsqrt(
                            ss + EPS * inv_scale_sq)
            yield

    def project_gate(n_blocks):
        bw = w // n_blocks
        for blk in range(n_blocks):
            z = jnp.dot(hb, win_ref[:, 3 * w + blk * bw:3 * w + (blk + 1) * bw], preferred_element_type=F32)
            zs[:, blk * bw:(blk + 1) * bw] = _silu(z).astype(zs.dtype)
            yield

    _interleave(project_group(0, qs, 1))
    _interleave(project_group(1, ks, 1))

    first, row, col = _pair_masks()
    incl = row >= col
    strict = row > col
    assert n_heads % 2 == 0 and 2 * CHUNK == HD_C
    pairs = range(n_heads // 2)

    heads = range(n_heads)
    hcols = [slice(h * HD_C, (h + 1) * HD_C) for h in heads]
    n_chunks = n_rows // CHUNK

    def chunk_rows(cidx):
        if isinstance(cidx, int):
            return slice(cidx * CHUNK, (cidx + 1) * CHUNK)
        return pl.ds(pl.multiple_of(cidx * CHUNK, CHUNK), CHUNK)

    def prepare_chunks(step):
        items = []
        duos = []
        for j in range(prep_chunks):
            cidx = step * prep_chunks + j
            rows = chunk_rows(cidx)
            gc = _cumsum_rows(gsc[rows, :])
            gct = gc.T
            bet = bsc[rows, :]
            for h in heads:
                gcol = jnp.broadcast_to(gc[:, n_heads + h:n_heads + h + 1], (CHUNK, HD_C))
                items.append(dict(cidx=cidx, rows=rows, h=h, gcol=gcol,
                                  b=jnp.broadcast_to(bet[:, h:h + 1], (CHUNK, HD_C))))
            for p in pairs:
                ia, ib = items[-n_heads + 2 * p], items[-n_heads + 2 * p + 1]
                g0 = n_heads + 2 * p
                grow = jnp.concatenate([gct[g0:g0 + 1, :], gct[g0 + 1:g0 + 2, :]], axis=1)
                gcol2 = jnp.where(first, ia["gcol"], ib["gcol"])
                duos.append(dict(cidx=cidx, rows=rows, p=p, a=ia, b=ib, beta=jnp.where(first, ia["b"], ib["b"]),
                                 decay=jnp.exp(jnp.where(incl, gcol2 - grow, -jnp.inf))))
        for it in items:
            rows, c = it["rows"], hcols[it["h"]]
            it["eg"] = jnp.exp(it["gcol"])
            it["q"], it["k"] = qs[rows, c], ks[rows, c]
        for it in items:
            kb = it["k"].astype(BF16)
            it["qk_kk"] = lax.dot_general(jnp.concatenate([it["q"].astype(BF16), kb], axis=0), kb,
                                          (((1,), (1,)), ((), ())), preferred_element_type=F32)
        yield
        for du in duos:
            qk_kk = jnp.concatenate([du["a"]["qk_kk"], du["b"]["qk_kk"]], axis=1)
            du["qk"] = qk_kk[0:CHUNK] * du["decay"]
            du["mat"] = jnp.where(strict, du["beta"] * qk_kk[CHUNK:2 * CHUNK] * du["decay"], 0.0)
        tinvs = [None] * len(duos)
        yield from _unit_lower_inverses([du["mat"] for du in duos], tinvs, first, row, col)
        sols = []
        for tinv, du in zip(tinvs, duos):
            rhs = jnp.concatenate(
                [jnp.concatenate([vs[it["rows"], hcols[it["h"]]] * it["b"], it["k"] * (it["b"] * it["eg"])], axis=1)
                 for it in (du["a"], du["b"])], axis=0)
            sols.append(jnp.dot(_block_diag(tinv, first), rhs.astype(BF16), preferred_element_type=F32))
        yield
        for sol, du in zip(sols, duos):
            rows = du["rows"]
            qkbuf[du["p"], rows, :] = du["qk"].astype(BF16)
            for half, it in enumerate((du["a"], du["b"])):
                gcol, c, h = it["gcol"], hcols[it["h"]], it["h"]
                g_last = gcol[CHUNK - 1:CHUNK, :]
                sol_h = sol[half * CHUNK:(half + 1) * CHUNK]
                u0buf[rows, c] = sol_h[:, 0:HD_C]
                wbuf[rows, c] = sol_h[:, HD_C:].astype(BF16)
                qtbuf[rows, c] = (it["q"] * it["eg"]).astype(BF16)
                kdtbuf[h, it["cidx"]] = (it["k"] * jnp.exp(g_last - gcol)).T.astype(BF16)
                eglbuf[it["cidx"], h:h + 1, :] = jnp.exp(g_last)

    chunks_per_stream = tt // CHUNK

    def recur_chunks(step):
        for j in range(prep_chunks):
            cidx = step * prep_chunks + j
            rows = chunk_rows(cidx)
            s = 0 if nb == 1 else cidx // chunks_per_stream
            s_old = [sbuf[s, h] for h in heads]
            ws_qs = [_dot(jnp.concatenate([wbuf[rows, hcols[h]], qtbuf[rows, hcols[h]]], axis=0), s_old[h])
                     for h in heads]
            yield
            u = [u0buf[rows, hcols[h]] - ws_qs[h][0:CHUNK] for h in heads]
            qku = [jnp.dot(_block_diag(qkbuf[p, rows, :], first),
                           jnp.concatenate([u[2 * p], u[2 * p + 1]], axis=0).astype(BF16),
                           preferred_element_type=F32) for p in pairs]
            o = [ws_qs[h][CHUNK:2 * CHUNK] + qku[h // 2][(h % 2) * CHUNK:(h % 2 + 1) * CHUNK] for h in heads]
            for h in heads:
                sbuf[s, h] = eglbuf[cidx, h:h + 1, :] * s_old[h] + _dot(kdtbuf[h, cidx], u[h])
            yield
            for h in heads:
                ms = jnp.mean(o[h] * o[h], axis=-1, keepdims=True)
                on = o[h] * lax.rsqrt(ms + EPS) * on_ref[...]
                ybuf[rows, hcols[h]] = (on * zs[rows, hcols[h]].astype(F32)).astype(ybuf.dtype)

    n_steps = n_chunks // prep_chunks
    _weave(prepare_chunks(0), [project_group(2, vs, PROJ_BLOCKS), project_gate(PROJ_BLOCKS)], period=1)

    def pipelined(step, carry):
        _interleave(prepare_chunks(step), recur_chunks(step - 1))
        return carry

    lax.fori_loop(1, n_steps, pipelined, 0)
    _interleave(recur_chunks(n_steps - 1))
    if stacked:
        for j in range(n_earlier):
            sout_ref[j] = earlier[j][...]
        sout_ref[n_earlier] = sbuf[...]
    else:
        sout_ref[...] = sbuf[...]
    _interleave(_project_out(ybuf, wout_ref, gpost_ref, x_ref, xo_ref, tt, 0, n_rows))


def _odd_layer(x, hist, s0, g_pre, g_post, w_in_bf, w_gate_bf, conv_w, ab, onorm, w_out_bf, slot, n_slots, earlier_s):
    b, t, d = x.shape
    w = w_in_bf.shape[1] // 4
    n_heads = w // HD_C
    has_state = s0 is not None
    last = slot == n_slots - 1
    assert len(earlier_s) == slot
    assert t % CHUNK == 0 and (t <= ROW_TILE or t % ROW_TILE == 0)
    state_copies = 1 + 2 * (n_slots if last else 1) + 2 * (slot if last else 0) + (2 if has_state else 0)
    nb, tt = _streams_per_step(b, t, n_heads * HD_C * HD_C * 4 * state_copies)
    n_rows = nb * tt
    n_chunks = n_rows // CHUNK
    prep_chunks = max(c for c in (1, 2, 4) if n_chunks % c == 0)
    consts = [g_pre.reshape(1, d), w_in_bf, w_gate_bf, conv_w, ab, onorm, w_out_bf, g_post.reshape(1, d)]
    in_specs = [pl.BlockSpec((nb, tt, d), lambda bi, i: (bi, i, 0))] + [_resident(c.shape) for c in consts]
    args = [x] + consts
    if has_state:
        in_specs += [_layer_slot_in(hist, nb, slot), _layer_slot_in(s0, nb, slot)]
        args += [hist, s0]
    s_spec, n_layers, earlier_specs = _per_stream_out((n_heads, HD_C, HD_C), nb, earlier_s, last)
    s_shape = jax.ShapeDtypeStruct(((n_layers,) if last else ()) + (b, n_heads, HD_C, HD_C), F32)
    n_earlier = len(earlier_specs)
    in_specs += earlier_specs
    args += list(earlier_s[:n_earlier])
    x_new, s_new, ctail = pl.pallas_call(
        functools.partial(_odd_layer_kernel, nb=nb, tt=tt, has_state=has_state, prep_chunks=prep_chunks,
                          n_earlier=n_earlier, stacked=last),
        grid=(b // nb, t // tt), in_specs=in_specs,
        out_specs=[pl.BlockSpec((nb, tt, d), lambda bi, i: (bi, i, 0)),
                   s_spec,
                   pl.BlockSpec((nb, SUBLANES, 3 * w), lambda bi, i: (bi, 0, 0))],
        out_shape=[jax.ShapeDtypeStruct((b, t, d), F32),
                   s_shape,
                   jax.ShapeDtypeStruct((b, SUBLANES, 3 * w), F32)],
        scratch_shapes=[pltpu.VMEM((nb, 3, SUBLANES, w), F32),
                        pltpu.VMEM((n_rows, w), F32), pltpu.VMEM((n_rows, w), F32), pltpu.VMEM((n_rows, w), F32),
                        pltpu.VMEM((n_rows, w), BF16),
                        pltpu.VMEM((n_rows, LANES), F32), pltpu.VMEM((n_rows, LANES), F32),
                        pltpu.VMEM((nb, n_heads, HD_C, HD_C), F32),
                        pltpu.VMEM((n_rows, w), BF16),
                        pltpu.VMEM((n_rows, w), BF16), pltpu.VMEM((n_rows, w), BF16), pltpu.VMEM((n_rows, w), F32),
                        pltpu.VMEM((n_heads // 2, n_rows, 2 * CHUNK), BF16),
                        pltpu.VMEM((n_heads, n_chunks, HD_C, CHUNK), BF16),
                        pltpu.VMEM((n_chunks, n_heads, HD_C), F32)],
        compiler_params=_params(), name="odd_layer")(*args)
    return x_new, ctail[:, SUBLANES - 3:], s_new


def _cast_kernel(w_ref, *rest):
    if len(rest) == 2:
        scale_ref, o_ref = rest
        o_ref[...] = (w_ref[...] * scale_ref[...]).astype(o_ref.dtype)
    else:
        (o_ref,) = rest
        o_ref[...] = w_ref[...].astype(o_ref.dtype)


def _to_bf16(w_all, layer, n_cols, col_scale=None):
    k = w_all.shape[1]
    assert n_cols % CAST_COLS == 0
    in_specs = [pl.BlockSpec((None, k, CAST_COLS), lambda j: (layer, 0, j))]
    args = [w_all]
    if col_scale is not None:
        in_specs.append(pl.BlockSpec((1, CAST_COLS), lambda j: (0, j)))
        args.append(col_scale.reshape(1, n_cols))
    return pl.pallas_call(
        _cast_kernel, grid=(n_cols // CAST_COLS,), in_specs=in_specs,
        out_specs=pl.BlockSpec((k, CAST_COLS), lambda j: (0, j)),
        out_shape=jax.ShapeDtypeStruct((k, n_cols), BF16),
        compiler_params=pltpu.CompilerParams(dimension_semantics=("parallel",)), name="to_bf16")(*args)


def _gate_cast_kernel(w_ref, o_ref, *, n_gate):
    lane = lax.broadcasted_iota(jnp.int32, w_ref.shape, 1)
    o_ref[...] = jnp.where(lane < n_gate, w_ref[...], 0.0).astype(o_ref.dtype)


def _gate_to_bf16(w_all, layer, first_col, n_gate):
    k = w_all.shape[1]
    assert first_col % LANES == 0 and n_gate <= LANES and first_col + n_gate == w_all.shape[2]
    return pl.pallas_call(
        functools.partial(_gate_cast_kernel, n_gate=n_gate), grid=(1,),
        in_specs=[pl.BlockSpec((None, k, LANES), lambda j: (layer, 0, first_col // LANES))],
        out_specs=pl.BlockSpec((k, LANES), lambda j: (0, 0)),
        out_shape=jax.ShapeDtypeStruct((k, LANES), BF16), name="gate_to_bf16")(w_all)


def _band_bias(table):
    n_heads = table.shape[0]
    assert CHUNK - 1 <= REL_CLIP < BAND - 1
    low = table[:, REL_CLIP - (CHUNK - 1):]
    high = jnp.broadcast_to(table[:, 2 * REL_CLIP:], (n_heads, BAND - 1 - REL_CLIP))
    r = jnp.concatenate([low, high], axis=1)[:, ::-1].astype(F32) * LOG2E
    length = r.shape[1]
    padded = jnp.pad(r, ((0, 0), (0, 1)))
    shifted = jnp.tile(padded, (1, CHUNK))[:, :CHUNK * length].reshape(n_heads, CHUNK, length)
    bias = shifted[:, :, CHUNK - 1:CHUNK - 1 + BAND]
    return bias.reshape(n_heads // 2, 2 * CHUNK, BAND)


def kernel(x_prompt, x_sample, cache_conv_a, cache_k_b, cache_v_b, state_conv_c, state_s_c, norm_pre, norm_post, w_in_even, conv_w_a, rel_bias_b, w_out_even, w_in_odd, conv_w_c, a_log_c, dt_bias_c, out_norm_c, w_out_odd):
    depth = norm_pre.shape[0]
    n_even, n_odd = w_in_even.shape[0], w_in_odd.shape[0]
    n_heads_b = cache_k_b.shape[-2]
    kc_all = cache_k_b.reshape(cache_k_b.shape[:3] + (-1,)).astype(BF16)
    vc_all = cache_v_b.reshape(cache_v_b.shape[:3] + (-1,)).astype(BF16)
    yp, ys = x_prompt, x_sample
    conv_a_p, conv_a_s, conv_c_p, conv_c_s = [], [], [], []
    kv_p, kv_s, s_p, s_s = [], [], [], []
    for layer in range(depth):
        i = layer // 2
        g_pre, g_post = norm_pre[layer], norm_post[layer]
        if layer % 2 == 0:
            n_in = w_in_even.shape[2]
            w_blk = n_in // 8
            col_scale = jnp.ones((n_in,), F32).at[4 * w_blk:5 * w_blk].set(HD_B ** -0.5 * LOG2E)
            w_in_bf = _to_bf16(w_in_even, i, n_in, col_scale)
            w_out_bf = _to_bf16(w_out_even, i, w_out_even.shape[2])
            shared = (g_pre, g_post, w_in_bf, conv_w_a[i], _band_bias(rel_bias_b[i]), w_out_bf, i, n_even)
            yp, hst, kv = _even_layer(yp, None, None, None, *shared, kv_p)
            kv_p.append(kv)
            conv_a_p.append(hst)
            ys, hst, kv = _even_layer(ys, cache_conv_a, kc_all, vc_all, *shared, kv_s)
            kv_s.append(kv)
            conv_a_s.append(hst)
        else:
            n_heads = a_log_c.shape[1]
            n_main = w_in_odd.shape[2] - 2 * n_heads
            w_main_bf = _to_bf16(w_in_odd, i, n_main)
            w_gate_bf = _gate_to_bf16(w_in_odd, i, n_main, 2 * n_heads)
            pad = (n_heads, LANES - 2 * n_heads)
            ab = jnp.stack([jnp.pad(a_log_c[i], pad), jnp.pad(dt_bias_c[i], pad)]).astype(F32)
            shared = (g_pre, g_post, w_main_bf, w_gate_bf, conv_w_c[i], ab, out_norm_c[i].reshape(1, -1),
                      _to_bf16(w_out_odd, i, w_out_odd.shape[2]), i, n_odd)
            yp, hst, s_new = _odd_layer(yp, None, None, *shared, s_p)
            s_p.append(s_new)
            conv_c_p.append(hst)
            ys, hst, s_new = _odd_layer(ys, state_conv_c, state_s_c, *shared, s_s)
            s_s.append(s_new)
            conv_c_s.append(hst)

    def heads_apart(kv):
        return kv.reshape(kv.shape[:-1] + (n_heads_b, kv.shape[-1] // n_heads_b))

    return (yp, ys,
            jnp.stack(conv_a_p), heads_apart(kv_p[-1][0]), heads_apart(kv_p[-1][1]), jnp.stack(conv_c_p), s_p[-1],
            jnp.stack(conv_a_s), heads_apart(kv_s[-1][0]), heads_apart(kv_s[-1][1]), jnp.stack(conv_c_s), s_s[-1])
```

```python
import functools
import itertools
import math

import jax
import jax.numpy as jnp
from jax import lax
from jax.experimental import pallas as pl
from jax.experimental.pallas import tpu as pltpu

F32 = jnp.float32
BF16 = jnp.bfloat16

EPS = 1e-6
LOG2E = math.log2(math.e)
CHUNK = 64
BAND_PAST = 8 * CHUNK
BAND = BAND_PAST + CHUNK
REL_CLIP = 4 * CHUNK
HD_B = 64
HD_C = 128
LANES = 128
SUBLANES = 8
VMEM_LIMIT_BYTES = 56 * 1024 * 1024
STATE_VMEM_BUDGET_BYTES = 18 * 1024 * 1024
ROW_TILE = 512
CAST_COLS = 512
PROJ_BLOCKS = 4


def _silu(x):
    return x * (1.0 / (1.0 + jnp.exp(-x)))


def _softplus(x):
    return jnp.maximum(x, 0.0) + jnp.log1p(jnp.exp(-jnp.abs(x)))


def _dot(a, b):
    return jnp.dot(a.astype(BF16), b.astype(BF16), preferred_element_type=F32)


def _load_rows(ref, nb):
    parts = [ref[s] for s in range(nb)]
    return parts[0] if nb == 1 else jnp.concatenate(parts, axis=0)


def _pre_norm(x_ref, gain_ref, nb):
    x = _load_rows(x_ref, nb)
    ms = jnp.mean(x * x, axis=-1, keepdims=True)
    return (x * lax.rsqrt(ms + EPS) * gain_ref[...]).astype(BF16)


def _project_out(ybuf, wout_ref, gain_ref, x_ref, xo_ref, tt, r0, r1):
    out = jnp.dot(ybuf[r0:r1, :], wout_ref[...], preferred_element_type=F32)
    ms = jnp.mean(out * out, axis=-1, keepdims=True)
    delta = out * lax.rsqrt(ms + EPS) * gain_ref[...]
    for s in range(x_ref.shape[0]):
        lo, hi = max(r0, s * tt), min(r1, (s + 1) * tt)
        if lo < hi:
            frames = slice(lo - s * tt, hi - s * tt)
            xo_ref[s, frames, :] = x_ref[s, frames, :] + delta[lo - r0:hi - r0]
    yield


def _causal_dwconv(prev_rows, new_rows, taps):
    tt = new_rows.shape[0]
    width = taps.shape[0]
    assert width <= SUBLANES
    full = jnp.concatenate([prev_rows, new_rows], axis=0)
    sums = [None, None]
    shifted = full
    for delay in range(width):
        if delay and delay % 2 == 0:
            shifted = pltpu.roll(shifted, 2, axis=0)
        term = shifted * taps[width - 1 - delay:width - delay]
        sums[delay % 2] = term if sums[delay % 2] is None else sums[delay % 2] + term
    acc = sums[0] if sums[1] is None else sums[0] + pltpu.roll(sums[1], 1, axis=0)
    return acc[SUBLANES:, :], full[tt:tt + SUBLANES, :]


def _streams_per_step(b, t, state_bytes_per_stream):
    tt = min(t, ROW_TILE)
    nb = max(1, min(ROW_TILE // tt, STATE_VMEM_BUDGET_BYTES // state_bytes_per_stream))
    while b % nb:
        nb -= 1
    return nb, tt


def _resident(shape):
    return pl.BlockSpec(shape, lambda bi, i: (0,) * len(shape), pipeline_mode=pl.Buffered(1))


def _params():
    return pltpu.CompilerParams(dimension_semantics=("parallel", "arbitrary"), vmem_limit_bytes=VMEM_LIMIT_BYTES)


def _even_layer_kernel(*refs, nb, tt, has_cache, n_earlier, stacked):
    x_ref, gpre_ref, win_ref, cw_ref, bias_ref, wout_ref, gpost_ref = refs[:7]
    if has_cache:
        hist_ref, kc_ref, vc_ref = refs[7:10]
        refs = refs[10:]
    else:
        refs = refs[7:]
    earlier = refs[:2 * n_earlier]
    refs = refs[2 * n_earlier:]
    xo_ref, utail_ref, knew_ref, vnew_ref, utails, kbuf, vbuf, qbuf, zbuf, ybuf = refs
    w = win_ref.shape[-1] // 8
    i = pl.program_id(1)
    if stacked:
        @pl.when(i == pl.num_programs(1) - 1)
        def _pass_through():
            for j in range(n_earlier):
                knew_ref[j] = earlier[2 * j][...]
                vnew_ref[j] = earlier[2 * j + 1][...]
        knew_ref, vnew_ref = knew_ref.at[n_earlier], vnew_ref.at[n_earlier]

    @pl.when(i == 0)
    def _init():
        utails[...] = jnp.zeros(utails.shape, F32)
        if has_cache:
            for s in range(nb):
                utails[s, SUBLANES - 2:SUBLANES, :] = hist_ref[s]
            kbuf[:, 0:BAND_PAST, :] = kc_ref[...]
            vbuf[:, 0:BAND_PAST, :] = vc_ref[...]
        else:
            kbuf[:, 0:BAND_PAST, :] = jnp.zeros((nb, BAND_PAST, w), BF16)
            vbuf[:, 0:BAND_PAST, :] = jnp.zeros((nb, BAND_PAST, w), BF16)

    hb = _pre_norm(x_ref, gpre_ref, nb)

    def proj(j):
        return jnp.dot(hb, win_ref[:, j * w:(j + 1) * w], preferred_element_type=F32)

    u = proj(1) * proj(2)
    gate = proj(0) * _silu(proj(3))
    for s in range(nb):
        srows = slice(s * tt, (s + 1) * tt)
        conv, tail = _causal_dwconv(utails[s], u[srows], cw_ref[...])
        utail_ref[s] = tail
        utails[s] = tail
        ybuf[srows, 0:w] = (gate[srows] * conv).astype(ybuf.dtype)

    qbuf[...] = proj(4).astype(qbuf.dtype)
    k = proj(5)
    v = proj(6)
    for s in range(nb):
        srows = slice(s * tt, (s + 1) * tt)
        knew_ref[s] = k[srows]
        vnew_ref[s] = v[srows]
        kbuf[s, BAND_PAST:BAND_PAST + tt, :] = k[srows].astype(kbuf.dtype)
        vbuf[s, BAND_PAST:BAND_PAST + tt, :] = v[srows].astype(vbuf.dtype)
    zbuf[...] = _silu(proj(7)).astype(zbuf.dtype)

    chunks_per_stream = tt // CHUNK
    n_pairs = w // LANES
    lane = lax.broadcasted_iota(jnp.int32, (CHUNK, LANES), 1)
    first_head = lane < HD_B

    def attend(first_tile):
        def band_start(c):
            missing = max(BAND_PAST - CHUNK * c, 0) if first_tile else 0
            return missing // LANES * LANES, missing % LANES

        def scores(s, c, hp):
            cols = slice(hp * LANES, (hp + 1) * LANES)
            lo, masked = band_start(c)
            r0 = s * tt + c * CHUNK
            qp = qbuf[r0:r0 + CHUNK, cols]
            zero = jnp.zeros_like(qp)
            q2 = jnp.concatenate([jnp.where(first_head, qp, zero), jnp.where(first_head, zero, qp)], axis=0)
            kp = kbuf[s, c * CHUNK + lo:c * CHUNK + BAND, cols]
            sc = lax.dot_general(q2, kp, (((1,), (1,)), ((), ())), preferred_element_type=F32)
            sc = sc + bias_ref[hp, :, lo:BAND]
            if masked:
                key_idx = lax.broadcasted_iota(jnp.int32, sc.shape, 1)
                sc = jnp.where(key_idx >= masked, sc, -jnp.inf)
            return sc

        def weighted_values(s, c, hp, sc):
            cols = slice(hp * LANES, (hp + 1) * LANES)
            lo, _ = band_start(c)
            m = jnp.max(sc, axis=-1, keepdims=True)
            e = jnp.exp2(sc - m)
            l = jnp.sum(e, axis=-1, keepdims=True)
            vp = vbuf[s, c * CHUNK + lo:c * CHUNK + BAND, cols]
            pv = jnp.dot(e.astype(BF16), vp, preferred_element_type=F32) / l
            return jnp.where(first_head, pv[0:CHUNK], pv[CHUNK:2 * CHUNK])

        items = [(s, c, hp) for s in range(nb) for c in range(chunks_per_stream) for hp in range(n_pairs)]
        skew = 3
        pending, outs = {}, {}
        for step in range(len(items) + skew):
            if step < len(items):
                pending[step] = scores(*items[step])
            j = step - skew
            if j >= 0:
                s, c, hp = items[j]
                outs[hp] = weighted_values(s, c, hp, pending.pop(j))
                if hp == n_pairs - 1:
                    rows = slice(s * tt + c * CHUNK, s * tt + (c + 1) * CHUNK)
                    o = jnp.concatenate([outs[x] for x in range(n_pairs)], axis=1)
                    ybuf[rows, w:2 * w] = (o * zbuf[rows, :].astype(F32)).astype(ybuf.dtype)

    if has_cache:
        attend(False)
    else:
        pl.when(i == 0)(functools.partial(attend, True))
        pl.when(i > 0)(functools.partial(attend, False))
    for s in range(nb):
        kbuf[s, 0:BAND_PAST, :] = kbuf[s, tt:tt + BAND_PAST, :]
        vbuf[s, 0:BAND_PAST, :] = vbuf[s, tt:tt + BAND_PAST, :]

    _interleave(_project_out(ybuf, wout_ref, gpost_ref, x_ref, xo_ref, tt, 0, nb * tt))


def _layer_slot_in(stacked, nb, slot):
    rest = stacked.shape[2:]
    return pl.BlockSpec((None, nb) + rest, lambda bi, i: (slot, bi) + (0,) * len(rest))


def _per_stream_out(shape, nb, earlier, last):
    zeros = (0,) * len(shape)
    own = pl.BlockSpec((nb,) + shape, lambda bi, i: (bi,) + zeros)
    if not last:
        return own, None, []
    n_layers = len(earlier) + 1
    spec = pl.BlockSpec((n_layers, nb) + shape, lambda bi, i: (0, bi) + zeros)
    return spec, n_layers, [own] * len(earlier)


def _even_layer(x, hist, kc, vc, g_pre, g_post, w_in_bf, conv_w, bias2, w_out_bf, slot, n_slots, earlier_kv):
    b, t, d = x.shape
    w = w_in_bf.shape[1] // 8
    n_heads = w // HD_B
    has_cache = kc is not None
    assert t % CHUNK == 0 and (t <= ROW_TILE or t % ROW_TILE == 0)
    band_bytes = 2 * (BAND_PAST + min(t, ROW_TILE)) * w * 2
    state_bytes = band_bytes + (2 * 2 * BAND_PAST * w * 2 if has_cache else 0)
    nb, tt = _streams_per_step(b, t, state_bytes)
    n_rows = nb * tt
    consts = [g_pre.reshape(1, d), w_in_bf, conv_w, bias2, w_out_bf, g_post.reshape(1, d)]
    in_specs = [pl.BlockSpec((nb, tt, d), lambda bi, i: (bi, i, 0))] + [_resident(c.shape) for c in consts]
    args = [x] + consts
    if has_cache:
        in_specs += [_layer_slot_in(hist, nb, slot), _layer_slot_in(kc, nb, slot), _layer_slot_in(vc, nb, slot)]
        args += [hist, kc, vc]
    assert tt == min(BAND_PAST, t)
    last = slot == n_slots - 1
    assert len(earlier_kv) == slot
    kv_spec, n_layers, earlier_specs = _per_stream_out((tt, w), nb, earlier_kv, last)
    kv_shape = jax.ShapeDtypeStruct(((n_layers,) if last else ()) + (b, tt, w), F32)
    n_earlier = len(earlier_specs)
    for k_prev, v_prev in earlier_kv[:n_earlier]:
        in_specs += earlier_specs[:1] * 2
        args += [k_prev, v_prev]
    x_new, utail, k_out, v_out = pl.pallas_call(
        functools.partial(_even_layer_kernel, nb=nb, tt=tt, has_cache=has_cache, n_earlier=n_earlier, stacked=last),
        grid=(b // nb, t // tt), in_specs=in_specs,
        out_specs=[pl.BlockSpec((nb, tt, d), lambda bi, i: (bi, i, 0)),
                   pl.BlockSpec((nb, SUBLANES, w), lambda bi, i: (bi, 0, 0)),
                   kv_spec, kv_spec],
        out_shape=[jax.ShapeDtypeStruct((b, t, d), F32),
                   jax.ShapeDtypeStruct((b, SUBLANES, w), F32),
                   kv_shape, kv_shape],
        scratch_shapes=[pltpu.VMEM((nb, SUBLANES, w), F32),
                        pltpu.VMEM((nb, BAND_PAST + tt, w), BF16), pltpu.VMEM((nb, BAND_PAST + tt, w), BF16),
                        pltpu.VMEM((n_rows, w), BF16), pltpu.VMEM((n_rows, w), BF16),
                        pltpu.VMEM((n_rows, 2 * w), BF16)],
        compiler_params=_params(), name="even_layer")(*args)
    del n_heads
    return x_new, utail[:, SUBLANES - 2:], (k_out, v_out)


def _cumsum_rows(x):
    n = x.shape[0]
    row = lax.broadcasted_iota(jnp.int32, x.shape, 0)
    shift = 1
    while shift < n:
        x = x + jnp.where(row >= shift, pltpu.roll(x, shift, axis=0), 0.0)
        shift *= 2
    return x


def _pair_masks():
    row = lax.broadcasted_iota(jnp.int32, (CHUNK, 2 * CHUNK), 0)
    lane = lax.broadcasted_iota(jnp.int32, (CHUNK, 2 * CHUNK), 1)
    first = lane < CHUNK
    col = jnp.where(first, lane, lane - CHUNK)
    return first, row, col


def _block_diag(pk, first):
    tiled = jnp.concatenate([pk, pk], axis=0).astype(BF16)
    keep = jnp.concatenate([first, jnp.logical_not(first)], axis=0)
    return jnp.where(keep, tiled, jnp.zeros_like(tiled))


def _unit_lower_inverses(mats, out, first, row, col):
    n = mats[0].shape[0]
    eye = jnp.where(row == col, 1.0, 0.0)
    xs = [-a for a in mats]
    ps = [eye + x for x in xs]
    ys = [jnp.dot(x.astype(BF16), _block_diag(x, first), preferred_element_type=F32) for x in xs]
    yield
    power = 2
    while 2 * power < n:
        sts = [jnp.dot(jnp.concatenate([p, y], axis=0).astype(BF16), _block_diag(y, first),
                       preferred_element_type=F32) for p, y in zip(ps, ys)]
        ps = [p + st[0:n] for p, st in zip(ps, sts)]
        ys = [st[n:2 * n] for st in sts]
        power *= 2
        yield
    out[:] = [p + jnp.dot(p.astype(BF16), _block_diag(y, first), preferred_element_type=F32)
              for p, y in zip(ps, ys)]
    yield


def _weave(main, fillers, period):
    filler = itertools.chain(*fillers)
    for n, _ in enumerate(main, start=1):
        if n % period == 0:
            next(filler, None)
    for _ in filler:
        pass


def _interleave(*generators):
    live = list(generators)
    while live:
        for g in list(live):
            try:
                next(g)
            except StopIteration:
                live.remove(g)


def _odd_layer_kernel(*refs, nb, tt, has_state, prep_chunks, n_earlier, stacked):
    x_ref, gpre_ref, win_ref, wg_ref, cw_ref, ab_ref, on_ref, wout_ref, gpost_ref = refs[:9]
    if has_state:
        hist_ref, s0_ref = refs[9:11]
        refs = refs[11:]
    else:
        refs = refs[9:]
    earlier = refs[:n_earlier]
    refs = refs[n_earlier:]
    (xo_ref, sout_ref, ctail_ref, ctails, qs, ks, vs, zs, gsc, bsc, sbuf, ybuf,
     wbuf, qtbuf, u0buf, qkbuf, kdtbuf, eglbuf) = refs
    w = win_ref.shape[-1] // 4
    n_heads = w // HD_C
    n_rows = nb * tt
    i = pl.program_id(1)

    @pl.when(i == 0)
    def _init():
        ctails[...] = jnp.zeros(ctails.shape, F32)
        if has_state:
            for s in range(nb):
                for g in range(3):
                    ctails[s, g, SUBLANES - 3:SUBLANES, :] = hist_ref[s, :, g * w:(g + 1) * w]
            sbuf[...] = s0_ref[...]
        else:
            sbuf[...] = jnp.zeros(sbuf.shape, F32)

    hb = _pre_norm(x_ref, gpre_ref, nb)

    gl = jnp.dot(hb, wg_ref[...], preferred_element_type=F32)
    bsc[...] = 1.0 / (1.0 + jnp.exp(-gl))
    gsc[...] = -jnp.exp(ab_ref[0:1, :]) * _softplus(gl + ab_ref[1:2, :])

    def project_group(g, dst, n_blocks):
        inv_scale_sq = float(HD_C) if g == 0 else 1.0
        sum_mat = jnp.full((HD_C, HD_C), inv_scale_sq, BF16)
        bw = w // n_blocks
        for blk in range(n_blocks):
            gcols = slice(g * w + blk * bw, g * w + (blk + 1) * bw)
            raw = jnp.dot(hb, win_ref[:, gcols], preferred_element_type=F32)
            for s in range(nb):
                srows = slice(s * tt, (s + 1) * tt)
                c, tail = _causal_dwconv(ctails[s, g, :, blk * bw:(blk + 1) * bw], raw[srows], cw_ref[:, gcols])
                c = _silu(c)
                ctail_ref[s, :, gcols] = tail
                ctails[s, g, :, blk * bw:(blk + 1) * bw] = tail
                if g == 2:
                    dst[srows, blk * bw:(blk + 1) * bw] = c
                else:
                    for h in range(bw // HD_C):
                        ch = c[:, h * HD_C:(h + 1) * HD_C]
                        ss = jnp.dot((ch * ch).astype(BF16), sum_mat, preferred_element_type=F32)
                        dst[srows, blk * bw + h * HD_C:blk * bw + (h + 1) * HD_C] = ch * lax.rsqrt(
                            ss + EPS * inv_scale_sq)
            yield

    def project_gate(n_blocks):
        bw = w // n_blocks
        for blk in range(n_blocks):
            z = jnp.dot(hb, win_ref[:, 3 * w + blk * bw:3 * w + (blk + 1) * bw], preferred_element_type=F32)
            zs[:, blk * bw:(blk + 1) * bw] = _silu(z).astype(zs.dtype)
            yield

    _interleave(project_group(0, qs, 1))
    _interleave(project_group(1, ks, 1))

    first, row, col = _pair_masks()
    incl = row >= col
    strict = row > col
    assert n_heads % 2 == 0 and 2 * CHUNK == HD_C
    pairs = range(n_heads // 2)

    heads = range(n_heads)
    hcols = [slice(h * HD_C, (h + 1) * HD_C) for h in heads]
    n_chunks = n_rows // CHUNK

    def chunk_rows(cidx):
        if isinstance(cidx, int):
            return slice(cidx * CHUNK, (cidx + 1) * CHUNK)
        return pl.ds(pl.multiple_of(cidx * CHUNK, CHUNK), CHUNK)

    def prepare_chunks(step):
        items = []
        duos = []
        for j in range(prep_chunks):
            cidx = step * prep_chunks + j
            rows = chunk_rows(cidx)
            gc = _cumsum_rows(gsc[rows, :])
            gct = gc.T
            bet = bsc[rows, :]
            for h in heads:
                gcol = jnp.broadcast_to(gc[:, n_heads + h:n_heads + h + 1], (CHUNK, HD_C))
                items.append(dict(cidx=cidx, rows=rows, h=h, gcol=gcol,
                                  b=jnp.broadcast_to(bet[:, h:h + 1], (CHUNK, HD_C))))
            for p in pairs:
                ia, ib = items[-n_heads + 2 * p], items[-n_heads + 2 * p + 1]
                g0 = n_heads + 2 * p
                grow = jnp.concatenate([gct[g0:g0 + 1, :], gct[g0 + 1:g0 + 2, :]], axis=1)
                gcol2 = jnp.where(first, ia["gcol"], ib["gcol"])
                duos.append(dict(cidx=cidx, rows=rows, p=p, a=ia, b=ib, beta=jnp.where(first, ia["b"], ib["b"]),
                                 decay=jnp.exp(jnp.where(incl, gcol2 - grow, -jnp.inf))))
        for it in items:
            rows, c = it["rows"], hcols[it["h"]]
            it["eg"] = jnp.exp(it["gcol"])
            it["q"], it["k"] = qs[rows, c], ks[rows, c]
        for it in items:
            kb = it["k"].astype(BF16)
            it["qk_kk"] = lax.dot_general(jnp.concatenate([it["q"].astype(BF16), kb], axis=0), kb,
                                          (((1,), (1,)), ((), ())), preferred_element_type=F32)
        yield
        for du in duos:
            qk_kk = jnp.concatenate([du["a"]["qk_kk"], du["b"]["qk_kk"]], axis=1)
            du["qk"] = qk_kk[0:CHUNK] * du["decay"]
            du["mat"] = jnp.where(strict, du["beta"] * qk_kk[CHUNK:2 * CHUNK] * du["decay"], 0.0)
        tinvs = [None] * len(duos)
        yield from _unit_lower_inverses([du["mat"] for du in duos], tinvs, first, row, col)
        sols = []
        for tinv, du in zip(tinvs, duos):
            rhs = jnp.concatenate(
                [jnp.concatenate([vs[it["rows"], hcols[it["h"]]] * it["b"], it["k"] * (it["b"] * it["eg"])], axis=1)
                 for it in (du["a"], du["b"])], axis=0)
            sols.append(jnp.dot(_block_diag(tinv, first), rhs.astype(BF16), preferred_element_type=F32))
        yield
        for sol, du in zip(sols, duos):
            rows = du["rows"]
            qkbuf[du["p"], rows, :] = du["qk"].astype(BF16)
            for half, it in enumerate((du["a"], du["b"])):
                gcol, c, h = it["gcol"], hcols[it["h"]], it["h"]
                g_last = gcol[CHUNK - 1:CHUNK, :]
                sol_h = sol[half * CHUNK:(half + 1) * CHUNK]
                u0buf[rows, c] = sol_h[:, 0:HD_C]
                wbuf[rows, c] = sol_h[:, HD_C:].astype(BF16)
                qtbuf[rows, c] = (it["q"] * it["eg"]).astype(BF16)
                kdtbuf[h, it["cidx"]] = (it["k"] * jnp.exp(g_last - gcol)).T.astype(BF16)
                eglbuf[it["cidx"], h:h + 1, :] = jnp.exp(g_last)

    chunks_per_stream = tt // CHUNK

    def recur_chunks(step):
        for j in range(prep_chunks):
            cidx = step * prep_chunks + j
            rows = chunk_rows(cidx)
            s = 0 if nb == 1 else cidx // chunks_per_stream
            s_old = [sbuf[s, h] for h in heads]
            ws_qs = [_dot(jnp.concatenate([wbuf[rows, hcols[h]], qtbuf[rows, hcols[h]]], axis=0), s_old[h])
                     for h in heads]
            yield
            u = [u0buf[rows, hcols[h]] - ws_qs[h][0:CHUNK] for h in heads]
            qku = [jnp.dot(_block_diag(qkbuf[p, rows, :], first),
                           jnp.concatenate([u[2 * p], u[2 * p + 1]], axis=0).astype(BF16),
                           preferred_element_type=F32) for p in pairs]
            o = [ws_qs[h][CHUNK:2 * CHUNK] + qku[h // 2][(h % 2) * CHUNK:(h % 2 + 1) * CHUNK] for h in heads]
            for h in heads:
                sbuf[s, h] = eglbuf[cidx, h:h + 1, :] * s_old[h] + _dot(kdtbuf[h, cidx], u[h])
            yield
            for h in heads:
                ms = jnp.mean(o[h] * o[h], axis=-1, keepdims=True)
                on = o[h] * lax.rsqrt(ms + EPS) * on_ref[...]
                ybuf[rows, hcols[h]] = (on * zs[rows, hcols[h]].astype(F32)).astype(ybuf.dtype)

    n_steps = n_chunks // prep_chunks
    _weave(prepare_chunks(0), [project_group(2, vs, PROJ_BLOCKS), project_gate(PROJ_BLOCKS)], period=1)

    def pipelined(step, carry):
        _interleave(prepare_chunks(step), recur_chunks(step - 1))
        return carry

    lax.fori_loop(1, n_steps, pipelined, 0)
    _interleave(recur_chunks(n_steps - 1))
    if stacked:
        @pl.when(i == pl.num_programs(1) - 1)
        def _pass_through():
            for j in range(n_earlier):
                sout_ref[j] = earlier[j][...]
        sout_ref[n_earlier] = sbuf[...]
    else:
        sout_ref[...] = sbuf[...]
    _interleave(_project_out(ybuf, wout_ref, gpost_ref, x_ref, xo_ref, tt, 0, n_rows))


def _odd_layer(x, hist, s0, g_pre, g_post, w_in_bf, w_gate_bf, conv_w, ab, onorm, w_out_bf, slot, n_slots, earlier_s):
    b, t, d = x.shape
    w = w_in_bf.shape[1] // 4
    n_heads = w // HD_C
    has_state = s0 is not None
    last = slot == n_slots - 1
    assert len(earlier_s) == slot
    assert t % CHUNK == 0 and (t <= ROW_TILE or t % ROW_TILE == 0)
    state_copies = 1 + 2 * (n_slots if last else 1) + 2 * (slot if last else 0) + (2 if has_state else 0)
    nb, tt = _streams_per_step(b, t, n_heads * HD_C * HD_C * 4 * state_copies)
    n_rows = nb * tt
    n_chunks = n_rows // CHUNK
    prep_chunks = max(c for c in (1, 2, 4) if n_chunks % c == 0)
    consts = [g_pre.reshape(1, d), w_in_bf, w_gate_bf, conv_w, ab, onorm, w_out_bf, g_post.reshape(1, d)]
    in_specs = [pl.BlockSpec((nb, tt, d), lambda bi, i: (bi, i, 0))] + [_resident(c.shape) for c in consts]
    args = [x] + consts
    if has_state:
        in_specs += [_layer_slot_in(hist, nb, slot), _layer_slot_in(s0, nb, slot)]
        args += [hist, s0]
    s_spec, n_layers, earlier_specs = _per_stream_out((n_heads, HD_C, HD_C), nb, earlier_s, last)
    s_shape = jax.ShapeDtypeStruct(((n_layers,) if last else ()) + (b, n_heads, HD_C, HD_C), F32)
    n_earlier = len(earlier_specs)
    in_specs += earlier_specs
    args += list(earlier_s[:n_earlier])
    x_new, s_new, ctail = pl.pallas_call(
        functools.partial(_odd_layer_kernel, nb=nb, tt=tt, has_state=has_state, prep_chunks=prep_chunks,
                          n_earlier=n_earlier, stacked=last),
        grid=(b // nb, t // tt), in_specs=in_specs,
        out_specs=[pl.BlockSpec((nb, tt, d), lambda bi, i: (bi, i, 0)),
                   s_spec,
                   pl.BlockSpec((nb, SUBLANES, 3 * w), lambda bi, i: (bi, 0, 0))],
        out_shape=[jax.ShapeDtypeStruct((b, t, d), F32),
                   s_shape,
                   jax.ShapeDtypeStruct((b, SUBLANES, 3 * w), F32)],
        scratch_shapes=[pltpu.VMEM((nb, 3, SUBLANES, w), F32),
                        pltpu.VMEM((n_rows, w), F32), pltpu.VMEM((n_rows, w), F32), pltpu.VMEM((n_rows, w), F32),
                        pltpu.VMEM((n_rows, w), BF16),
                        pltpu.VMEM((n_rows, LANES), F32), pltpu.VMEM((n_rows, LANES), F32),
                        pltpu.VMEM((nb, n_heads, HD_C, HD_C), F32),
                        pltpu.VMEM((n_rows, w), BF16),
                        pltpu.VMEM((n_rows, w), BF16), pltpu.VMEM((n_rows, w), BF16), pltpu.VMEM((n_rows, w), F32),
                        pltpu.VMEM((n_heads // 2, n_rows, 2 * CHUNK), BF16),
                        pltpu.VMEM((n_heads, n_chunks, HD_C, CHUNK), BF16),
                        pltpu.VMEM((n_chunks, n_heads, HD_C), F32)],
        compiler_params=_params(), name="odd_layer")(*args)
    return x_new, ctail[:, SUBLANES - 3:], s_new


def _cast_kernel(w_ref, *rest):
    if len(rest) == 2:
        scale_ref, o_ref = rest
        o_ref[...] = (w_ref[...] * scale_ref[...]).astype(o_ref.dtype)
    else:
        (o_ref,) = rest
        o_ref[...] = w_ref[...].astype(o_ref.dtype)


def _to_bf16(w_all, layer, n_cols, col_scale=None):
    k = w_all.shape[1]
    assert n_cols % CAST_COLS == 0
    in_specs = [pl.BlockSpec((None, k, CAST_COLS), lambda j: (layer, 0, j))]
    args = [w_all]
    if col_scale is not None:
        in_specs.append(pl.BlockSpec((1, CAST_COLS), lambda j: (0, j)))
        args.append(col_scale.reshape(1, n_cols))
    return pl.pallas_call(
        _cast_kernel, grid=(n_cols // CAST_COLS,), in_specs=in_specs,
        out_specs=pl.BlockSpec((k, CAST_COLS), lambda j: (0, j)),
        out_shape=jax.ShapeDtypeStruct((k, n_cols), BF16),
        compiler_params=pltpu.CompilerParams(dimension_semantics=("parallel",)), name="to_bf16")(*args)


def _gate_cast_kernel(w_ref, o_ref, *, n_gate):
    lane = lax.broadcasted_iota(jnp.int32, w_ref.shape, 1)
    o_ref[...] = jnp.where(lane < n_gate, w_ref[...], 0.0).astype(o_ref.dtype)


def _gate_to_bf16(w_all, layer, first_col, n_gate):
    k = w_all.shape[1]
    assert first_col % LANES == 0 and n_gate <= LANES and first_col + n_gate == w_all.shape[2]
    return pl.pallas_call(
        functools.partial(_gate_cast_kernel, n_gate=n_gate), grid=(1,),
        in_specs=[pl.BlockSpec((None, k, LANES), lambda j: (layer, 0, first_col // LANES))],
        out_specs=pl.BlockSpec((k, LANES), lambda j: (0, 0)),
        out_shape=jax.ShapeDtypeStruct((k, LANES), BF16), name="gate_to_bf16")(w_all)


def _band_bias(table):
    n_heads = table.shape[0]
    assert CHUNK - 1 <= REL_CLIP < BAND - 1
    low = table[:, REL_CLIP - (CHUNK - 1):]
    high = jnp.broadcast_to(table[:, 2 * REL_CLIP:], (n_heads, BAND - 1 - REL_CLIP))
    r = jnp.concatenate([low, high], axis=1)[:, ::-1].astype(F32) * LOG2E
    length = r.shape[1]
    padded = jnp.pad(r, ((0, 0), (0, 1)))
    shifted = jnp.tile(padded, (1, CHUNK))[:, :CHUNK * length].reshape(n_heads, CHUNK, length)
    bias = shifted[:, :, CHUNK - 1:CHUNK - 1 + BAND]
    return bias.reshape(n_heads // 2, 2 * CHUNK, BAND)


def kernel(x_prompt, x_sample, cache_conv_a, cache_k_b, cache_v_b, state_conv_c, state_s_c, norm_pre, norm_post, w_in_even, conv_w_a, rel_bias_b, w_out_even, w_in_odd, conv_w_c, a_log_c, dt_bias_c, out_norm_c, w_out_odd):
    depth = norm_pre.shape[0]
    n_even, n_odd = w_in_even.shape[0], w_in_odd.shape[0]
    n_heads_b = cache_k_b.shape[-2]
    kc_all = cache_k_b.reshape(cache_k_b.shape[:3] + (-1,)).astype(BF16)
    vc_all = cache_v_b.reshape(cache_v_b.shape[:3] + (-1,)).astype(BF16)
    yp, ys = x_prompt, x_sample
    conv_a_p, conv_a_s, conv_c_p, conv_c_s = [], [], [], []
    kv_p, kv_s, s_p, s_s = [], [], [], []
    for layer in range(depth):
        i = layer // 2
        g_pre, g_post = norm_pre[layer], norm_post[layer]
        if layer % 2 == 0:
            n_in = w_in_even.shape[2]
            w_blk = n_in // 8
            col_scale = jnp.ones((n_in,), F32).at[4 * w_blk:5 * w_blk].set(HD_B ** -0.5 * LOG2E)
            w_in_bf = _to_bf16(w_in_even, i, n_in, col_scale)
            w_out_bf = _to_bf16(w_out_even, i, w_out_even.shape[2])
            shared = (g_pre, g_post, w_in_bf, conv_w_a[i], _band_bias(rel_bias_b[i]), w_out_bf, i, n_even)
            yp, hst, kv = _even_layer(yp, None, None, None, *shared, kv_p)
            kv_p.append(kv)
            conv_a_p.append(hst)
            ys, hst, kv = _even_layer(ys, cache_conv_a, kc_all, vc_all, *shared, kv_s)
            kv_s.append(kv)
            conv_a_s.append(hst)
        else:
            n_heads = a_log_c.shape[1]
            n_main = w_in_odd.shape[2] - 2 * n_heads
            w_main_bf = _to_bf16(w_in_odd, i, n_main)
            w_gate_bf = _gate_to_bf16(w_in_odd, i, n_main, 2 * n_heads)
            pad = (n_heads, LANES - 2 * n_heads)
            ab = jnp.stack([jnp.pad(a_log_c[i], pad), jnp.pad(dt_bias_c[i], pad)]).astype(F32)
            shared = (g_pre, g_post, w_main_bf, w_gate_bf, conv_w_c[i], ab, out_norm_c[i].reshape(1, -1),
                      _to_bf16(w_out_odd, i, w_out_odd.shape[2]), i, n_odd)
            yp, hst, s_new = _odd_layer(yp, None, None, *shared, s_p)
            s_p.append(s_new)
            conv_c_p.append(hst)
            ys, hst, s_new = _odd_layer(ys, state_conv_c, state_s_c, *shared, s_s)
            s_s.append(s_new)
            conv_c_s.append(hst)

    def heads_apart(kv):
        return kv.reshape(kv.shape[:-1] + (n_heads_b, kv.shape[-1] // n_heads_b))

    return (yp, ys,
            jnp.stack(conv_a_p), heads_apart(kv_p[-1][0]), heads_apart(kv_p[-1][1]), jnp.stack(conv_c_p), s_p[-1],
            jnp.stack(conv_a_s), heads_apart(kv_s[-1][0]), heads_apart(kv_s[-1][1]), jnp.stack(conv_c_s), s_s[-1])
```

```python
import functools
import itertools
import math

import jax
import jax.numpy as jnp
from jax import lax
from jax.experimental import pallas as pl
from jax.experimental.pallas import tpu as pltpu

F32 = jnp.float32
BF16 = jnp.bfloat16

EPS = 1e-6
LOG2E = math.log2(math.e)
CHUNK = 64
BAND_PAST = 8 * CHUNK
BAND = BAND_PAST + CHUNK
REL_CLIP = 4 * CHUNK
HD_B = 64
HD_C = 128
LANES = 128
SUBLANES = 8
VMEM_LIMIT_BYTES = 56 * 1024 * 1024
STATE_VMEM_BUDGET_BYTES = 18 * 1024 * 1024
ROW_TILE = 512
CAST_COLS = 512
PROJ_BLOCKS = 4


def _silu(x):
    return x * (1.0 / (1.0 + jnp.exp(-x)))


def _softplus(x):
    return jnp.maximum(x, 0.0) + jnp.log1p(jnp.exp(-jnp.abs(x)))


def _dot(a, b):
    return jnp.dot(a.astype(BF16), b.astype(BF16), preferred_element_type=F32)


def _load_rows(ref, nb):
    parts = [ref[s] for s in range(nb)]
    return parts[0] if nb == 1 else jnp.concatenate(parts, axis=0)


def _pre_norm(x_ref, gain_ref, nb):
    x = _load_rows(x_ref, nb)
    ms = jnp.mean(x * x, axis=-1, keepdims=True)
    return (x * lax.rsqrt(ms + EPS) * gain_ref[...]).astype(BF16)


def _project_out(ybuf, wout_ref, gain_ref, x_ref, xo_ref):
    nb, tt, _ = x_ref.shape
    out = jnp.dot(ybuf[...], wout_ref[...], preferred_element_type=F32)
    ms = jnp.mean(out * out, axis=-1, keepdims=True)
    delta = out * lax.rsqrt(ms + EPS) * gain_ref[...]
    for s in range(nb):
        xo_ref[s] = x_ref[s] + delta[s * tt:(s + 1) * tt]


def _causal_dwconv(prev_rows, new_rows, taps):
    tt = new_rows.shape[0]
    width = taps.shape[0]
    assert width <= SUBLANES
    full = jnp.concatenate([prev_rows, new_rows], axis=0)
    sums = [None, None]
    shifted = full
    for delay in range(width):
        if delay and delay % 2 == 0:
            shifted = pltpu.roll(shifted, 2, axis=0)
        term = shifted * taps[width - 1 - delay:width - delay]
        sums[delay % 2] = term if sums[delay % 2] is None else sums[delay % 2] + term
    acc = sums[0] if sums[1] is None else sums[0] + pltpu.roll(sums[1], 1, axis=0)
    return acc[SUBLANES:, :], full[tt:tt + SUBLANES, :]


def _streams_per_step(b, t, state_bytes_per_stream):
    tt = min(t, ROW_TILE)
    nb = max(1, min(ROW_TILE // tt, STATE_VMEM_BUDGET_BYTES // state_bytes_per_stream))
    while b % nb:
        nb -= 1
    return nb, tt


def _resident(shape):
    return pl.BlockSpec(shape, lambda bi, i: (0,) * len(shape), pipeline_mode=pl.Buffered(1))


def _params():
    return pltpu.CompilerParams(dimension_semantics=("parallel", "arbitrary"), vmem_limit_bytes=VMEM_LIMIT_BYTES)


def _even_layer_kernel(*refs, nb, tt, has_cache, n_earlier, stacked):
    x_ref, gpre_ref, win_ref, cw_ref, bias_ref, wout_ref, gpost_ref = refs[:7]
    if has_cache:
        hist_ref, kc_ref, vc_ref = refs[7:10]
        refs = refs[10:]
    else:
        refs = refs[7:]
    earlier = refs[:2 * n_earlier]
    refs = refs[2 * n_earlier:]
    xo_ref, utail_ref, knew_ref, vnew_ref, utails, kbuf, vbuf, qbuf, zbuf, ybuf = refs
    w = win_ref.shape[-1] // 8
    i = pl.program_id(1)
    if stacked:
        @pl.when(i == pl.num_programs(1) - 1)
        def _pass_through():
            for j in range(n_earlier):
                knew_ref[j] = earlier[2 * j][...]
                vnew_ref[j] = earlier[2 * j + 1][...]
        knew_ref, vnew_ref = knew_ref.at[n_earlier], vnew_ref.at[n_earlier]

    @pl.when(i == 0)
    def _init():
        utails[...] = jnp.zeros(utails.shape, F32)
        if has_cache:
            for s in range(nb):
                utails[s, SUBLANES - 2:SUBLANES, :] = hist_ref[s]
            kbuf[:, 0:BAND_PAST, :] = kc_ref[...]
            vbuf[:, 0:BAND_PAST, :] = vc_ref[...]
        else:
            kbuf[:, 0:BAND_PAST, :] = jnp.zeros((nb, BAND_PAST, w), BF16)
            vbuf[:, 0:BAND_PAST, :] = jnp.zeros((nb, BAND_PAST, w), BF16)

    hb = _pre_norm(x_ref, gpre_ref, nb)

    def proj(j):
        return jnp.dot(hb, win_ref[:, j * w:(j + 1) * w], preferred_element_type=F32)

    u = proj(1) * proj(2)
    gate = proj(0) * _silu(proj(3))
    for s in range(nb):
        srows = slice(s * tt, (s + 1) * tt)
        conv, tail = _causal_dwconv(utails[s], u[srows], cw_ref[...])
        utail_ref[s] = tail
        utails[s] = tail
        ybuf[srows, 0:w] = (gate[srows] * conv).astype(ybuf.dtype)

    qbuf[...] = proj(4).astype(qbuf.dtype)
    k = proj(5)
    v = proj(6)
    for s in range(nb):
        srows = slice(s * tt, (s + 1) * tt)
        knew_ref[s] = k[srows]
        vnew_ref[s] = v[srows]
        kbuf[s, BAND_PAST:BAND_PAST + tt, :] = k[srows].astype(kbuf.dtype)
        vbuf[s, BAND_PAST:BAND_PAST + tt, :] = v[srows].astype(vbuf.dtype)
    zbuf[...] = _silu(proj(7)).astype(zbuf.dtype)

    chunks_per_stream = tt // CHUNK
    n_pairs = w // LANES
    lane = lax.broadcasted_iota(jnp.int32, (CHUNK, LANES), 1)
    first_head = lane < HD_B

    def attend(first_tile):
        def band_start(c):
            missing = max(BAND_PAST - CHUNK * c, 0) if first_tile else 0
            return missing // LANES * LANES, missing % LANES

        def scores(s, c, hp):
            cols = slice(hp * LANES, (hp + 1) * LANES)
            lo, masked = band_start(c)
            r0 = s * tt + c * CHUNK
            qp = qbuf[r0:r0 + CHUNK, cols]
            zero = jnp.zeros_like(qp)
            q2 = jnp.concatenate([jnp.where(first_head, qp, zero), jnp.where(first_head, zero, qp)], axis=0)
            kp = kbuf[s, c * CHUNK + lo:c * CHUNK + BAND, cols]
            sc = lax.dot_general(q2, kp, (((1,), (1,)), ((), ())), preferred_element_type=F32)
            sc = sc + bias_ref[hp, :, lo:BAND]
            if masked:
                key_idx = lax.broadcasted_iota(jnp.int32, sc.shape, 1)
                sc = jnp.where(key_idx >= masked, sc, -jnp.inf)
            return sc

        def weighted_values(s, c, hp, sc):
            cols = slice(hp * LANES, (hp + 1) * LANES)
            lo, _ = band_start(c)
            m = jnp.max(sc, axis=-1, keepdims=True)
            e = jnp.exp2(sc - m)
            l = jnp.sum(e, axis=-1, keepdims=True)
            vp = vbuf[s, c * CHUNK + lo:c * CHUNK + BAND, cols]
            pv = jnp.dot(e.astype(BF16), vp, preferred_element_type=F32) / l
            return jnp.where(first_head, pv[0:CHUNK], pv[CHUNK:2 * CHUNK])

        items = [(s, c, hp) for s in range(nb) for c in range(chunks_per_stream) for hp in range(n_pairs)]
        skew = 3
        pending, outs = {}, {}
        for step in range(len(items) + skew):
            if step < len(items):
                pending[step] = scores(*items[step])
            j = step - skew
            if j >= 0:
                s, c, hp = items[j]
                outs[hp] = weighted_values(s, c, hp, pending.pop(j))
                if hp == n_pairs - 1:
                    rows = slice(s * tt + c * CHUNK, s * tt + (c + 1) * CHUNK)
                    o = jnp.concatenate([outs[x] for x in range(n_pairs)], axis=1)
                    ybuf[rows, w:2 * w] = (o * zbuf[rows, :].astype(F32)).astype(ybuf.dtype)

    if has_cache:
        attend(False)
    else:
        pl.when(i == 0)(functools.partial(attend, True))
        pl.when(i > 0)(functools.partial(attend, False))
    for s in range(nb):
        kbuf[s, 0:BAND_PAST, :] = kbuf[s, tt:tt + BAND_PAST, :]
        vbuf[s, 0:BAND_PAST, :] = vbuf[s, tt:tt + BAND_PAST, :]

    _project_out(ybuf, wout_ref, gpost_ref, x_ref, xo_ref)


def _layer_slot_in(stacked, nb, slot):
    rest = stacked.shape[2:]
    return pl.BlockSpec((None, nb) + rest, lambda bi, i: (slot, bi) + (0,) * len(rest))


def _per_stream_out(shape, nb, earlier, last):
    zeros = (0,) * len(shape)
    own = pl.BlockSpec((nb,) + shape, lambda bi, i: (bi,) + zeros)
    if not last:
        return own, None, []
    n_layers = len(earlier) + 1
    spec = pl.BlockSpec((n_layers, nb) + shape, lambda bi, i: (0, bi) + zeros)
    return spec, n_layers, [own] * len(earlier)


def _even_layer(x, hist, kc, vc, g_pre, g_post, w_in_bf, conv_w, bias2, w_out_bf, slot, n_slots, earlier_kv):
    b, t, d = x.shape
    w = w_in_bf.shape[1] // 8
    has_cache = kc is not None
    assert t % CHUNK == 0 and (t <= ROW_TILE or t % ROW_TILE == 0)
    band_bytes = 2 * (BAND_PAST + min(t, ROW_TILE)) * w * 2
    state_bytes = band_bytes + (2 * 2 * BAND_PAST * w * 2 if has_cache else 0)
    nb, tt = _streams_per_step(b, t, state_bytes)
    n_rows = nb * tt
    consts = [g_pre.reshape(1, d), w_in_bf, conv_w, bias2, w_out_bf, g_post.reshape(1, d)]
    in_specs = [pl.BlockSpec((nb, tt, d), lambda bi, i: (bi, i, 0))] + [_resident(c.shape) for c in consts]
    args = [x] + consts
    if has_cache:
        in_specs += [_layer_slot_in(hist, nb, slot), _layer_slot_in(kc, nb, slot), _layer_slot_in(vc, nb, slot)]
        args += [hist, kc, vc]
    assert tt == min(BAND_PAST, t)
    last = slot == n_slots - 1
    assert len(earlier_kv) == slot
    kv_spec, n_layers, earlier_specs = _per_stream_out((tt, w), nb, earlier_kv, last)
    kv_shape = jax.ShapeDtypeStruct(((n_layers,) if last else ()) + (b, tt, w), F32)
    n_earlier = len(earlier_specs)
    for k_prev, v_prev in earlier_kv[:n_earlier]:
        in_specs += earlier_specs[:1] * 2
        args += [k_prev, v_prev]
    x_new, utail, k_out, v_out = pl.pallas_call(
        functools.partial(_even_layer_kernel, nb=nb, tt=tt, has_cache=has_cache, n_earlier=n_earlier, stacked=last),
        grid=(b // nb, t // tt), in_specs=in_specs,
        out_specs=[pl.BlockSpec((nb, tt, d), lambda bi, i: (bi, i, 0)),
                   pl.BlockSpec((nb, SUBLANES, w), lambda bi, i: (bi, 0, 0)),
                   kv_spec, kv_spec],
        out_shape=[jax.ShapeDtypeStruct((b, t, d), F32),
                   jax.ShapeDtypeStruct((b, SUBLANES, w), F32),
                   kv_shape, kv_shape],
        scratch_shapes=[pltpu.VMEM((nb, SUBLANES, w), F32),
                        pltpu.VMEM((nb, BAND_PAST + tt, w), BF16), pltpu.VMEM((nb, BAND_PAST + tt, w), BF16),
                        pltpu.VMEM((n_rows, w), BF16), pltpu.VMEM((n_rows, w), BF16),
                        pltpu.VMEM((n_rows, 2 * w), BF16)],
        compiler_params=_params(), name="even_layer")(*args)
    return x_new, utail[:, SUBLANES - 2:], (k_out, v_out)


def _cumsum_rows(x):
    n = x.shape[0]
    row = lax.broadcasted_iota(jnp.int32, x.shape, 0)
    shift = 1
    while shift < n:
        x = x + jnp.where(row >= shift, pltpu.roll(x, shift, axis=0), 0.0)
        shift *= 2
    return x


def _pair_masks():
    row = lax.broadcasted_iota(jnp.int32, (CHUNK, 2 * CHUNK), 0)
    lane = lax.broadcasted_iota(jnp.int32, (CHUNK, 2 * CHUNK), 1)
    first = lane < CHUNK
    col = jnp.where(first, lane, lane - CHUNK)
    return first, row, col


def _block_diag(pk, first):
    tiled = jnp.concatenate([pk, pk], axis=0).astype(BF16)
    keep = jnp.concatenate([first, jnp.logical_not(first)], axis=0)
    return jnp.where(keep, tiled, jnp.zeros_like(tiled))


def _unit_lower_inverses(mats, out, first, row, col):
    n = mats[0].shape[0]
    eye = jnp.where(row == col, 1.0, 0.0)
    xs = [-a for a in mats]
    ps = [eye + x for x in xs]
    ys = [jnp.dot(x.astype(BF16), _block_diag(x, first), preferred_element_type=F32) for x in xs]
    yield
    power = 2
    while 2 * power < n:
        sts = [jnp.dot(jnp.concatenate([p, y], axis=0).astype(BF16), _block_diag(y, first),
                       preferred_element_type=F32) for p, y in zip(ps, ys)]
        ps = [p + st[0:n] for p, st in zip(ps, sts)]
        ys = [st[n:2 * n] for st in sts]
        power *= 2
        yield
    out[:] = [p + jnp.dot(p.astype(BF16), _block_diag(y, first), preferred_element_type=F32)
              for p, y in zip(ps, ys)]
    yield


def _weave(main, fillers, period):
    filler = itertools.chain(*fillers)
    for n, _ in enumerate(main, start=1):
        if n % period == 0:
            next(filler, None)
    for _ in filler:
        pass


def _interleave(*generators):
    live = list(generators)
    while live:
        for g in list(live):
            try:
                next(g)
            except StopIteration:
                live.remove(g)


def _odd_layer_kernel(*refs, nb, tt, has_state, prep_chunks, n_earlier, stacked):
    x_ref, gpre_ref, win_ref, wg_ref, cw_ref, ab_ref, on_ref, wout_ref, gpost_ref = refs[:9]
    if has_state:
        hist_ref, s0_ref = refs[9:11]
        refs = refs[11:]
    else:
        refs = refs[9:]
    earlier = refs[:n_earlier]
    refs = refs[n_earlier:]
    (xo_ref, sout_ref, ctail_ref, ctails, qs, ks, vs, zs, gsc, bsc, sbuf, ybuf,
     wbuf, qtbuf, u0buf, qkbuf, kdtbuf, eglbuf) = refs
    w = win_ref.shape[-1] // 4
    n_heads = w // HD_C
    n_rows = nb * tt
    i = pl.program_id(1)

    @pl.when(i == 0)
    def _init():
        ctails[...] = jnp.zeros(ctails.shape, F32)
        if has_state:
            for s in range(nb):
                for g in range(3):
                    ctails[s, g, SUBLANES - 3:SUBLANES, :] = hist_ref[s, :, g * w:(g + 1) * w]
            sbuf[...] = s0_ref[...]
        else:
            sbuf[...] = jnp.zeros(sbuf.shape, F32)

    hb = _pre_norm(x_ref, gpre_ref, nb)

    gl = jnp.dot(hb, wg_ref[...], preferred_element_type=F32)
    bsc[...] = 1.0 / (1.0 + jnp.exp(-gl))
    gsc[...] = -jnp.exp(ab_ref[0:1, :]) * _softplus(gl + ab_ref[1:2, :])

    def project_group(g, dst, n_blocks):
        inv_scale_sq = float(HD_C) if g == 0 else 1.0
        sum_mat = jnp.full((HD_C, HD_C), inv_scale_sq, BF16)
        bw = w // n_blocks
        for blk in range(n_blocks):
            gcols = slice(g * w + blk * bw, g * w + (blk + 1) * bw)
            raw = jnp.dot(hb, win_ref[:, gcols], preferred_element_type=F32)
            for s in range(nb):
                srows = slice(s * tt, (s + 1) * tt)
                c, tail = _causal_dwconv(ctails[s, g, :, blk * bw:(blk + 1) * bw], raw[srows], cw_ref[:, gcols])
                c = _silu(c)
                ctail_ref[s, :, gcols] = tail
                ctails[s, g, :, blk * bw:(blk + 1) * bw] = tail
                if g == 2:
                    dst[srows, blk * bw:(blk + 1) * bw] = c
                else:
                    for h in range(bw // HD_C):
                        ch = c[:, h * HD_C:(h + 1) * HD_C]
                        ss = jnp.dot((ch * ch).astype(BF16), sum_mat, preferred_element_type=F32)
                        dst[srows, blk * bw + h * HD_C:blk * bw + (h + 1) * HD_C] = ch * lax.rsqrt(
                            ss + EPS * inv_scale_sq)
            yield

    def project_gate(n_blocks):
        bw = w // n_blocks
        for blk in range(n_blocks):
            z = jnp.dot(hb, win_ref[:, 3 * w + blk * bw:3 * w + (blk + 1) * bw], preferred_element_type=F32)
            zs[:, blk * bw:(blk + 1) * bw] = _silu(z).astype(zs.dtype)
            yield

    _interleave(project_group(0, qs, 1))
    _interleave(project_group(1, ks, 1))

    first, row, col = _pair_masks()
    incl = row >= col
    strict = row > col
    assert n_heads % 2 == 0 and 2 * CHUNK == HD_C
    pairs = range(n_heads // 2)

    heads = range(n_heads)
    hcols = [slice(h * HD_C, (h + 1) * HD_C) for h in heads]
    n_chunks = n_rows // CHUNK

    def chunk_rows(cidx):
        if isinstance(cidx, int):
            return slice(cidx * CHUNK, (cidx + 1) * CHUNK)
        return pl.ds(pl.multiple_of(cidx * CHUNK, CHUNK), CHUNK)

    def prepare_chunks(step):
        items = []
        duos = []
        for j in range(prep_chunks):
            cidx = step * prep_chunks + j
            rows = chunk_rows(cidx)
            gc = _cumsum_rows(gsc[rows, :])
            gct = gc.T
            bet = bsc[rows, :]
            for h in heads:
                gcol = jnp.broadcast_to(gc[:, n_heads + h:n_heads + h + 1], (CHUNK, HD_C))
                items.append(dict(cidx=cidx, rows=rows, h=h, gcol=gcol,
                                  b=jnp.broadcast_to(bet[:, h:h + 1], (CHUNK, HD_C))))
            for p in pairs:
                ia, ib = items[-n_heads + 2 * p], items[-n_heads + 2 * p + 1]
                g0 = n_heads + 2 * p
                grow = jnp.concatenate([gct[g0:g0 + 1, :], gct[g0 + 1:g0 + 2, :]], axis=1)
                gcol2 = jnp.where(first, ia["gcol"], ib["gcol"])
                duos.append(dict(cidx=cidx, rows=rows, p=p, a=ia, b=ib, beta=jnp.where(first, ia["b"], ib["b"]),
                                 decay=jnp.exp(jnp.where(incl, gcol2 - grow, -jnp.inf))))
        for it in items:
            rows, c = it["rows"], hcols[it["h"]]
            it["eg"] = jnp.exp(it["gcol"])
            it["q"], it["k"] = qs[rows, c], ks[rows, c]
        for it in items:
            kb = it["k"].astype(BF16)
            it["qk_kk"] = lax.dot_general(jnp.concatenate([it["q"].astype(BF16), kb], axis=0), kb,
                                          (((1,), (1,)), ((), ())), preferred_element_type=F32)
        yield
        for du in duos:
            qk_kk = jnp.concatenate([du["a"]["qk_kk"], du["b"]["qk_kk"]], axis=1)
            du["qk"] = qk_kk[0:CHUNK] * du["decay"]
            du["mat"] = jnp.where(strict, du["beta"] * qk_kk[CHUNK:2 * CHUNK] * du["decay"], 0.0)
        tinvs = [None] * len(duos)
        yield from _unit_lower_inverses([du["mat"] for du in duos], tinvs, first, row, col)
        sols = []
        for tinv, du in zip(tinvs, duos):
            rhs = jnp.concatenate(
                [jnp.concatenate([vs[it["rows"], hcols[it["h"]]] * it["b"], it["k"] * (it["b"] * it["eg"])], axis=1)
                 for it in (du["a"], du["b"])], axis=0)
            sols.append(jnp.dot(_block_diag(tinv, first), rhs.astype(BF16), preferred_element_type=F32))
        yield
        for sol, du in zip(sols, duos):
            rows = du["rows"]
            qkbuf[du["p"], rows, :] = du["qk"].astype(BF16)
            for half, it in enumerate((du["a"], du["b"])):
                gcol, c, h = it["gcol"], hcols[it["h"]], it["h"]
                g_last = gcol[CHUNK - 1:CHUNK, :]
                sol_h = sol[half * CHUNK:(half + 1) * CHUNK]
                u0buf[rows, c] = sol_h[:, 0:HD_C]
                wbuf[rows, c] = sol_h[:, HD_C:].astype(BF16)
                qtbuf[rows, c] = (it["q"] * it["eg"]).astype(BF16)
                kdtbuf[h, it["cidx"]] = (it["k"] * jnp.exp(g_last - gcol)).T.astype(BF16)
                eglbuf[it["cidx"], h:h + 1, :] = jnp.exp(g_last)

    chunks_per_stream = tt // CHUNK

    def recur_chunks(step):
        for j in range(prep_chunks):
            cidx = step * prep_chunks + j
            rows = chunk_rows(cidx)
            s = 0 if nb == 1 else cidx // chunks_per_stream
            s_old = [sbuf[s, h] for h in heads]
            ws_qs = [_dot(jnp.concatenate([wbuf[rows, hcols[h]], qtbuf[rows, hcols[h]]], axis=0), s_old[h])
                     for h in heads]
            yield
            u = [u0buf[rows, hcols[h]] - ws_qs[h][0:CHUNK] for h in heads]
            qku = [jnp.dot(_block_diag(qkbuf[p, rows, :], first),
                           jnp.concatenate([u[2 * p], u[2 * p + 1]], axis=0).astype(BF16),
                           preferred_element_type=F32) for p in pairs]
            o = [ws_qs[h][CHUNK:2 * CHUNK] + qku[h // 2][(h % 2) * CHUNK:(h % 2 + 1) * CHUNK] for h in heads]
            for h in heads:
                sbuf[s, h] = eglbuf[cidx, h:h + 1, :] * s_old[h] + _dot(kdtbuf[h, cidx], u[h])
            yield
            for h in heads:
                ms = jnp.mean(o[h] * o[h], axis=-1, keepdims=True)
                on = o[h] * lax.rsqrt(ms + EPS) * on_ref[...]
                ybuf[rows, hcols[h]] = (on * zs[rows, hcols[h]].astype(F32)).astype(ybuf.dtype)

    n_steps = n_chunks // prep_chunks
    _weave(prepare_chunks(0), [project_group(2, vs, PROJ_BLOCKS), project_gate(PROJ_BLOCKS)], period=1)

    def pipelined(step, carry):
        _interleave(prepare_chunks(step), recur_chunks(step - 1))
        return carry

    lax.fori_loop(1, n_steps, pipelined, 0)
    _interleave(recur_chunks(n_steps - 1))
    if stacked:
        @pl.when(i == pl.num_programs(1) - 1)
        def _pass_through():
            for j in range(n_earlier):
                sout_ref[j] = earlier[j][...]
        sout_ref[n_earlier] = sbuf[...]
    else:
        sout_ref[...] = sbuf[...]
    _project_out(ybuf, wout_ref, gpost_ref, x_ref, xo_ref)


def _odd_layer(x, hist, s0, g_pre, g_post, w_in_bf, w_gate_bf, conv_w, ab, onorm, w_out_bf, slot, n_slots, earlier_s):
    b, t, d = x.shape
    w = w_in_bf.shape[1] // 4
    n_heads = w // HD_C
    has_state = s0 is not None
    last = slot == n_slots - 1
    assert len(earlier_s) == slot
    assert t % CHUNK == 0 and (t <= ROW_TILE or t % ROW_TILE == 0)
    state_copies = 1 + 2 * (n_slots if last else 1) + 2 * (slot if last else 0) + (2 if has_state else 0)
    nb, tt = _streams_per_step(b, t, n_heads * HD_C * HD_C * 4 * state_copies)
    n_rows = nb * tt
    n_chunks = n_rows // CHUNK
    prep_chunks = max(c for c in (1, 2, 4) if n_chunks % c == 0)
    consts = [g_pre.reshape(1, d), w_in_bf, w_gate_bf, conv_w, ab, onorm, w_out_bf, g_post.reshape(1, d)]
    in_specs = [pl.BlockSpec((nb, tt, d), lambda bi, i: (bi, i, 0))] + [_resident(c.shape) for c in consts]
    args = [x] + consts
    if has_state:
        in_specs += [_layer_slot_in(hist, nb, slot), _layer_slot_in(s0, nb, slot)]
        args += [hist, s0]
    s_spec, n_layers, earlier_specs = _per_stream_out((n_heads, HD_C, HD_C), nb, earlier_s, last)
    s_shape = jax.ShapeDtypeStruct(((n_layers,) if last else ()) + (b, n_heads, HD_C, HD_C), F32)
    n_earlier = len(earlier_specs)
    in_specs += earlier_specs
    args += list(earlier_s[:n_earlier])
    x_new, s_new, ctail = pl.pallas_call(
        functools.partial(_odd_layer_kernel, nb=nb, tt=tt, has_state=has_state, prep_chunks=prep_chunks,
                          n_earlier=n_earlier, stacked=last),
        grid=(b // nb, t // tt), in_specs=in_specs,
        out_specs=[pl.BlockSpec((nb, tt, d), lambda bi, i: (bi, i, 0)),
                   s_spec,
                   pl.BlockSpec((nb, SUBLANES, 3 * w), lambda bi, i: (bi, 0, 0))],
        out_shape=[jax.ShapeDtypeStruct((b, t, d), F32),
                   s_shape,
                   jax.ShapeDtypeStruct((b, SUBLANES, 3 * w), F32)],
        scratch_shapes=[pltpu.VMEM((nb, 3, SUBLANES, w), F32),
                        pltpu.VMEM((n_rows, w), F32), pltpu.VMEM((n_rows, w), F32), pltpu.VMEM((n_rows, w), F32),
                        pltpu.VMEM((n_rows, w), BF16),
                        pltpu.VMEM((n_rows, LANES), F32), pltpu.VMEM((n_rows, LANES), F32),
                        pltpu.VMEM((nb, n_heads, HD_C, HD_C), F32),
                        pltpu.VMEM((n_rows, w), BF16),
                        pltpu.VMEM((n_rows, w), BF16), pltpu.VMEM((n_rows, w), BF16), pltpu.VMEM((n_rows, w), F32),
                        pltpu.VMEM((n_heads // 2, n_rows, 2 * CHUNK), BF16),
                        pltpu.VMEM((n_heads, n_chunks, HD_C, CHUNK), BF16),
                        pltpu.VMEM((n_chunks, n_heads, HD_C), F32)],
        compiler_params=_params(), name="odd_layer")(*args)
    return x_new, ctail[:, SUBLANES - 3:], s_new


def _cast_kernel(w_ref, *rest):
    if len(rest) == 2:
        scale_ref, o_ref = rest
        o_ref[...] = (w_ref[...] * scale_ref[...]).astype(o_ref.dtype)
    else:
        (o_ref,) = rest
        o_ref[...] = w_ref[...].astype(o_ref.dtype)


def _to_bf16(w_all, layer, n_cols, col_scale=None):
    k = w_all.shape[1]
    assert n_cols % CAST_COLS == 0
    in_specs = [pl.BlockSpec((None, k, CAST_COLS), lambda j: (layer, 0, j))]
    args = [w_all]
    if col_scale is not None:
        in_specs.append(pl.BlockSpec((1, CAST_COLS), lambda j: (0, j)))
        args.append(col_scale.reshape(1, n_cols))
    return pl.pallas_call(
        _cast_kernel, grid=(n_cols // CAST_COLS,), in_specs=in_specs,
        out_specs=pl.BlockSpec((k, CAST_COLS), lambda j: (0, j)),
        out_shape=jax.ShapeDtypeStruct((k, n_cols), BF16),
        compiler_params=pltpu.CompilerParams(dimension_semantics=("parallel",)), name="to_bf16")(*args)


def _gate_cast_kernel(w_ref, o_ref, *, n_gate):
    lane = lax.broadcasted_iota(jnp.int32, w_ref.shape, 1)
    o_ref[...] = jnp.where(lane < n_gate, w_ref[...], 0.0).astype(o_ref.dtype)


def _gate_to_bf16(w_all, layer, first_col, n_gate):
    k = w_all.shape[1]
    assert first_col % LANES == 0 and n_gate <= LANES and first_col + n_gate == w_all.shape[2]
    return pl.pallas_call(
        functools.partial(_gate_cast_kernel, n_gate=n_gate), grid=(1,),
        in_specs=[pl.BlockSpec((None, k, LANES), lambda j: (layer, 0, first_col // LANES))],
        out_specs=pl.BlockSpec((k, LANES), lambda j: (0, 0)),
        out_shape=jax.ShapeDtypeStruct((k, LANES), BF16), name="gate_to_bf16")(w_all)


def _band_bias(table):
    n_heads = table.shape[0]
    assert CHUNK - 1 <= REL_CLIP < BAND - 1
    low = table[:, REL_CLIP - (CHUNK - 1):]
    high = jnp.broadcast_to(table[:, 2 * REL_CLIP:], (n_heads, BAND - 1 - REL_CLIP))
    r = jnp.concatenate([low, high], axis=1)[:, ::-1].astype(F32) * LOG2E
    length = r.shape[1]
    padded = jnp.pad(r, ((0, 0), (0, 1)))
    shifted = jnp.tile(padded, (1, CHUNK))[:, :CHUNK * length].reshape(n_heads, CHUNK, length)
    bias = shifted[:, :, CHUNK - 1:CHUNK - 1 + BAND]
    return bias.reshape(n_heads // 2, 2 * CHUNK, BAND)


def kernel(x_prompt, x_sample, cache_conv_a, cache_k_b, cache_v_b, state_conv_c, state_s_c, norm_pre, norm_post, w_in_even, conv_w_a, rel_bias_b, w_out_even, w_in_odd, conv_w_c, a_log_c, dt_bias_c, out_norm_c, w_out_odd):
    depth = norm_pre.shape[0]
    n_even, n_odd = w_in_even.shape[0], w_in_odd.shape[0]
    n_heads_b = cache_k_b.shape[-2]
    kc_all = cache_k_b.reshape(cache_k_b.shape[:3] + (-1,)).astype(BF16)
    vc_all = cache_v_b.reshape(cache_v_b.shape[:3] + (-1,)).astype(BF16)
    yp, ys = x_prompt, x_sample
    conv_a_p, conv_a_s, conv_c_p, conv_c_s = [], [], [], []
    kv_p, kv_s, s_p, s_s = [], [], [], []
    for layer in range(depth):
        i = layer // 2
        g_pre, g_post = norm_pre[layer], norm_post[layer]
        if layer % 2 == 0:
            n_in = w_in_even.shape[2]
            w_blk = n_in // 8
            col_scale = jnp.ones((n_in,), F32).at[4 * w_blk:5 * w_blk].set(HD_B ** -0.5 * LOG2E)
            w_in_bf = _to_bf16(w_in_even, i, n_in, col_scale)
            w_out_bf = _to_bf16(w_out_even, i, w_out_even.shape[2])
            shared = (g_pre, g_post, w_in_bf, conv_w_a[i], _band_bias(rel_bias_b[i]), w_out_bf, i, n_even)
            yp, hst, kv = _even_layer(yp, None, None, None, *shared, kv_p)
            kv_p.append(kv)
            conv_a_p.append(hst)
            ys, hst, kv = _even_layer(ys, cache_conv_a, kc_all, vc_all, *shared, kv_s)
            kv_s.append(kv)
            conv_a_s.append(hst)
        else:
            n_heads = a_log_c.shape[1]
            n_main = w_in_odd.shape[2] - 2 * n_heads
            w_main_bf = _to_bf16(w_in_odd, i, n_main)
            w_gate_bf = _gate_to_bf16(w_in_odd, i, n_main, 2 * n_heads)
            pad = (n_heads, LANES - 2 * n_heads)
            ab = jnp.stack([jnp.pad(a_log_c[i], pad), jnp.pad(dt_bias_c[i], pad)]).astype(F32)
            shared = (g_pre, g_post, w_main_bf, w_gate_bf, conv_w_c[i], ab, out_norm_c[i].reshape(1, -1),
                      _to_bf16(w_out_odd, i, w_out_odd.shape[2]), i, n_odd)
            yp, hst, s_new = _odd_layer(yp, None, None, *shared, s_p)
            s_p.append(s_new)
            conv_c_p.append(hst)
            ys, hst, s_new = _odd_layer(ys, state_conv_c, state_s_c, *shared, s_s)
            s_s.append(s_new)
            conv_c_s.append(hst)

    def heads_apart(kv):
        return kv.reshape(kv.shape[:-1] + (n_heads_b, kv.shape[-1] // n_heads_b))

    return (yp, ys,
            jnp.stack(conv_a_p), heads_apart(kv_p[-1][0]), heads_apart(kv_p[-1][1]), jnp.stack(conv_c_p), s_p[-1],
            jnp.stack(conv_a_s), heads_apart(kv_s[-1][0]), heads_apart(kv_s[-1][1]), jnp.stack(conv_c_s), s_s[-1])
```

```python
import functools
import itertools
import math

import jax
import jax.numpy as jnp
from jax import lax
from jax.experimental import pallas as pl
from jax.experimental.pallas import tpu as pltpu

F32 = jnp.float32
BF16 = jnp.bfloat16

EPS = 1e-6
LOG2E = math.log2(math.e)
CHUNK = 64
BAND_PAST = 8 * CHUNK
BAND = BAND_PAST + CHUNK
REL_CLIP = 4 * CHUNK
HD_B = 64
HD_C = 128
LANES = 128
SUBLANES = 8
VMEM_LIMIT_BYTES = 56 * 1024 * 1024
STATE_VMEM_BUDGET_BYTES = 18 * 1024 * 1024
ROW_TILE = 512
CAST_COLS = 512
PROJ_BLOCKS = 4


def _silu(x):
    return x * (1.0 / (1.0 + jnp.exp(-x)))


def _softplus(x):
    return jnp.maximum(x, 0.0) + jnp.log1p(jnp.exp(-jnp.abs(x)))


def _dot(a, b):
    return jnp.dot(a.astype(BF16), b.astype(BF16), preferred_element_type=F32)


def _dot_t(a, b_t):
    return lax.dot_general(a, b_t, (((1,), (1,)), ((), ())), preferred_element_type=F32)


def _load_rows(ref, nb):
    parts = [ref[s] for s in range(nb)]
    return parts[0] if nb == 1 else jnp.concatenate(parts, axis=0)


def _pre_norm(x_ref, gain_ref, nb):
    x = _load_rows(x_ref, nb)
    ms = jnp.mean(x * x, axis=-1, keepdims=True)
    return (x * lax.rsqrt(ms + EPS) * gain_ref[...]).astype(BF16)


def _project_out(ybuf, wout_ref, gain_ref, x_ref, xo_ref):
    nb, tt, _ = x_ref.shape
    out = jnp.dot(ybuf[...], wout_ref[...], preferred_element_type=F32)
    ms = jnp.mean(out * out, axis=-1, keepdims=True)
    delta = out * lax.rsqrt(ms + EPS) * gain_ref[...]
    for s in range(nb):
        xo_ref[s] = x_ref[s] + delta[s * tt:(s + 1) * tt]


def _causal_dwconv(prev_rows, new_rows, taps):
    tt = new_rows.shape[0]
    width = taps.shape[0]
    assert width <= SUBLANES
    full = jnp.concatenate([prev_rows, new_rows], axis=0)
    sums = [None, None]
    shifted = full
    for delay in range(width):
        if delay and delay % 2 == 0:
            shifted = pltpu.roll(shifted, 2, axis=0)
        term = shifted * taps[width - 1 - delay:width - delay]
        sums[delay % 2] = term if sums[delay % 2] is None else sums[delay % 2] + term
    acc = sums[0] if sums[1] is None else sums[0] + pltpu.roll(sums[1], 1, axis=0)
    return acc[SUBLANES:, :], full[tt:tt + SUBLANES, :]


def _streams_per_step(b, t, state_bytes_per_stream):
    tt = min(t, ROW_TILE)
    nb = max(1, min(ROW_TILE // tt, STATE_VMEM_BUDGET_BYTES // state_bytes_per_stream))
    while b % nb:
        nb -= 1
    return nb, tt


def _resident(shape):
    return pl.BlockSpec(shape, lambda bi, i: (0,) * len(shape), pipeline_mode=pl.Buffered(1))


def _params():
    return pltpu.CompilerParams(dimension_semantics=("parallel", "arbitrary"), vmem_limit_bytes=VMEM_LIMIT_BYTES)


def _even_layer_kernel(*refs, nb, tt, has_cache, n_earlier, stacked):
    x_ref, gpre_ref, win_ref, cw_ref, bias_ref, wout_ref, gpost_ref = refs[:7]
    if has_cache:
        hist_ref, kc_ref, vc_ref = refs[7:10]
        refs = refs[10:]
    else:
        refs = refs[7:]
    earlier = refs[:2 * n_earlier]
    refs = refs[2 * n_earlier:]
    xo_ref, utail_ref, knew_ref, vnew_ref, utails, kbuf, vbuf, qbuf, zbuf, ybuf = refs
    w = win_ref.shape[-1] // 8
    i = pl.program_id(1)
    if stacked:
        @pl.when(i == pl.num_programs(1) - 1)
        def _pass_through():
            for j in range(n_earlier):
                knew_ref[j] = earlier[2 * j][...]
                vnew_ref[j] = earlier[2 * j + 1][...]
        knew_ref, vnew_ref = knew_ref.at[n_earlier], vnew_ref.at[n_earlier]

    @pl.when(i == 0)
    def _init():
        utails[...] = jnp.zeros(utails.shape, F32)
        if has_cache:
            for s in range(nb):
                utails[s, SUBLANES - 2:SUBLANES, :] = hist_ref[s]
            kbuf[:, 0:BAND_PAST, :] = kc_ref[...]
            vbuf[:, 0:BAND_PAST, :] = vc_ref[...]
        else:
            kbuf[:, 0:BAND_PAST, :] = jnp.zeros((nb, BAND_PAST, w), BF16)
            vbuf[:, 0:BAND_PAST, :] = jnp.zeros((nb, BAND_PAST, w), BF16)

    hb = _pre_norm(x_ref, gpre_ref, nb)

    def proj(j):
        return jnp.dot(hb, win_ref[:, j * w:(j + 1) * w], preferred_element_type=F32)

    u = proj(1) * proj(2)
    gate = proj(0) * _silu(proj(3))
    for s in range(nb):
        srows = slice(s * tt, (s + 1) * tt)
        conv, tail = _causal_dwconv(utails[s], u[srows], cw_ref[...])
        utail_ref[s] = tail
        utails[s] = tail
        ybuf[srows, 0:w] = (gate[srows] * conv).astype(ybuf.dtype)

    qbuf[...] = proj(4).astype(qbuf.dtype)
    k = proj(5)
    v = proj(6)
    for s in range(nb):
        srows = slice(s * tt, (s + 1) * tt)
        knew_ref[s] = k[srows]
        vnew_ref[s] = v[srows]
        kbuf[s, BAND_PAST:BAND_PAST + tt, :] = k[srows].astype(kbuf.dtype)
        vbuf[s, BAND_PAST:BAND_PAST + tt, :] = v[srows].astype(vbuf.dtype)
    zbuf[...] = _silu(proj(7)).astype(zbuf.dtype)

    chunks_per_stream = tt // CHUNK
    n_pairs = w // LANES
    lane = lax.broadcasted_iota(jnp.int32, (CHUNK, LANES), 1)
    first_head = lane < HD_B

    def attend(first_tile):
        def band_start(c):
            missing = max(BAND_PAST - CHUNK * c, 0) if first_tile else 0
            return missing // LANES * LANES, missing % LANES

        def scores(s, c, hp):
            cols = slice(hp * LANES, (hp + 1) * LANES)
            lo, masked = band_start(c)
            r0 = s * tt + c * CHUNK
            qp = qbuf[r0:r0 + CHUNK, cols]
            zero = jnp.zeros_like(qp)
            q2 = jnp.concatenate([jnp.where(first_head, qp, zero), jnp.where(first_head, zero, qp)], axis=0)
            kp = kbuf[s, c * CHUNK + lo:c * CHUNK + BAND, cols]
            sc = lax.dot_general(q2, kp, (((1,), (1,)), ((), ())), preferred_element_type=F32)
            sc = sc + bias_ref[hp, :, lo:BAND]
            if masked:
                key_idx = lax.broadcasted_iota(jnp.int32, sc.shape, 1)
                sc = jnp.where(key_idx >= masked, sc, -jnp.inf)
            return sc

        def weighted_values(s, c, hp, sc):
            cols = slice(hp * LANES, (hp + 1) * LANES)
            lo, _ = band_start(c)
            m = jnp.max(sc, axis=-1, keepdims=True)
            e = jnp.exp2(sc - m)
            l = jnp.sum(e, axis=-1, keepdims=True)
            vp = vbuf[s, c * CHUNK + lo:c * CHUNK + BAND, cols]
            pv = jnp.dot(e.astype(BF16), vp, preferred_element_type=F32) / l
            return jnp.where(first_head, pv[0:CHUNK], pv[CHUNK:2 * CHUNK])

        items = [(s, c, hp) for s in range(nb) for c in range(chunks_per_stream) for hp in range(n_pairs)]
        skew = 3
        pending, outs = {}, {}
        for step in range(len(items) + skew):
            if step < len(items):
                pending[step] = scores(*items[step])
            j = step - skew
            if j >= 0:
                s, c, hp = items[j]
                outs[hp] = weighted_values(s, c, hp, pending.pop(j))
                if hp == n_pairs - 1:
                    rows = slice(s * tt + c * CHUNK, s * tt + (c + 1) * CHUNK)
                    o = jnp.concatenate([outs[x] for x in range(n_pairs)], axis=1)
                    ybuf[rows, w:2 * w] = (o * zbuf[rows, :].astype(F32)).astype(ybuf.dtype)

    if has_cache:
        attend(False)
    else:
        pl.when(i == 0)(functools.partial(attend, True))
        pl.when(i > 0)(functools.partial(attend, False))
    for s in range(nb):
        kbuf[s, 0:BAND_PAST, :] = kbuf[s, tt:tt + BAND_PAST, :]
        vbuf[s, 0:BAND_PAST, :] = vbuf[s, tt:tt + BAND_PAST, :]

    _project_out(ybuf, wout_ref, gpost_ref, x_ref, xo_ref)


def _layer_slot_in(stacked, nb, slot):
    rest = stacked.shape[2:]
    return pl.BlockSpec((None, nb) + rest, lambda bi, i: (slot, bi) + (0,) * len(rest))


def _per_stream_out(shape, nb, earlier, last):
    zeros = (0,) * len(shape)
    own = pl.BlockSpec((nb,) + shape, lambda bi, i: (bi,) + zeros)
    if not last:
        return own, None, []
    n_layers = len(earlier) + 1
    spec = pl.BlockSpec((n_layers, nb) + shape, lambda bi, i: (0, bi) + zeros)
    return spec, n_layers, [own] * len(earlier)


def _even_layer(x, hist, kc, vc, g_pre, g_post, w_in_bf, conv_w, bias2, w_out_bf, slot, n_slots, earlier_kv):
    b, t, d = x.shape
    w = w_in_bf.shape[1] // 8
    has_cache = kc is not None
    assert t % CHUNK == 0 and (t <= ROW_TILE or t % ROW_TILE == 0)
    band_bytes = 2 * (BAND_PAST + min(t, ROW_TILE)) * w * 2
    state_bytes = band_bytes + (2 * 2 * BAND_PAST * w * 2 if has_cache else 0)
    nb, tt = _streams_per_step(b, t, state_bytes)
    n_rows = nb * tt
    consts = [g_pre.reshape(1, d), w_in_bf, conv_w, bias2, w_out_bf, g_post.reshape(1, d)]
    in_specs = [pl.BlockSpec((nb, tt, d), lambda bi, i: (bi, i, 0))] + [_resident(c.shape) for c in consts]
    args = [x] + consts
    if has_cache:
        in_specs += [_layer_slot_in(hist, nb, slot), _layer_slot_in(kc, nb, slot), _layer_slot_in(vc, nb, slot)]
        args += [hist, kc, vc]
    assert tt == min(BAND_PAST, t)
    last = slot == n_slots - 1
    assert len(earlier_kv) == slot
    kv_spec, n_layers, earlier_specs = _per_stream_out((tt, w), nb, earlier_kv, last)
    kv_shape = jax.ShapeDtypeStruct(((n_layers,) if last else ()) + (b, tt, w), F32)
    n_earlier = len(earlier_specs)
    for k_prev, v_prev in earlier_kv[:n_earlier]:
        in_specs += earlier_specs[:1] * 2
        args += [k_prev, v_prev]
    x_new, utail, k_out, v_out = pl.pallas_call(
        functools.partial(_even_layer_kernel, nb=nb, tt=tt, has_cache=has_cache, n_earlier=n_earlier, stacked=last),
        grid=(b // nb, t // tt), in_specs=in_specs,
        out_specs=[pl.BlockSpec((nb, tt, d), lambda bi, i: (bi, i, 0)),
                   pl.BlockSpec((nb, SUBLANES, w), lambda bi, i: (bi, 0, 0)),
                   kv_spec, kv_spec],
        out_shape=[jax.ShapeDtypeStruct((b, t, d), F32),
                   jax.ShapeDtypeStruct((b, SUBLANES, w), F32),
                   kv_shape, kv_shape],
        scratch_shapes=[pltpu.VMEM((nb, SUBLANES, w), F32),
                        pltpu.VMEM((nb, BAND_PAST + tt, w), BF16), pltpu.VMEM((nb, BAND_PAST + tt, w), BF16),
                        pltpu.VMEM((n_rows, w), BF16), pltpu.VMEM((n_rows, w), BF16),
                        pltpu.VMEM((n_rows, 2 * w), BF16)],
        compiler_params=_params(), name="even_layer")(*args)
    return x_new, utail[:, SUBLANES - 2:], (k_out, v_out)


def _cumsum_rows(x):
    n = x.shape[0]
    row = lax.broadcasted_iota(jnp.int32, x.shape, 0)
    shift = 1
    while shift < n:
        x = x + jnp.where(row >= shift, pltpu.roll(x, shift, axis=0), 0.0)
        shift *= 2
    return x


def _pair_masks():
    row = lax.broadcasted_iota(jnp.int32, (CHUNK, 2 * CHUNK), 0)
    lane = lax.broadcasted_iota(jnp.int32, (CHUNK, 2 * CHUNK), 1)
    first = lane < CHUNK
    col = jnp.where(first, lane, lane - CHUNK)
    return first, row, col


def _block_diag(pk, first):
    tiled = jnp.concatenate([pk, pk], axis=0).astype(BF16)
    keep = jnp.concatenate([first, jnp.logical_not(first)], axis=0)
    return jnp.where(keep, tiled, jnp.zeros_like(tiled))


def _unit_lower_inverses(mats, out, first, row, col):
    n = mats[0].shape[0]
    eye = jnp.where(row == col, 1.0, 0.0)
    xs = [-a for a in mats]
    ps = [eye + x for x in xs]
    ys = [jnp.dot(x.astype(BF16), _block_diag(x, first), preferred_element_type=F32) for x in xs]
    yield
    power = 2
    while 2 * power < n:
        sts = [jnp.dot(jnp.concatenate([p, y], axis=0).astype(BF16), _block_diag(y, first),
                       preferred_element_type=F32) for p, y in zip(ps, ys)]
        ps = [p + st[0:n] for p, st in zip(ps, sts)]
        ys = [st[n:2 * n] for st in sts]
        power *= 2
        yield
    out[:] = [p + jnp.dot(p.astype(BF16), _block_diag(y, first), preferred_element_type=F32)
              for p, y in zip(ps, ys)]
    yield


def _weave(main, fillers, period):
    filler = itertools.chain(*fillers)
    for n, _ in enumerate(main, start=1):
        if n % period == 0:
            next(filler, None)
    for _ in filler:
        pass


def _interleave(*generators):
    live = list(generators)
    while live:
        for g in list(live):
            try:
                next(g)
            except StopIteration:
                live.remove(g)


def _odd_layer_kernel(*refs, nb, tt, has_state, prep_chunks, n_earlier, stacked):
    x_ref, gpre_ref, win_ref, wg_ref, cw_ref, ab_ref, on_ref, wout_ref, gpost_ref = refs[:9]
    if has_state:
        hist_ref, s0_ref = refs[9:11]
        refs = refs[11:]
    else:
        refs = refs[9:]
    earlier = refs[:n_earlier]
    refs = refs[n_earlier:]
    (xo_ref, sout_ref, ctail_ref, ctails, qs, ks, vs, zs, gsc, bsc, sbuf, ybuf,
     wbuf, qtbuf, u0buf, qkbuf, kdtbuf, eglbuf) = refs
    w = win_ref.shape[0] // 4
    n_heads = w // HD_C
    n_rows = nb * tt
    i = pl.program_id(1)

    @pl.when(i == 0)
    def _init():
        ctails[...] = jnp.zeros(ctails.shape, F32)
        if has_state:
            for s in range(nb):
                for g in range(3):
                    ctails[s, g, SUBLANES - 3:SUBLANES, :] = hist_ref[s, :, g * w:(g + 1) * w]
            sbuf[...] = s0_ref[...]
        else:
            sbuf[...] = jnp.zeros(sbuf.shape, F32)

    hb = _pre_norm(x_ref, gpre_ref, nb)

    gl = _dot_t(hb, wg_ref[...])
    bsc[...] = 1.0 / (1.0 + jnp.exp(-gl))
    gsc[...] = -jnp.exp(ab_ref[0:1, :]) * _softplus(gl + ab_ref[1:2, :])

    def project_group(g, dst, n_blocks):
        inv_scale_sq = float(HD_C) if g == 0 else 1.0
        sum_mat = jnp.full((HD_C, HD_C), inv_scale_sq, BF16)
        bw = w // n_blocks
        for blk in range(n_blocks):
            gcols = slice(g * w + blk * bw, g * w + (blk + 1) * bw)
            raw = _dot_t(hb, win_ref[gcols, :])
            for s in range(nb):
                srows = slice(s * tt, (s + 1) * tt)
                c, tail = _causal_dwconv(ctails[s, g, :, blk * bw:(blk + 1) * bw], raw[srows], cw_ref[:, gcols])
                c = _silu(c)
                ctail_ref[s, :, gcols] = tail
                ctails[s, g, :, blk * bw:(blk + 1) * bw] = tail
                if g == 2:
                    dst[srows, blk * bw:(blk + 1) * bw] = c
                else:
                    for h in range(bw // HD_C):
                        ch = c[:, h * HD_C:(h + 1) * HD_C]
                        ss = jnp.dot((ch * ch).astype(BF16), sum_mat, preferred_element_type=F32)
                        dst[srows, blk * bw + h * HD_C:blk * bw + (h + 1) * HD_C] = ch * lax.rsqrt(
                            ss + EPS * inv_scale_sq)
            yield

    def project_gate(n_blocks):
        bw = w // n_blocks
        for blk in range(n_blocks):
            z = _dot_t(hb, win_ref[3 * w + blk * bw:3 * w + (blk + 1) * bw, :])
            zs[:, blk * bw:(blk + 1) * bw] = _silu(z).astype(zs.dtype)
            yield

    _interleave(project_group(0, qs, 1))
    _interleave(project_group(1, ks, 1))

    first, row, col = _pair_masks()
    incl = row >= col
    strict = row > col
    assert n_heads % 2 == 0 and 2 * CHUNK == HD_C
    pairs = range(n_heads // 2)

    heads = range(n_heads)
    hcols = [slice(h * HD_C, (h + 1) * HD_C) for h in heads]
    n_chunks = n_rows // CHUNK

    def chunk_rows(cidx):
        if isinstance(cidx, int):
            return slice(cidx * CHUNK, (cidx + 1) * CHUNK)
        return pl.ds(pl.multiple_of(cidx * CHUNK, CHUNK), CHUNK)

    def prepare_chunks(step):
        items = []
        duos = []
        for j in range(prep_chunks):
            cidx = step * prep_chunks + j
            rows = chunk_rows(cidx)
            gc = _cumsum_rows(gsc[rows, :])
            gct = gc.T
            bet = bsc[rows, :]
            for h in heads:
                gcol = jnp.broadcast_to(gc[:, n_heads + h:n_heads + h + 1], (CHUNK, HD_C))
                items.append(dict(cidx=cidx, rows=rows, h=h, gcol=gcol,
                                  b=jnp.broadcast_to(bet[:, h:h + 1], (CHUNK, HD_C))))
            for p in pairs:
                ia, ib = items[-n_heads + 2 * p], items[-n_heads + 2 * p + 1]
                g0 = n_heads + 2 * p
                grow = jnp.concatenate([gct[g0:g0 + 1, :], gct[g0 + 1:g0 + 2, :]], axis=1)
                gcol2 = jnp.where(first, ia["gcol"], ib["gcol"])
                duos.append(dict(cidx=cidx, rows=rows, p=p, a=ia, b=ib, beta=jnp.where(first, ia["b"], ib["b"]),
                                 decay=jnp.exp(jnp.where(incl, gcol2 - grow, -jnp.inf))))
        for it in items:
            rows, c = it["rows"], hcols[it["h"]]
            it["eg"] = jnp.exp(it["gcol"])
            it["q"], it["k"] = qs[rows, c], ks[rows, c]
        for it in items:
            kb = it["k"].astype(BF16)
            it["qk_kk"] = lax.dot_general(jnp.concatenate([it["q"].astype(BF16), kb], axis=0), kb,
                                          (((1,), (1,)), ((), ())), preferred_element_type=F32)
        yield
        for du in duos:
            qk_kk = jnp.concatenate([du["a"]["qk_kk"], du["b"]["qk_kk"]], axis=1)
            du["qk"] = qk_kk[0:CHUNK] * du["decay"]
            du["mat"] = jnp.where(strict, du["beta"] * qk_kk[CHUNK:2 * CHUNK] * du["decay"], 0.0)
        tinvs = [None] * len(duos)
        yield from _unit_lower_inverses([du["mat"] for du in duos], tinvs, first, row, col)
        sols = []
        for tinv, du in zip(tinvs, duos):
            rhs = jnp.concatenate(
                [jnp.concatenate([vs[it["rows"], hcols[it["h"]]] * it["b"], it["k"] * (it["b"] * it["eg"])], axis=1)
                 for it in (du["a"], du["b"])], axis=0)
            sols.append(jnp.dot(_block_diag(tinv, first), rhs.astype(BF16), preferred_element_type=F32))
        yield
        for sol, du in zip(sols, duos):
            rows = du["rows"]
            qkbuf[du["p"], rows, :] = du["qk"].astype(BF16)
            for half, it in enumerate((du["a"], du["b"])):
                gcol, c, h = it["gcol"], hcols[it["h"]], it["h"]
                g_last = gcol[CHUNK - 1:CHUNK, :]
                sol_h = sol[half * CHUNK:(half + 1) * CHUNK]
                u0buf[rows, c] = sol_h[:, 0:HD_C]
                wbuf[rows, c] = sol_h[:, HD_C:].astype(BF16)
                qtbuf[rows, c] = (it["q"] * it["eg"]).astype(BF16)
                kdtbuf[h, it["cidx"]] = (it["k"] * jnp.exp(g_last - gcol)).T.astype(BF16)
                eglbuf[it["cidx"], h:h + 1, :] = jnp.exp(g_last)

    chunks_per_stream = tt // CHUNK

    def recur_chunks(step):
        for j in range(prep_chunks):
            cidx = step * prep_chunks + j
            rows = chunk_rows(cidx)
            s = 0 if nb == 1 else cidx // chunks_per_stream
            s_old = [sbuf[s, h] for h in heads]
            ws_qs = [_dot(jnp.concatenate([wbuf[rows, hcols[h]], qtbuf[rows, hcols[h]]], axis=0), s_old[h])
                     for h in heads]
            yield
            u = [u0buf[rows, hcols[h]] - ws_qs[h][0:CHUNK] for h in heads]
            qku = [jnp.dot(_block_diag(qkbuf[p, rows, :], first),
                           jnp.concatenate([u[2 * p], u[2 * p + 1]], axis=0).astype(BF16),
                           preferred_element_type=F32) for p in pairs]
            o = [ws_qs[h][CHUNK:2 * CHUNK] + qku[h // 2][(h % 2) * CHUNK:(h % 2 + 1) * CHUNK] for h in heads]
            for h in heads:
                sbuf[s, h] = eglbuf[cidx, h:h + 1, :] * s_old[h] + _dot(kdtbuf[h, cidx], u[h])
            yield
            for h in heads:
                ms = jnp.mean(o[h] * o[h], axis=-1, keepdims=True)
                on = o[h] * lax.rsqrt(ms + EPS) * on_ref[...]
                ybuf[rows, hcols[h]] = (on * zs[rows, hcols[h]].astype(F32)).astype(ybuf.dtype)

    n_steps = n_chunks // prep_chunks
    _weave(prepare_chunks(0), [project_group(2, vs, PROJ_BLOCKS), project_gate(PROJ_BLOCKS)], period=1)

    def pipelined(step, carry):
        _interleave(prepare_chunks(step), recur_chunks(step - 1))
        return carry

    lax.fori_loop(1, n_steps, pipelined, 0)
    _interleave(recur_chunks(n_steps - 1))
    if stacked:
        @pl.when(i == pl.num_programs(1) - 1)
        def _pass_through():
            for j in range(n_earlier):
                sout_ref[j] = earlier[j][...]
        sout_ref[n_earlier] = sbuf[...]
    else:
        sout_ref[...] = sbuf[...]
    _project_out(ybuf, wout_ref, gpost_ref, x_ref, xo_ref)


def _odd_layer(x, hist, s0, g_pre, g_post, w_in_bf, w_gate_bf, conv_w, ab, onorm, w_out_bf, slot, n_slots, earlier_s):
    b, t, d = x.shape
    w = w_in_bf.shape[0] // 4
    n_heads = w // HD_C
    has_state = s0 is not None
    last = slot == n_slots - 1
    assert len(earlier_s) == slot
    assert t % CHUNK == 0 and (t <= ROW_TILE or t % ROW_TILE == 0)
    state_copies = 1 + 2 * (n_slots if last else 1) + 2 * (slot if last else 0) + (2 if has_state else 0)
    nb, tt = _streams_per_step(b, t, n_heads * HD_C * HD_C * 4 * state_copies)
    n_rows = nb * tt
    n_chunks = n_rows // CHUNK
    prep_chunks = max(c for c in (1, 2, 4) if n_chunks % c == 0)
    consts = [g_pre.reshape(1, d), w_in_bf, w_gate_bf, conv_w, ab, onorm, w_out_bf, g_post.reshape(1, d)]
    in_specs = [pl.BlockSpec((nb, tt, d), lambda bi, i: (bi, i, 0))] + [_resident(c.shape) for c in consts]
    args = [x] + consts
    if has_state:
        in_specs += [_layer_slot_in(hist, nb, slot), _layer_slot_in(s0, nb, slot)]
        args += [hist, s0]
    s_spec, n_layers, earlier_specs = _per_stream_out((n_heads, HD_C, HD_C), nb, earlier_s, last)
    s_shape = jax.ShapeDtypeStruct(((n_layers,) if last else ()) + (b, n_heads, HD_C, HD_C), F32)
    n_earlier = len(earlier_specs)
    in_specs += earlier_specs
    args += list(earlier_s[:n_earlier])
    x_new, s_new, ctail = pl.pallas_call(
        functools.partial(_odd_layer_kernel, nb=nb, tt=tt, has_state=has_state, prep_chunks=prep_chunks,
                          n_earlier=n_earlier, stacked=last),
        grid=(b // nb, t // tt), in_specs=in_specs,
        out_specs=[pl.BlockSpec((nb, tt, d), lambda bi, i: (bi, i, 0)),
                   s_spec,
                   pl.BlockSpec((nb, SUBLANES, 3 * w), lambda bi, i: (bi, 0, 0))],
        out_shape=[jax.ShapeDtypeStruct((b, t, d), F32),
                   s_shape,
                   jax.ShapeDtypeStruct((b, SUBLANES, 3 * w), F32)],
        scratch_shapes=[pltpu.VMEM((nb, 3, SUBLANES, w), F32),
                        pltpu.VMEM((n_rows, w), F32), pltpu.VMEM((n_rows, w), F32), pltpu.VMEM((n_rows, w), F32),
                        pltpu.VMEM((n_rows, w), BF16),
                        pltpu.VMEM((n_rows, LANES), F32), pltpu.VMEM((n_rows, LANES), F32),
                        pltpu.VMEM((nb, n_heads, HD_C, HD_C), F32),
                        pltpu.VMEM((n_rows, w), BF16),
                        pltpu.VMEM((n_rows, w), BF16), pltpu.VMEM((n_rows, w), BF16), pltpu.VMEM((n_rows, w), F32),
                        pltpu.VMEM((n_heads // 2, n_rows, 2 * CHUNK), BF16),
                        pltpu.VMEM((n_heads, n_chunks, HD_C, CHUNK), BF16),
                        pltpu.VMEM((n_chunks, n_heads, HD_C), F32)],
        compiler_params=_params(), name="odd_layer")(*args)
    return x_new, ctail[:, SUBLANES - 3:], s_new


def _cast_kernel(w_ref, *rest):
    if len(rest) == 2:
        scale_ref, o_ref = rest
        o_ref[...] = (w_ref[...] * scale_ref[...]).astype(o_ref.dtype)
    else:
        (o_ref,) = rest
        o_ref[...] = w_ref[...].astype(o_ref.dtype)


def _to_bf16(w_all, layer, n_cols, col_scale=None):
    k = w_all.shape[1]
    assert n_cols % CAST_COLS == 0
    in_specs = [pl.BlockSpec((None, k, CAST_COLS), lambda j: (layer, 0, j))]
    args = [w_all]
    if col_scale is not None:
        in_specs.append(pl.BlockSpec((1, CAST_COLS), lambda j: (0, j)))
        args.append(col_scale.reshape(1, n_cols))
    return pl.pallas_call(
        _cast_kernel, grid=(n_cols // CAST_COLS,), in_specs=in_specs,
        out_specs=pl.BlockSpec((k, CAST_COLS), lambda j: (0, j)),
        out_shape=jax.ShapeDtypeStruct((k, n_cols), BF16),
        compiler_params=pltpu.CompilerParams(dimension_semantics=("parallel",)), name="to_bf16")(*args)


def _rows_to_bf16(wt_all, layer, n_rows):
    k = wt_all.shape[2]
    assert n_rows % CAST_COLS == 0
    return pl.pallas_call(
        _cast_kernel, grid=(n_rows // CAST_COLS,),
        in_specs=[pl.BlockSpec((None, CAST_COLS, k), lambda j: (layer, j, 0))],
        out_specs=pl.BlockSpec((CAST_COLS, k), lambda j: (j, 0)),
        out_shape=jax.ShapeDtypeStruct((n_rows, k), BF16),
        compiler_params=pltpu.CompilerParams(dimension_semantics=("parallel",)), name="rows_to_bf16")(wt_all)


def _gate_cast_kernel(w_ref, o_ref):
    n_gate = w_ref.shape[0]
    o_ref[0:n_gate, :] = w_ref[...].astype(o_ref.dtype)
    o_ref[n_gate:, :] = jnp.zeros((o_ref.shape[0] - n_gate, o_ref.shape[1]), o_ref.dtype)


def _gate_rows_to_bf16(wt_all, layer, first_row, n_gate):
    k = wt_all.shape[2]
    assert first_row % n_gate == 0 and n_gate % (2 * SUBLANES) == 0 and first_row + n_gate == wt_all.shape[1]
    return pl.pallas_call(
        _gate_cast_kernel, grid=(1,),
        in_specs=[pl.BlockSpec((None, n_gate, k), lambda j: (layer, first_row // n_gate, 0))],
        out_specs=pl.BlockSpec((LANES, k), lambda j: (0, 0)),
        out_shape=jax.ShapeDtypeStruct((LANES, k), BF16), name="gate_to_bf16")(wt_all)


def _band_bias(table):
    n_heads = table.shape[0]
    assert CHUNK - 1 <= REL_CLIP < BAND - 1
    low = table[:, REL_CLIP - (CHUNK - 1):]
    high = jnp.broadcast_to(table[:, 2 * REL_CLIP:], (n_heads, BAND - 1 - REL_CLIP))
    r = jnp.concatenate([low, high], axis=1)[:, ::-1].astype(F32) * LOG2E
    length = r.shape[1]
    padded = jnp.pad(r, ((0, 0), (0, 1)))
    shifted = jnp.tile(padded, (1, CHUNK))[:, :CHUNK * length].reshape(n_heads, CHUNK, length)
    bias = shifted[:, :, CHUNK - 1:CHUNK - 1 + BAND]
    return bias.reshape(n_heads // 2, 2 * CHUNK, BAND)


def kernel(x_prompt, x_sample, cache_conv_a, cache_k_b, cache_v_b, state_conv_c, state_s_c, norm_pre, norm_post, w_in_even, conv_w_a, rel_bias_b, w_out_even, w_in_odd, conv_w_c, a_log_c, dt_bias_c, out_norm_c, w_out_odd):
    depth = norm_pre.shape[0]
    n_even, n_odd = w_in_even.shape[0], w_in_odd.shape[0]
    n_heads_b = cache_k_b.shape[-2]
    kc_all = cache_k_b.reshape(cache_k_b.shape[:3] + (-1,)).astype(BF16)
    vc_all = cache_v_b.reshape(cache_v_b.shape[:3] + (-1,)).astype(BF16)
    yp, ys = x_prompt, x_sample
    conv_a_p, conv_a_s, conv_c_p, conv_c_s = [], [], [], []
    kv_p, kv_s, s_p, s_s = [], [], [], []
    for layer in range(depth):
        i = layer // 2
        g_pre, g_post = norm_pre[layer], norm_post[layer]
        if layer % 2 == 0:
            n_in = w_in_even.shape[2]
            w_blk = n_in // 8
            col_scale = jnp.ones((n_in,), F32).at[4 * w_blk:5 * w_blk].set(HD_B ** -0.5 * LOG2E)
            w_in_bf = _to_bf16(w_in_even, i, n_in, col_scale)
            w_out_bf = _to_bf16(w_out_even, i, w_out_even.shape[2])
            shared = (g_pre, g_post, w_in_bf, conv_w_a[i], _band_bias(rel_bias_b[i]), w_out_bf, i, n_even)
            yp, hst, kv = _even_layer(yp, None, None, None, *shared, kv_p)
            kv_p.append(kv)
            conv_a_p.append(hst)
            ys, hst, kv = _even_layer(ys, cache_conv_a, kc_all, vc_all, *shared, kv_s)
            kv_s.append(kv)
            conv_a_s.append(hst)
        else:
            n_heads = a_log_c.shape[1]
            n_main = w_in_odd.shape[2] - 2 * n_heads
            w_in_odd_t = jnp.swapaxes(w_in_odd, 1, 2)
            w_main_bf = _rows_to_bf16(w_in_odd_t, i, n_main)
            w_gate_bf = _gate_rows_to_bf16(w_in_odd_t, i, n_main, 2 * n_heads)
            pad = (n_heads, LANES - 2 * n_heads)
            ab = jnp.stack([jnp.pad(a_log_c[i], pad), jnp.pad(dt_bias_c[i], pad)]).astype(F32)
            shared = (g_pre, g_post, w_main_bf, w_gate_bf, conv_w_c[i], ab, out_norm_c[i].reshape(1, -1),
                      _to_bf16(w_out_odd, i, w_out_odd.shape[2]), i, n_odd)
            yp, hst, s_new = _odd_layer(yp, None, None, *shared, s_p)
            s_p.append(s_new)
            conv_c_p.append(hst)
            ys, hst, s_new = _odd_layer(ys, state_conv_c, state_s_c, *shared, s_s)
            s_s.append(s_new)
            conv_c_s.append(hst)

    def heads_apart(kv):
        return kv.reshape(kv.shape[:-1] + (n_heads_b, kv.shape[-1] // n_heads_b))

    return (yp, ys,
            jnp.stack(conv_a_p), heads_apart(kv_p[-1][0]), heads_apart(kv_p[-1][1]), jnp.stack(conv_c_p), s_p[-1],
            jnp.stack(conv_a_s), heads_apart(kv_s[-1][0]), heads_apart(kv_s[-1][1]), jnp.stack(conv_c_s), s_s[-1])
```

```python
import functools
import itertools
import math

import jax
import jax.numpy as jnp
from jax import lax
from jax.experimental import pallas as pl
from jax.experimental.pallas import tpu as pltpu

F32 = jnp.float32
BF16 = jnp.bfloat16

EPS = 1e-6
LOG2E = math.log2(math.e)
CHUNK = 64
BAND_PAST = 8 * CHUNK
BAND = BAND_PAST + CHUNK
REL_CLIP = 4 * CHUNK
HD_B = 64
HD_C = 128
LANES = 128
SUBLANES = 8
VMEM_LIMIT_BYTES = 56 * 1024 * 1024
STATE_VMEM_BUDGET_BYTES = 18 * 1024 * 1024
ROW_TILE = 512
CAST_COLS = 512
PROJ_BLOCKS = 4


def _silu(x):
    return x * (1.0 / (1.0 + jnp.exp(-x)))


def _softplus(x):
    return jnp.maximum(x, 0.0) + jnp.log1p(jnp.exp(-jnp.abs(x)))


def _dot(a, b):
    return jnp.dot(a.astype(BF16), b.astype(BF16), preferred_element_type=F32)


def _dot_t(a, b_t):
    return lax.dot_general(a, b_t, (((1,), (1,)), ((), ())), preferred_element_type=F32)


def _load_rows(ref, nb):
    parts = [ref[s] for s in range(nb)]
    return parts[0] if nb == 1 else jnp.concatenate(parts, axis=0)


def _pre_norm(x_ref, gain_ref, nb):
    x = _load_rows(x_ref, nb)
    ms = jnp.mean(x * x, axis=-1, keepdims=True)
    return (x * lax.rsqrt(ms + EPS) * gain_ref[...]).astype(BF16)


def _project_out(ybuf, wout_ref, gain_ref, x_ref, xo_ref):
    nb, tt, _ = x_ref.shape
    out = jnp.dot(ybuf[...], wout_ref[...], preferred_element_type=F32)
    ms = jnp.mean(out * out, axis=-1, keepdims=True)
    delta = out * lax.rsqrt(ms + EPS) * gain_ref[...]
    for s in range(nb):
        xo_ref[s] = x_ref[s] + delta[s * tt:(s + 1) * tt]


def _causal_dwconv(prev_rows, new_rows, taps):
    tt = new_rows.shape[0]
    width = taps.shape[0]
    assert width <= SUBLANES
    full = jnp.concatenate([prev_rows, new_rows], axis=0)
    sums = [None, None]
    shifted = full
    for delay in range(width):
        if delay and delay % 2 == 0:
            shifted = pltpu.roll(shifted, 2, axis=0)
        term = shifted * taps[width - 1 - delay:width - delay]
        sums[delay % 2] = term if sums[delay % 2] is None else sums[delay % 2] + term
    acc = sums[0] if sums[1] is None else sums[0] + pltpu.roll(sums[1], 1, axis=0)
    return acc[SUBLANES:, :], full[tt:tt + SUBLANES, :]


def _streams_per_step(b, t, state_bytes_per_stream):
    tt = min(t, ROW_TILE)
    nb = max(1, min(ROW_TILE // tt, STATE_VMEM_BUDGET_BYTES // state_bytes_per_stream))
    while b % nb:
        nb -= 1
    return nb, tt


def _resident(shape):
    return pl.BlockSpec(shape, lambda bi, i: (0,) * len(shape), pipeline_mode=pl.Buffered(1))


def _params():
    return pltpu.CompilerParams(dimension_semantics=("parallel", "arbitrary"), vmem_limit_bytes=VMEM_LIMIT_BYTES)


def _even_layer_kernel(*refs, nb, tt, has_cache, n_earlier, stacked):
    x_ref, gpre_ref, win_ref, cw_ref, bias_ref, wout_ref, gpost_ref = refs[:7]
    if has_cache:
        hist_ref, kc_ref, vc_ref = refs[7:10]
        refs = refs[10:]
    else:
        refs = refs[7:]
    earlier = refs[:2 * n_earlier]
    refs = refs[2 * n_earlier:]
    xo_ref, utail_ref, knew_ref, vnew_ref, utails, kbuf, vbuf, qbuf, zbuf, ybuf = refs
    w = win_ref.shape[-1] // 8
    i = pl.program_id(1)
    if stacked:
        @pl.when(i == pl.num_programs(1) - 1)
        def _pass_through():
            for j in range(n_earlier):
                knew_ref[j] = earlier[2 * j][...]
                vnew_ref[j] = earlier[2 * j + 1][...]
        knew_ref, vnew_ref = knew_ref.at[n_earlier], vnew_ref.at[n_earlier]

    @pl.when(i == 0)
    def _init():
        utails[...] = jnp.zeros(utails.shape, F32)
        if has_cache:
            for s in range(nb):
                utails[s, SUBLANES - 2:SUBLANES, :] = hist_ref[s]
            kbuf[:, 0:BAND_PAST, :] = kc_ref[...]
            vbuf[:, 0:BAND_PAST, :] = vc_ref[...]
        else:
            kbuf[:, 0:BAND_PAST, :] = jnp.zeros((nb, BAND_PAST, w), BF16)
            vbuf[:, 0:BAND_PAST, :] = jnp.zeros((nb, BAND_PAST, w), BF16)

    hb = _pre_norm(x_ref, gpre_ref, nb)

    def proj(j):
        return jnp.dot(hb, win_ref[:, j * w:(j + 1) * w], preferred_element_type=F32)

    u = proj(1) * proj(2)
    gate = proj(0) * _silu(proj(3))
    for s in range(nb):
        srows = slice(s * tt, (s + 1) * tt)
        conv, tail = _causal_dwconv(utails[s], u[srows], cw_ref[...])
        utail_ref[s] = tail
        utails[s] = tail
        ybuf[srows, 0:w] = (gate[srows] * conv).astype(ybuf.dtype)

    qbuf[...] = proj(4).astype(qbuf.dtype)
    k = proj(5)
    v = proj(6)
    for s in range(nb):
        srows = slice(s * tt, (s + 1) * tt)
        knew_ref[s] = k[srows]
        vnew_ref[s] = v[srows]
        kbuf[s, BAND_PAST:BAND_PAST + tt, :] = k[srows].astype(kbuf.dtype)
        vbuf[s, BAND_PAST:BAND_PAST + tt, :] = v[srows].astype(vbuf.dtype)
    zbuf[...] = _silu(proj(7)).astype(zbuf.dtype)

    chunks_per_stream = tt // CHUNK
    n_pairs = w // LANES
    lane = lax.broadcasted_iota(jnp.int32, (CHUNK, LANES), 1)
    first_head = lane < HD_B

    def attend(first_tile):
        def band_start(c):
            missing = max(BAND_PAST - CHUNK * c, 0) if first_tile else 0
            return missing // LANES * LANES, missing % LANES

        def scores(s, c, hp):
            cols = slice(hp * LANES, (hp + 1) * LANES)
            lo, masked = band_start(c)
            r0 = s * tt + c * CHUNK
            qp = qbuf[r0:r0 + CHUNK, cols]
            zero = jnp.zeros_like(qp)
            q2 = jnp.concatenate([jnp.where(first_head, qp, zero), jnp.where(first_head, zero, qp)], axis=0)
            kp = kbuf[s, c * CHUNK + lo:c * CHUNK + BAND, cols]
            sc = lax.dot_general(q2, kp, (((1,), (1,)), ((), ())), preferred_element_type=F32)
            sc = sc + bias_ref[hp, :, lo:BAND]
            if masked:
                key_idx = lax.broadcasted_iota(jnp.int32, sc.shape, 1)
                sc = jnp.where(key_idx >= masked, sc, -jnp.inf)
            return sc

        def weighted_values(s, c, hp, sc):
            cols = slice(hp * LANES, (hp + 1) * LANES)
            lo, _ = band_start(c)
            m = jnp.max(sc, axis=-1, keepdims=True)
            e = jnp.exp2(sc - m)
            l = jnp.sum(e, axis=-1, keepdims=True)
            vp = vbuf[s, c * CHUNK + lo:c * CHUNK + BAND, cols]
            pv = jnp.dot(e.astype(BF16), vp, preferred_element_type=F32) / l
            return jnp.where(first_head, pv[0:CHUNK], pv[CHUNK:2 * CHUNK])

        items = [(s, c, hp) for s in range(nb) for c in range(chunks_per_stream) for hp in range(n_pairs)]
        skew = 3
        pending, outs = {}, {}
        for step in range(len(items) + skew):
            if step < len(items):
                pending[step] = scores(*items[step])
            j = step - skew
            if j >= 0:
                s, c, hp = items[j]
                outs[hp] = weighted_values(s, c, hp, pending.pop(j))
                if hp == n_pairs - 1:
                    rows = slice(s * tt + c * CHUNK, s * tt + (c + 1) * CHUNK)
                    o = jnp.concatenate([outs[x] for x in range(n_pairs)], axis=1)
                    ybuf[rows, w:2 * w] = (o * zbuf[rows, :].astype(F32)).astype(ybuf.dtype)

    if has_cache:
        attend(False)
    else:
        pl.when(i == 0)(functools.partial(attend, True))
        pl.when(i > 0)(functools.partial(attend, False))
    for s in range(nb):
        kbuf[s, 0:BAND_PAST, :] = kbuf[s, tt:tt + BAND_PAST, :]
        vbuf[s, 0:BAND_PAST, :] = vbuf[s, tt:tt + BAND_PAST, :]

    _project_out(ybuf, wout_ref, gpost_ref, x_ref, xo_ref)


def _layer_slot_in(stacked, nb, slot):
    rest = stacked.shape[2:]
    return pl.BlockSpec((None, nb) + rest, lambda bi, i: (slot, bi) + (0,) * len(rest))


def _per_stream_out(shape, nb, earlier, last):
    zeros = (0,) * len(shape)
    own = pl.BlockSpec((nb,) + shape, lambda bi, i: (bi,) + zeros)
    if not last:
        return own, None, []
    n_layers = len(earlier) + 1
    spec = pl.BlockSpec((n_layers, nb) + shape, lambda bi, i: (0, bi) + zeros)
    return spec, n_layers, [own] * len(earlier)


def _even_layer(x, hist, kc, vc, g_pre, g_post, w_in_bf, conv_w, bias2, w_out_bf, slot, n_slots, earlier_kv):
    b, t, d = x.shape
    w = w_in_bf.shape[1] // 8
    has_cache = kc is not None
    assert t % CHUNK == 0 and (t <= ROW_TILE or t % ROW_TILE == 0)
    band_bytes = 2 * (BAND_PAST + min(t, ROW_TILE)) * w * 2
    state_bytes = band_bytes + (2 * 2 * BAND_PAST * w * 2 if has_cache else 0)
    nb, tt = _streams_per_step(b, t, state_bytes)
    n_rows = nb * tt
    consts = [g_pre.reshape(1, d), w_in_bf, conv_w, bias2, w_out_bf, g_post.reshape(1, d)]
    in_specs = [pl.BlockSpec((nb, tt, d), lambda bi, i: (bi, i, 0))] + [_resident(c.shape) for c in consts]
    args = [x] + consts
    if has_cache:
        in_specs += [_layer_slot_in(hist, nb, slot), _layer_slot_in(kc, nb, slot), _layer_slot_in(vc, nb, slot)]
        args += [hist, kc, vc]
    assert tt == min(BAND_PAST, t)
    last = slot == n_slots - 1
    assert len(earlier_kv) == slot
    kv_spec, n_layers, earlier_specs = _per_stream_out((tt, w), nb, earlier_kv, last)
    kv_shape = jax.ShapeDtypeStruct(((n_layers,) if last else ()) + (b, tt, w), F32)
    n_earlier = len(earlier_specs)
    for k_prev, v_prev in earlier_kv[:n_earlier]:
        in_specs += earlier_specs[:1] * 2
        args += [k_prev, v_prev]
    x_new, utail, k_out, v_out = pl.pallas_call(
        functools.partial(_even_layer_kernel, nb=nb, tt=tt, has_cache=has_cache, n_earlier=n_earlier, stacked=last),
        grid=(b // nb, t // tt), in_specs=in_specs,
        out_specs=[pl.BlockSpec((nb, tt, d), lambda bi, i: (bi, i, 0)),
                   pl.BlockSpec((nb, SUBLANES, w), lambda bi, i: (bi, 0, 0)),
                   kv_spec, kv_spec],
        out_shape=[jax.ShapeDtypeStruct((b, t, d), F32),
                   jax.ShapeDtypeStruct((b, SUBLANES, w), F32),
                   kv_shape, kv_shape],
        scratch_shapes=[pltpu.VMEM((nb, SUBLANES, w), F32),
                        pltpu.VMEM((nb, BAND_PAST + tt, w), BF16), pltpu.VMEM((nb, BAND_PAST + tt, w), BF16),
                        pltpu.VMEM((n_rows, w), BF16), pltpu.VMEM((n_rows, w), BF16),
                        pltpu.VMEM((n_rows, 2 * w), BF16)],
        compiler_params=_params(), name="even_layer")(*args)
    return x_new, utail[:, SUBLANES - 2:], (k_out, v_out)


def _cumsum_rows(x):
    n = x.shape[0]
    row = lax.broadcasted_iota(jnp.int32, x.shape, 0)
    shift = 1
    while shift < n:
        x = x + jnp.where(row >= shift, pltpu.roll(x, shift, axis=0), 0.0)
        shift *= 2
    return x


def _pair_masks():
    row = lax.broadcasted_iota(jnp.int32, (CHUNK, 2 * CHUNK), 0)
    lane = lax.broadcasted_iota(jnp.int32, (CHUNK, 2 * CHUNK), 1)
    first = lane < CHUNK
    col = jnp.where(first, lane, lane - CHUNK)
    return first, row, col


def _block_diag(pk, first):
    tiled = jnp.concatenate([pk, pk], axis=0).astype(BF16)
    keep = jnp.concatenate([first, jnp.logical_not(first)], axis=0)
    return jnp.where(keep, tiled, jnp.zeros_like(tiled))


def _unit_lower_inverses(mats, out, first, row, col):
    n = mats[0].shape[0]
    eye = jnp.where(row == col, 1.0, 0.0)
    xs = [-a for a in mats]
    ps = [eye + x for x in xs]
    ys = [jnp.dot(x.astype(BF16), _block_diag(x, first), preferred_element_type=F32) for x in xs]
    yield
    power = 2
    while 2 * power < n:
        sts = [jnp.dot(jnp.concatenate([p, y], axis=0).astype(BF16), _block_diag(y, first),
                       preferred_element_type=F32) for p, y in zip(ps, ys)]
        ps = [p + st[0:n] for p, st in zip(ps, sts)]
        ys = [st[n:2 * n] for st in sts]
        power *= 2
        yield
    out[:] = [p + jnp.dot(p.astype(BF16), _block_diag(y, first), preferred_element_type=F32)
              for p, y in zip(ps, ys)]
    yield


def _weave(main, fillers, period):
    filler = itertools.chain(*fillers)
    for n, _ in enumerate(main, start=1):
        if n % period == 0:
            next(filler, None)
    for _ in filler:
        pass


def _interleave(*generators):
    live = list(generators)
    while live:
        for g in list(live):
            try:
                next(g)
            except StopIteration:
                live.remove(g)


def _odd_layer_kernel(*refs, nb, tt, has_state, prep_chunks, n_earlier, stacked):
    x_ref, gpre_ref, win_ref, wg_ref, cw_ref, ab_ref, on_ref, wout_ref, gpost_ref = refs[:9]
    if has_state:
        hist_ref, s0_ref = refs[9:11]
        refs = refs[11:]
    else:
        refs = refs[9:]
    earlier = refs[:n_earlier]
    refs = refs[n_earlier:]
    (xo_ref, sout_ref, ctail_ref, ctails, qs, ks, vs, zs, gsc, bsc, sbuf, ybuf,
     wbuf, qtbuf, u0buf, qkbuf, kdtbuf, eglbuf) = refs
    w = win_ref.shape[0] // 4
    n_heads = w // HD_C
    n_rows = nb * tt
    i = pl.program_id(1)

    @pl.when(i == 0)
    def _init():
        ctails[...] = jnp.zeros(ctails.shape, F32)
        if has_state:
            for s in range(nb):
                for g in range(3):
                    ctails[s, g, SUBLANES - 3:SUBLANES, :] = hist_ref[s, :, g * w:(g + 1) * w]
            sbuf[...] = s0_ref[...]
        else:
            sbuf[...] = jnp.zeros(sbuf.shape, F32)

    hb = _pre_norm(x_ref, gpre_ref, nb)

    gl = _dot_t(hb, wg_ref[...])
    bsc[...] = 1.0 / (1.0 + jnp.exp(-gl))
    gsc[...] = -jnp.exp(ab_ref[0:1, :]) * _softplus(gl + ab_ref[1:2, :])

    def project_group(g, dst, n_blocks):
        inv_scale_sq = float(HD_C) if g == 0 else 1.0
        sum_mat = jnp.full((HD_C, HD_C), inv_scale_sq, BF16)
        bw = w // n_blocks
        for blk in range(n_blocks):
            gcols = slice(g * w + blk * bw, g * w + (blk + 1) * bw)
            raw = _dot_t(hb, win_ref[gcols, :])
            for s in range(nb):
                srows = slice(s * tt, (s + 1) * tt)
                c, tail = _causal_dwconv(ctails[s, g, :, blk * bw:(blk + 1) * bw], raw[srows], cw_ref[:, gcols])
                c = _silu(c)
                ctail_ref[s, :, gcols] = tail
                ctails[s, g, :, blk * bw:(blk + 1) * bw] = tail
                if g == 2:
                    dst[srows, blk * bw:(blk + 1) * bw] = c
                else:
                    for h in range(bw // HD_C):
                        ch = c[:, h * HD_C:(h + 1) * HD_C]
                        ss = jnp.dot((ch * ch).astype(BF16), sum_mat, preferred_element_type=F32)
                        dst[srows, blk * bw + h * HD_C:blk * bw + (h + 1) * HD_C] = ch * lax.rsqrt(
                            ss + EPS * inv_scale_sq)
            yield

    def project_gate(n_blocks):
        bw = w // n_blocks
        for blk in range(n_blocks):
            z = _dot_t(hb, win_ref[3 * w + blk * bw:3 * w + (blk + 1) * bw, :])
            zs[:, blk * bw:(blk + 1) * bw] = _silu(z).astype(zs.dtype)
            yield

    _interleave(project_group(0, qs, 1))
    _interleave(project_group(1, ks, 1))

    first, row, col = _pair_masks()
    incl = row >= col
    strict = row > col
    assert n_heads % 2 == 0 and 2 * CHUNK == HD_C
    pairs = range(n_heads // 2)

    heads = range(n_heads)
    hcols = [slice(h * HD_C, (h + 1) * HD_C) for h in heads]
    n_chunks = n_rows // CHUNK

    def chunk_rows(cidx):
        if isinstance(cidx, int):
            return slice(cidx * CHUNK, (cidx + 1) * CHUNK)
        return pl.ds(pl.multiple_of(cidx * CHUNK, CHUNK), CHUNK)

    def prepare_chunks(step):
        items = []
        duos = []
        for j in range(prep_chunks):
            cidx = step * prep_chunks + j
            rows = chunk_rows(cidx)
            gc = _cumsum_rows(gsc[rows, :])
            gct = gc.T
            bet = bsc[rows, :]
            for h in heads:
                gcol = jnp.broadcast_to(gc[:, n_heads + h:n_heads + h + 1], (CHUNK, HD_C))
                items.append(dict(cidx=cidx, rows=rows, h=h, gcol=gcol,
                                  b=jnp.broadcast_to(bet[:, h:h + 1], (CHUNK, HD_C))))
            for p in pairs:
                ia, ib = items[-n_heads + 2 * p], items[-n_heads + 2 * p + 1]
                g0 = n_heads + 2 * p
                grow = jnp.concatenate([gct[g0:g0 + 1, :], gct[g0 + 1:g0 + 2, :]], axis=1)
                gcol2 = jnp.where(first, ia["gcol"], ib["gcol"])
                duos.append(dict(cidx=cidx, rows=rows, p=p, a=ia, b=ib, beta=jnp.where(first, ia["b"], ib["b"]),
                                 decay=jnp.exp(jnp.where(incl, gcol2 - grow, -jnp.inf))))
        for it in items:
            rows, c = it["rows"], hcols[it["h"]]
            it["eg"] = jnp.exp(it["gcol"])
            it["q"], it["k"] = qs[rows, c], ks[rows, c]
        for it in items:
            kb = it["k"].astype(BF16)
            it["qk_kk"] = lax.dot_general(jnp.concatenate([it["q"].astype(BF16), kb], axis=0), kb,
                                          (((1,), (1,)), ((), ())), preferred_element_type=F32)
        yield
        for du in duos:
            qk_kk = jnp.concatenate([du["a"]["qk_kk"], du["b"]["qk_kk"]], axis=1)
            du["qk"] = qk_kk[0:CHUNK] * du["decay"]
            du["mat"] = jnp.where(strict, du["beta"] * qk_kk[CHUNK:2 * CHUNK] * du["decay"], 0.0)
        tinvs = [None] * len(duos)
        yield from _unit_lower_inverses([du["mat"] for du in duos], tinvs, first, row, col)
        sols = []
        for tinv, du in zip(tinvs, duos):
            rhs = jnp.concatenate(
                [jnp.concatenate([vs[it["rows"], hcols[it["h"]]] * it["b"], it["k"] * (it["b"] * it["eg"])], axis=1)
                 for it in (du["a"], du["b"])], axis=0)
            sols.append(jnp.dot(_block_diag(tinv, first), rhs.astype(BF16), preferred_element_type=F32))
        yield
        for sol, du in zip(sols, duos):
            rows = du["rows"]
            qkbuf[du["p"], rows, :] = du["qk"].astype(BF16)
            for half, it in enumerate((du["a"], du["b"])):
                gcol, c, h = it["gcol"], hcols[it["h"]], it["h"]
                g_last = gcol[CHUNK - 1:CHUNK, :]
                sol_h = sol[half * CHUNK:(half + 1) * CHUNK]
                u0buf[rows, c] = sol_h[:, 0:HD_C]
                wbuf[rows, c] = sol_h[:, HD_C:].astype(BF16)
                qtbuf[rows, c] = (it["q"] * it["eg"]).astype(BF16)
                kdtbuf[h, it["cidx"]] = (it["k"] * jnp.exp(g_last - gcol)).T.astype(BF16)
                eglbuf[it["cidx"], h:h + 1, :] = jnp.exp(g_last)

    chunks_per_stream = tt // CHUNK

    def recur_chunks(step):
        cidxs = [step * prep_chunks + j for j in range(prep_chunks)]
        for batch in ([cidxs] if chunks_per_stream == 1 else [[c] for c in cidxs]):
            work = []
            for cidx in batch:
                rows = chunk_rows(cidx)
                s = 0 if nb == 1 else cidx // chunks_per_stream
                s_old = [sbuf[s, h] for h in heads]
                ws_qs = [_dot(jnp.concatenate([wbuf[rows, hcols[h]], qtbuf[rows, hcols[h]]], axis=0), s_old[h])
                         for h in heads]
                work.append(dict(cidx=cidx, rows=rows, s=s, s_old=s_old, ws_qs=ws_qs))
            yield
            for it in work:
                cidx, rows, s, s_old, ws_qs = it["cidx"], it["rows"], it["s"], it["s_old"], it["ws_qs"]
                u = [u0buf[rows, hcols[h]] - ws_qs[h][0:CHUNK] for h in heads]
                qku = [jnp.dot(_block_diag(qkbuf[p, rows, :], first),
                               jnp.concatenate([u[2 * p], u[2 * p + 1]], axis=0).astype(BF16),
                               preferred_element_type=F32) for p in pairs]
                it["o"] = [ws_qs[h][CHUNK:2 * CHUNK] + qku[h // 2][(h % 2) * CHUNK:(h % 2 + 1) * CHUNK]
                           for h in heads]
                for h in heads:
                    sbuf[s, h] = eglbuf[cidx, h:h + 1, :] * s_old[h] + _dot(kdtbuf[h, cidx], u[h])
            yield
            for it in work:
                for h in heads:
                    o = it["o"][h]
                    ms = jnp.mean(o * o, axis=-1, keepdims=True)
                    on = o * lax.rsqrt(ms + EPS) * on_ref[...]
                    ybuf[it["rows"], hcols[h]] = (on * zs[it["rows"], hcols[h]].astype(F32)).astype(ybuf.dtype)

    n_steps = n_chunks // prep_chunks
    _weave(prepare_chunks(0), [project_group(2, vs, PROJ_BLOCKS), project_gate(PROJ_BLOCKS)], period=1)

    def pipelined(step, carry):
        _interleave(prepare_chunks(step), recur_chunks(step - 1))
        return carry

    lax.fori_loop(1, n_steps, pipelined, 0)
    _interleave(recur_chunks(n_steps - 1))
    if stacked:
        @pl.when(i == pl.num_programs(1) - 1)
        def _pass_through():
            for j in range(n_earlier):
                sout_ref[j] = earlier[j][...]
        sout_ref[n_earlier] = sbuf[...]
    else:
        sout_ref[...] = sbuf[...]
    _project_out(ybuf, wout_ref, gpost_ref, x_ref, xo_ref)


def _odd_layer(x, hist, s0, g_pre, g_post, w_in_bf, w_gate_bf, conv_w, ab, onorm, w_out_bf, slot, n_slots, earlier_s):
    b, t, d = x.shape
    w = w_in_bf.shape[0] // 4
    n_heads = w // HD_C
    has_state = s0 is not None
    last = slot == n_slots - 1
    assert len(earlier_s) == slot
    assert t % CHUNK == 0 and (t <= ROW_TILE or t % ROW_TILE == 0)
    state_copies = 1 + 2 * (n_slots if last else 1) + 2 * (slot if last else 0) + (2 if has_state else 0)
    nb, tt = _streams_per_step(b, t, n_heads * HD_C * HD_C * 4 * state_copies)
    n_rows = nb * tt
    n_chunks = n_rows // CHUNK
    prep_chunks = max(c for c in (1, 2, 4) if n_chunks % c == 0)
    consts = [g_pre.reshape(1, d), w_in_bf, w_gate_bf, conv_w, ab, onorm, w_out_bf, g_post.reshape(1, d)]
    in_specs = [pl.BlockSpec((nb, tt, d), lambda bi, i: (bi, i, 0))] + [_resident(c.shape) for c in consts]
    args = [x] + consts
    if has_state:
        in_specs += [_layer_slot_in(hist, nb, slot), _layer_slot_in(s0, nb, slot)]
        args += [hist, s0]
    s_spec, n_layers, earlier_specs = _per_stream_out((n_heads, HD_C, HD_C), nb, earlier_s, last)
    s_shape = jax.ShapeDtypeStruct(((n_layers,) if last else ()) + (b, n_heads, HD_C, HD_C), F32)
    n_earlier = len(earlier_specs)
    in_specs += earlier_specs
    args += list(earlier_s[:n_earlier])
    x_new, s_new, ctail = pl.pallas_call(
        functools.partial(_odd_layer_kernel, nb=nb, tt=tt, has_state=has_state, prep_chunks=prep_chunks,
                          n_earlier=n_earlier, stacked=last),
        grid=(b // nb, t // tt), in_specs=in_specs,
        out_specs=[pl.BlockSpec((nb, tt, d), lambda bi, i: (bi, i, 0)),
                   s_spec,
                   pl.BlockSpec((nb, SUBLANES, 3 * w), lambda bi, i: (bi, 0, 0))],
        out_shape=[jax.ShapeDtypeStruct((b, t, d), F32),
                   s_shape,
                   jax.ShapeDtypeStruct((b, SUBLANES, 3 * w), F32)],
        scratch_shapes=[pltpu.VMEM((nb, 3, SUBLANES, w), F32),
                        pltpu.VMEM((n_rows, w), F32), pltpu.VMEM((n_rows, w), F32), pltpu.VMEM((n_rows, w), F32),
                        pltpu.VMEM((n_rows, w), BF16),
                        pltpu.VMEM((n_rows, LANES), F32), pltpu.VMEM((n_rows, LANES), F32),
                        pltpu.VMEM((nb, n_heads, HD_C, HD_C), F32),
                        pltpu.VMEM((n_rows, w), BF16),
                        pltpu.VMEM((n_rows, w), BF16), pltpu.VMEM((n_rows, w), BF16), pltpu.VMEM((n_rows, w), F32),
                        pltpu.VMEM((n_heads // 2, n_rows, 2 * CHUNK), BF16),
                        pltpu.VMEM((n_heads, n_chunks, HD_C, CHUNK), BF16),
                        pltpu.VMEM((n_chunks, n_heads, HD_C), F32)],
        compiler_params=_params(), name="odd_layer")(*args)
    return x_new, ctail[:, SUBLANES - 3:], s_new


def _cast_kernel(w_ref, *rest):
    if len(rest) == 2:
        scale_ref, o_ref = rest
        o_ref[...] = (w_ref[...] * scale_ref[...]).astype(o_ref.dtype)
    else:
        (o_ref,) = rest
        o_ref[...] = w_ref[...].astype(o_ref.dtype)


def _to_bf16(w_all, layer, n_cols, col_scale=None):
    k = w_all.shape[1]
    assert n_cols % CAST_COLS == 0
    in_specs = [pl.BlockSpec((None, k, CAST_COLS), lambda j: (layer, 0, j))]
    args = [w_all]
    if col_scale is not None:
        in_specs.append(pl.BlockSpec((1, CAST_COLS), lambda j: (0, j)))
        args.append(col_scale.reshape(1, n_cols))
    return pl.pallas_call(
        _cast_kernel, grid=(n_cols // CAST_COLS,), in_specs=in_specs,
        out_specs=pl.BlockSpec((k, CAST_COLS), lambda j: (0, j)),
        out_shape=jax.ShapeDtypeStruct((k, n_cols), BF16),
        compiler_params=pltpu.CompilerParams(dimension_semantics=("parallel",)), name="to_bf16")(*args)


def _rows_to_bf16(wt_all, layer, n_rows):
    k = wt_all.shape[2]
    assert n_rows % CAST_COLS == 0
    return pl.pallas_call(
        _cast_kernel, grid=(n_rows // CAST_COLS,),
        in_specs=[pl.BlockSpec((None, CAST_COLS, k), lambda j: (layer, j, 0))],
        out_specs=pl.BlockSpec((CAST_COLS, k), lambda j: (j, 0)),
        out_shape=jax.ShapeDtypeStruct((n_rows, k), BF16),
        compiler_params=pltpu.CompilerParams(dimension_semantics=("parallel",)), name="rows_to_bf16")(wt_all)


def _gate_cast_kernel(w_ref, o_ref):
    n_gate = w_ref.shape[0]
    o_ref[0:n_gate, :] = w_ref[...].astype(o_ref.dtype)
    o_ref[n_gate:, :] = jnp.zeros((o_ref.shape[0] - n_gate, o_ref.shape[1]), o_ref.dtype)


def _gate_rows_to_bf16(wt_all, layer, first_row, n_gate):
    k = wt_all.shape[2]
    assert first_row % n_gate == 0 and n_gate % (2 * SUBLANES) == 0 and first_row + n_gate == wt_all.shape[1]
    return pl.pallas_call(
        _gate_cast_kernel, grid=(1,),
        in_specs=[pl.BlockSpec((None, n_gate, k), lambda j: (layer, first_row // n_gate, 0))],
        out_specs=pl.BlockSpec((LANES, k), lambda j: (0, 0)),
        out_shape=jax.ShapeDtypeStruct((LANES, k), BF16), name="gate_to_bf16")(wt_all)


def _band_bias(table):
    n_heads = table.shape[0]
    assert CHUNK - 1 <= REL_CLIP < BAND - 1
    low = table[:, REL_CLIP - (CHUNK - 1):]
    high = jnp.broadcast_to(table[:, 2 * REL_CLIP:], (n_heads, BAND - 1 - REL_CLIP))
    r = jnp.concatenate([low, high], axis=1)[:, ::-1].astype(F32) * LOG2E
    length = r.shape[1]
    padded = jnp.pad(r, ((0, 0), (0, 1)))
    shifted = jnp.tile(padded, (1, CHUNK))[:, :CHUNK * length].reshape(n_heads, CHUNK, length)
    bias = shifted[:, :, CHUNK - 1:CHUNK - 1 + BAND]
    return bias.reshape(n_heads // 2, 2 * CHUNK, BAND)


def kernel(x_prompt, x_sample, cache_conv_a, cache_k_b, cache_v_b, state_conv_c, state_s_c, norm_pre, norm_post, w_in_even, conv_w_a, rel_bias_b, w_out_even, w_in_odd, conv_w_c, a_log_c, dt_bias_c, out_norm_c, w_out_odd):
    depth = norm_pre.shape[0]
    n_even, n_odd = w_in_even.shape[0], w_in_odd.shape[0]
    n_heads_b = cache_k_b.shape[-2]
    kc_all = cache_k_b.reshape(cache_k_b.shape[:3] + (-1,)).astype(BF16)
    vc_all = cache_v_b.reshape(cache_v_b.shape[:3] + (-1,)).astype(BF16)
    yp, ys = x_prompt, x_sample
    conv_a_p, conv_a_s, conv_c_p, conv_c_s = [], [], [], []
    kv_p, kv_s, s_p, s_s = [], [], [], []
    for layer in range(depth):
        i = layer // 2
        g_pre, g_post = norm_pre[layer], norm_post[layer]
        if layer % 2 == 0:
            n_in = w_in_even.shape[2]
            w_blk = n_in // 8
            col_scale = jnp.ones((n_in,), F32).at[4 * w_blk:5 * w_blk].set(HD_B ** -0.5 * LOG2E)
            w_in_bf = _to_bf16(w_in_even, i, n_in, col_scale)
            w_out_bf = _to_bf16(w_out_even, i, w_out_even.shape[2])
            shared = (g_pre, g_post, w_in_bf, conv_w_a[i], _band_bias(rel_bias_b[i]), w_out_bf, i, n_even)
            yp, hst, kv = _even_layer(yp, None, None, None, *shared, kv_p)
            kv_p.append(kv)
            conv_a_p.append(hst)
            ys, hst, kv = _even_layer(ys, cache_conv_a, kc_all, vc_all, *shared, kv_s)
            kv_s.append(kv)
            conv_a_s.append(hst)
        else:
            n_heads = a_log_c.shape[1]
            n_main = w_in_odd.shape[2] - 2 * n_heads
            w_in_odd_t = jnp.swapaxes(w_in_odd, 1, 2)
            w_main_bf = _rows_to_bf16(w_in_odd_t, i, n_main)
            w_gate_bf = _gate_rows_to_bf16(w_in_odd_t, i, n_main, 2 * n_heads)
            pad = (n_heads, LANES - 2 * n_heads)
            ab = jnp.stack([jnp.pad(a_log_c[i], pad), jnp.pad(dt_bias_c[i], pad)]).astype(F32)
            shared = (g_pre, g_post, w_main_bf, w_gate_bf, conv_w_c[i], ab, out_norm_c[i].reshape(1, -1),
                      _to_bf16(w_out_odd, i, w_out_odd.shape[2]), i, n_odd)
            yp, hst, s_new = _odd_layer(yp, None, None, *shared, s_p)
            s_p.append(s_new)
            conv_c_p.append(hst)
            ys, hst, s_new = _odd_layer(ys, state_conv_c, state_s_c, *shared, s_s)
            s_s.append(s_new)
            conv_c_s.append(hst)

    def heads_apart(kv):
        return kv.reshape(kv.shape[:-1] + (n_heads_b, kv.shape[-1] // n_heads_b))

    return (yp, ys,
            jnp.stack(conv_a_p), heads_apart(kv_p[-1][0]), heads_apart(kv_p[-1][1]), jnp.stack(conv_c_p), s_p[-1],
            jnp.stack(conv_a_s), heads_apart(kv_s[-1][0]), heads_apart(kv_s[-1][1]), jnp.stack(conv_c_s), s_s[-1])
```
